```python
import jax
import jax.numpy as jnp
from jax import lax
import numpy as np

D_MODEL = 2048
BATCH = 8
SEQ = 2048
DEPTH = 2

GRID_W = 64
CTX_LEN = 256
HEAD_DIM = 128
ROPE_THETA = 10000.0
NORM_EPS = 1e-6
N_MOD = 6

A_Q_HEADS = 8
A_KV_HEADS = 2
Q_BLOCK = 128
B_HEADS = 8
RET_CHUNK = 128
RET_DECAY_BASE = 5.0
C_HEADS = 16
C_DK = 128
C_DV = D_MODEL // C_HEADS
HGRN_CHUNK = 16

A_WIDTH = A_Q_HEADS * HEAD_DIM
B_WIDTH = B_HEADS * HEAD_DIM
MIX_WIDTH = A_WIDTH + B_WIDTH
EVEN_KV_SIZES = (A_KV_HEADS * HEAD_DIM, A_KV_HEADS * HEAD_DIM, B_WIDTH, B_WIDTH)
EVEN_SIZES = EVEN_KV_SIZES + (A_WIDTH, B_WIDTH, B_WIDTH)
EVEN_IN = 2 * A_KV_HEADS * HEAD_DIM + 2 * B_WIDTH + A_WIDTH + 2 * B_WIDTH
C_KEY_WIDTH = C_HEADS * C_DK
C_VAL_WIDTH = C_HEADS * C_DV
ODD_STATE_SIZES = (C_KEY_WIDTH, C_KEY_WIDTH, C_VAL_WIDTH)
ODD_SIZES = ODD_STATE_SIZES + (C_KEY_WIDTH, C_VAL_WIDTH)
ODD_IN = 3 * C_KEY_WIDTH + 2 * C_VAL_WIDTH

N_EXPERTS = 32
TOP_K = 4
D_EXPERT = D_MODEL
SWIGLU_ALPHA = 1.702
SWIGLU_LIMIT = 7.0
MOE_BLOCK = 256

N_EVEN_LAYERS = (DEPTH + 1) // 2
N_ODD_LAYERS = DEPTH // 2

kernel_name = 'hybrid_dit_attn_retnet_hgrn2_moe'


def rms_norm(x, gain):
    xf = x.astype(jnp.float32)
    y = xf * lax.rsqrt(jnp.mean(xf * xf, axis=-1, keepdims=True) + NORM_EPS)
    return (y * gain.astype(jnp.float32)).astype(x.dtype)


def modulate(h, shift, scale):
    return h * (1 + scale) + shift


def split_cols(t, sizes):
    cuts = [int(v) for v in np.cumsum(sizes)[:-1]]
    return jnp.split(t, cuts, axis=-1)


def to_heads(t, n_heads):
    b, l, _ = t.shape
    return t.reshape(b, l, n_heads, -1).transpose(0, 2, 1, 3)


def from_heads(t):
    b, h, l, d = t.shape
    return t.transpose(0, 2, 1, 3).reshape(b, l, h * d)


def flip_seq(t):
    return jnp.flip(t, axis=2)


def axial_rope_tables(n_tokens):
    n_rows = n_tokens // GRID_W
    row = jnp.repeat(jnp.arange(n_rows, dtype=jnp.float32), GRID_W)
    col = jnp.tile(jnp.arange(GRID_W, dtype=jnp.float32), n_rows)
    n_freq = HEAD_DIM // 4
    inv_freq = ROPE_THETA ** (-jnp.arange(n_freq, dtype=jnp.float32) / n_freq)
    ang = jnp.concatenate([row[:, None] * inv_freq, col[:, None] * inv_freq], axis=-1)
    return jnp.cos(ang), jnp.sin(ang)


def apply_rope(t, cos, sin):
    half = t.shape[-1] // 2
    t1 = t[..., :half].astype(jnp.float32)
    t2 = t[..., half:].astype(jnp.float32)
    return jnp.concatenate([t1 * cos - t2 * sin, t1 * sin + t2 * cos], axis=-1).astype(t.dtype)


def gqa_attend(q, k, v):
    b, hq, lq, d = q.shape
    hkv = k.shape[1]
    qg = q.reshape(b, hkv, hq // hkv, lq, d)
    s = jnp.einsum('bkgqd,bksd->bkgqs', qg, k).astype(jnp.float32) * (d ** -0.5)
    p = jax.nn.softmax(s, axis=-1).astype(v.dtype)
    return jnp.einsum('bkgqs,bksd->bkgqd', p, v).reshape(b, hq, lq, d)


def blocked_attention(q, k, v):
    b, h, l, d = q.shape
    nb = l // Q_BLOCK
    qb = jnp.moveaxis(q.reshape(b, h, nb, Q_BLOCK, d), 2, 0)
    ob = lax.map(lambda blk: gqa_attend(blk, k, v), qb)
    return jnp.moveaxis(ob, 0, 2).reshape(b, h, l, d)


def retention_chunked(q, k, v, log_g, s0):
    b, h, l, dk = q.shape
    dv = v.shape[-1]
    n = l // RET_CHUNK
    idx = jnp.arange(RET_CHUNK, dtype=jnp.float32)
    diff = idx[:, None] - idx[None, :]
    intra = jnp.where(diff >= 0, jnp.exp(log_g[:, None, None] * jnp.maximum(diff, 0.0)), 0.0)
    q_decay = jnp.exp(log_g[:, None] * (idx + 1.0))[None, :, :, None]
    k_decay = jnp.exp(log_g[:, None] * (RET_CHUNK - 1.0 - idx))[None, :, :, None]
    chunk_decay = jnp.exp(log_g * RET_CHUNK)[None, :, None, None]

    def chunks(t):
        return jnp.moveaxis(t.reshape(b, h, n, RET_CHUNK, t.shape[-1]), 2, 0)

    def step(s, inp):
        qc, kc, vc = inp
        scores = jnp.einsum('bhnd,bhmd->bhnm', qc, kc) * intra
        o = jnp.einsum('bhnm,bhmv->bhnv', scores, vc) + jnp.einsum('bhnd,bhdv->bhnv', qc * q_decay, s)
        s = s * chunk_decay + jnp.einsum('bhmd,bhmv->bhdv', kc * k_decay, vc)
        return s, o

    s, o = lax.scan(step, s0, (chunks(q), chunks(k), chunks(v)))
    return jnp.moveaxis(o, 0, 2).reshape(b, h, l, dv), s


def retention_state(k, v, log_g):
    l = k.shape[2]
    w = jnp.exp(log_g[:, None] * (l - 1.0 - jnp.arange(l, dtype=jnp.float32)))
    return jnp.einsum('bhld,bhlv->bhdv', k * w[None, :, :, None], v)


def gla_chunked(q, k, v, log_f, s0):
    b, h, l, dk = q.shape
    dv = v.shape[-1]
    n = l // HGRN_CHUNK
    causal = jnp.tril(jnp.ones((HGRN_CHUNK, HGRN_CHUNK), dtype=bool))

    def chunks(t):
        return jnp.moveaxis(t.reshape(b, h, n, HGRN_CHUNK, t.shape[-1]), 2, 0)

    def step(s, inp):
        qc, kc, vc, gc = inp
        cum = jnp.cumsum(gc, axis=2)
        diff = cum[:, :, :, None, :] - cum[:, :, None, :, :]
        decay = jnp.exp(jnp.where(causal[:, :, None], diff, -jnp.inf))
        a = jnp.einsum('bhnd,bhmd,bhnmd->bhnm', qc, kc, decay)
        o = jnp.einsum('bhnm,bhmv->bhnv', a, vc) + jnp.einsum('bhnd,bhdv->bhnv', qc * jnp.exp(cum), s)
        last = cum[:, :, -1:, :]
        s = s * jnp.exp(last)[:, :, 0, :, None] + jnp.einsum('bhmd,bhmv->bhdv', kc * jnp.exp(last - cum), vc)
        return s, o

    s, o = lax.scan(step, s0, (chunks(q), chunks(k), chunks(v), chunks(log_f)))
    return jnp.moveaxis(o, 0, 2).reshape(b, h, l, dv), s


def gla_state(k, v, log_f):
    cum = jnp.cumsum(log_f, axis=2)
    return jnp.einsum('bhld,bhlv->bhdv', k * jnp.exp(cum[:, :, -1:] - cum), v)


def hgrn2_gates(f_raw, lb):
    fr = f_raw.astype(jnp.float32)
    log_f = jnp.logaddexp(jnp.log(lb), jnp.log1p(-lb) + jax.nn.log_sigmoid(fr))
    k = (1.0 - lb) * jax.nn.sigmoid(-fr)
    return k, log_f


def group_norm_gate(o, gate, gain, dtype):
    mu = jnp.mean(o, axis=-1, keepdims=True)
    var = jnp.mean(jnp.square(o - mu), axis=-1, keepdims=True)
    on = from_heads((o - mu) * lax.rsqrt(var + NORM_EPS))
    return (on * gain.astype(jnp.float32) * jax.nn.silu(gate.astype(jnp.float32))).astype(dtype)


def rms_norm_gate(o, gate, gain, dtype):
    on = from_heads(o * lax.rsqrt(jnp.mean(o * o, axis=-1, keepdims=True) + NORM_EPS))
    return (on * gain.astype(jnp.float32) * jax.nn.silu(gate.astype(jnp.float32))).astype(dtype)


def clamped_swiglu(hid):
    x_glu = jnp.minimum(hid[..., ::2], SWIGLU_LIMIT)
    x_lin = jnp.clip(hid[..., 1::2], -SWIGLU_LIMIT, SWIGLU_LIMIT)
    return x_glu * jax.nn.sigmoid(SWIGLU_ALPHA * x_glu) * (x_lin + 1)


def moe_ffn(h, w_router, b_router, w1, b1, w2, b2):
    n_tok, d = h.shape
    logits = (h @ w_router).astype(jnp.float32) + b_router.astype(jnp.float32)
    top_val, top_idx = lax.top_k(logits, TOP_K)
    gates = jax.nn.softmax(top_val, axis=-1)
    n_assign = n_tok * TOP_K
    n_blocks = -(-(n_assign + N_EXPERTS * (MOE_BLOCK - 1)) // MOE_BLOCK)
    n_rows = n_blocks * MOE_BLOCK
    flat_e = top_idx.reshape(-1)
    flat_tok = jnp.arange(n_assign, dtype=jnp.int32) // TOP_K
    order = jnp.argsort(flat_e)
    e_sorted = flat_e[order]
    counts = jnp.bincount(flat_e, length=N_EXPERTS)
    padded = (counts + MOE_BLOCK - 1) // MOE_BLOCK * MOE_BLOCK
    pad_end = jnp.cumsum(padded)
    pad_start = pad_end - padded
    start = jnp.cumsum(counts) - counts
    dest = pad_start[e_sorted] + jnp.arange(n_assign, dtype=jnp.int32) - start[e_sorted]
    row_tok = jnp.full((n_rows,), n_tok, jnp.int32).at[dest].set(flat_tok[order])
    row_gate = jnp.zeros((n_rows,), jnp.float32).at[dest].set(gates.reshape(-1)[order])
    block_start = jnp.arange(n_blocks, dtype=jnp.int32) * MOE_BLOCK
    block_e = jnp.minimum(jnp.searchsorted(pad_end, block_start, side='right'), N_EXPERTS - 1)
    h_pad = jnp.concatenate([h, jnp.zeros((1, d), h.dtype)], axis=0)

    def block_step(acc, blk):
        tok, gate, e = blk
        hid = h_pad[tok] @ w1[e] + b1[e]
        out = clamped_swiglu(hid) @ w2[e] + b2[e]
        return acc.at[tok].add(out.astype(jnp.float32) * gate[:, None]), None

    acc0 = jnp.zeros((n_tok + 1, d), jnp.float32)
    acc, _ = lax.scan(block_step, acc0, (row_tok.reshape(n_blocks, MOE_BLOCK), row_gate.reshape(n_blocks, MOE_BLOCK), block_e))
    return acc[:n_tok].astype(h.dtype)


def mixer_attention_retention(hx, hy, w_in, w_out, q_gain, k_gain, decay_exp, gn_gain, cos, sin, ctx_out):
    bsz = hx.shape[0]
    ak, av, bk, bv, aq, bq, bg = split_cols(hx @ w_in, EVEN_SIZES)
    y_sizes = EVEN_SIZES if ctx_out else EVEN_KV_SIZES
    y_parts = split_cols(hy @ w_in[:, :sum(y_sizes)], y_sizes)
    ak_y, av_y, bk_y, bv_y = y_parts[:4]

    qa = apply_rope(rms_norm(to_heads(aq, A_Q_HEADS), q_gain), cos, sin)
    ka = apply_rope(rms_norm(to_heads(ak, A_KV_HEADS), k_gain), cos, sin)
    ka_y = rms_norm(to_heads(ak_y, A_KV_HEADS), k_gain)
    va = to_heads(av, A_KV_HEADS)
    va_y = to_heads(av_y, A_KV_HEADS)
    k_all = jnp.concatenate([ka_y, ka], axis=2)
    v_all = jnp.concatenate([va_y, va], axis=2)
    att_x = from_heads(blocked_attention(qa, k_all, v_all))

    log_g = jnp.log1p(-jnp.exp2(-decay_exp.astype(jnp.float32)))
    k_scale = HEAD_DIM ** -0.5
    qr = apply_rope(to_heads(bq, B_HEADS), cos, sin).astype(jnp.float32)
    kr = apply_rope(to_heads(bk, B_HEADS), cos, sin).astype(jnp.float32) * k_scale
    vr = to_heads(bv, B_HEADS).astype(jnp.float32)
    kr_y = to_heads(bk_y, B_HEADS).astype(jnp.float32) * k_scale
    vr_y = to_heads(bv_y, B_HEADS).astype(jnp.float32)
    if ctx_out:
        aq_y, bq_y, bg_y = y_parts[4:]
        att_y = from_heads(gqa_attend(rms_norm(to_heads(aq_y, A_Q_HEADS), q_gain), ka_y, va_y))
        qr_y = to_heads(bq_y, B_HEADS).astype(jnp.float32)
        s0 = jnp.zeros((bsz, B_HEADS, HEAD_DIM, HEAD_DIM), jnp.float32)
        oyf, syf = retention_chunked(qr_y, kr_y, vr_y, log_g[0], s0)
        oyb, syb = retention_chunked(flip_seq(qr_y), flip_seq(kr_y), flip_seq(vr_y), log_g[1], s0)
        ret_y = group_norm_gate(oyf + flip_seq(oyb), bg_y, gn_gain, hy.dtype)
        out_y = jnp.concatenate([att_y, ret_y], axis=-1) @ w_out
    else:
        syf = retention_state(kr_y, vr_y, log_g[0])
        syb = retention_state(flip_seq(kr_y), flip_seq(vr_y), log_g[1])
        out_y = None
    oxf, _ = retention_chunked(qr, kr, vr, log_g[0], syf)
    oxb, _ = retention_chunked(flip_seq(qr), flip_seq(kr), flip_seq(vr), log_g[1], syb)
    ret_x = group_norm_gate(oxf + flip_seq(oxb), bg, gn_gain, hx.dtype)
    out_x = jnp.concatenate([att_x, ret_x], axis=-1) @ w_out
    return out_x, out_y


def mixer_hgrn2(hx, hy, w_in, w_out, lb, gn_gain, ctx_out):
    bsz = hx.shape[0]
    ffx, fbx, ix, qx, gx = split_cols(hx @ w_in, ODD_SIZES)
    y_sizes = ODD_SIZES if ctx_out else ODD_STATE_SIZES
    y_parts = split_cols(hy @ w_in[:, :sum(y_sizes)], y_sizes)

    def state_inputs(f_fwd, f_bwd, inp):
        k_f, lf_f = hgrn2_gates(f_fwd, lb)
        k_b, lf_b = hgrn2_gates(f_bwd, lb)
        return (to_heads(k_f, C_HEADS), to_heads(lf_f, C_HEADS), to_heads(k_b, C_HEADS), to_heads(lf_b, C_HEADS), to_heads(inp.astype(jnp.float32), C_HEADS))

    kxf, lxf, kxb, lxb, vx = state_inputs(ffx, fbx, ix)
    kyf, lyf, kyb, lyb, vy = state_inputs(y_parts[0], y_parts[1], y_parts[2])
    if ctx_out:
        s0 = jnp.zeros((bsz, C_HEADS, C_DK, C_DV), jnp.float32)
        qy = to_heads(jax.nn.silu(y_parts[3].astype(jnp.float32)), C_HEADS)
        oyf, syf = gla_chunked(qy, kyf, vy, lyf, s0)
        oyb, syb = gla_chunked(flip_seq(qy), flip_seq(kyb), flip_seq(vy), flip_seq(lyb), s0)
        out_y = rms_norm_gate(oyf + flip_seq(oyb), y_parts[4], gn_gain, hy.dtype) @ w_out
    else:
        syf = gla_state(kyf, vy, lyf)
        syb = gla_state(flip_seq(kyb), flip_seq(vy), flip_seq(lyb))
        out_y = None
    qh = to_heads(jax.nn.silu(qx.astype(jnp.float32)), C_HEADS)
    oxf, _ = gla_chunked(qh, kxf, vx, lxf, syf)
    oxb, _ = gla_chunked(flip_seq(qh), flip_seq(kxb), flip_seq(vx), flip_seq(lxb), syb)
    out_x = rms_norm_gate(oxf + flip_seq(oxb), gx, gn_gain, hx.dtype) @ w_out
    return out_x, out_y


def setup_inputs(seed: int = 0) -> dict:
    key = jax.random.key(seed)
    ks = jax.random.split(key, 25)

    def normal(k, shape, scale=1.0):
        return jax.random.normal(k, shape, jnp.float32) * scale

    def gain(k, shape):
        return 1.0 + normal(k, shape, 0.05)

    return {
        'x': normal(ks[0], (BATCH, SEQ, D_MODEL)),
        'c': normal(ks[1], (BATCH, D_MODEL)),
        'ctx': normal(ks[2], (BATCH, CTX_LEN, D_MODEL)),
        'c_ctx': normal(ks[3], (D_MODEL,)),
        'ada_w': normal(ks[4], (DEPTH, D_MODEL, N_MOD * D_MODEL), 0.5 * D_MODEL ** -0.5),
        'ada_b': normal(ks[5], (DEPTH, N_MOD * D_MODEL), 0.02),
        'norm_mix': gain(ks[6], (DEPTH, D_MODEL)),
        'norm_ffn': gain(ks[7], (DEPTH, D_MODEL)),
        'ab_w_in': normal(ks[8], (N_EVEN_LAYERS, D_MODEL, EVEN_IN), D_MODEL ** -0.5),
        'ab_w_out': normal(ks[9], (N_EVEN_LAYERS, MIX_WIDTH, D_MODEL), MIX_WIDTH ** -0.5),
        'a_q_norm': gain(ks[10], (N_EVEN_LAYERS, HEAD_DIM)),
        'a_k_norm': gain(ks[11], (N_EVEN_LAYERS, HEAD_DIM)),
        'b_decay_exp': RET_DECAY_BASE + jnp.arange(B_HEADS, dtype=jnp.float32) + normal(ks[12], (N_EVEN_LAYERS, 2, B_HEADS), 0.1),
        'b_gn': gain(ks[13], (N_EVEN_LAYERS, B_WIDTH)),
        'c_w_in': normal(ks[14], (N_ODD_LAYERS, D_MODEL, ODD_IN), D_MODEL ** -0.5),
        'c_w_out': normal(ks[15], (N_ODD_LAYERS, C_VAL_WIDTH, D_MODEL), C_VAL_WIDTH ** -0.5),
        'c_lb': normal(ks[16], (DEPTH, C_KEY_WIDTH), 0.1),
        'c_gn': gain(ks[17], (N_ODD_LAYERS, C_VAL_WIDTH)),
        'router_w': normal(ks[18], (DEPTH, D_MODEL, N_EXPERTS), D_MODEL ** -0.5),
        'router_b': normal(ks[19], (DEPTH, N_EXPERTS), 0.01),
        'exp_w1': normal(ks[20], (DEPTH, N_EXPERTS, D_MODEL, 2 * D_EXPERT), D_MODEL ** -0.5),
        'exp_b1': normal(ks[21], (DEPTH, N_EXPERTS, 2 * D_EXPERT), 0.02),
        'exp_w2': normal(ks[22], (DEPTH, N_EXPERTS, D_EXPERT, D_MODEL), D_EXPERT ** -0.5),
        'exp_b2': normal(ks[23], (DEPTH, N_EXPERTS, D_MODEL), 0.02),
        'norm_final': gain(ks[24], (D_MODEL,)),
    }


def reference(x, c, ctx, c_ctx, ada_w, ada_b, norm_mix, norm_ffn, ab_w_in, ab_w_out, a_q_norm, a_k_norm, b_decay_exp, b_gn, c_w_in, c_w_out, c_lb, c_gn, router_w, router_b, exp_w1, exp_b1, exp_w2, exp_b2, norm_final):
    bsz, n_lat, d = x.shape
    n_ctx = ctx.shape[1]
    cos, sin = axial_rope_tables(n_lat)
    lb_soft = jax.nn.softmax(c_lb.astype(jnp.float32), axis=0)
    lower_bounds = jnp.cumsum(lb_soft, axis=0) - lb_soft[0]
    y = ctx
    for l in range(DEPTH):
        ctx_out = l < DEPTH - 1
        j = l // 2
        mod_x = jnp.split((jax.nn.silu(c) @ ada_w[l] + ada_b[l])[:, None, :], N_MOD, axis=-1)
        mod_y = jnp.split(jax.nn.silu(c_ctx) @ ada_w[l] + ada_b[l], N_MOD, axis=-1)
        hx = modulate(rms_norm(x, norm_mix[l]), mod_x[0], mod_x[1])
        hy = modulate(rms_norm(y, norm_mix[l]), mod_y[0], mod_y[1])
        if l % 2 == 0:
            ox, oy = mixer_attention_retention(hx, hy, ab_w_in[j], ab_w_out[j], a_q_norm[j], a_k_norm[j], b_decay_exp[j], b_gn[j], cos, sin, ctx_out)
        else:
            ox, oy = mixer_hgrn2(hx, hy, c_w_in[j], c_w_out[j], lower_bounds[l], c_gn[j], ctx_out)
        x = x + mod_x[2] * ox
        hx = modulate(rms_norm(x, norm_ffn[l]), mod_x[3], mod_x[4])
        if ctx_out:
            y = y + mod_y[2] * oy
            hy = modulate(rms_norm(y, norm_ffn[l]), mod_y[3], mod_y[4])
            tokens = jnp.concatenate([hx.reshape(-1, d), hy.reshape(-1, d)], axis=0)
            ffn = moe_ffn(tokens, router_w[l], router_b[l], exp_w1[l], exp_b1[l], exp_w2[l], exp_b2[l])
            x = x + mod_x[5] * ffn[: bsz * n_lat].reshape(bsz, n_lat, d)
            y = y + mod_y[5] * ffn[bsz * n_lat:].reshape(bsz, n_ctx, d)
        else:
            ffn = moe_ffn(hx.reshape(-1, d), router_w[l], router_b[l], exp_w1[l], exp_b1[l], exp_w2[l], exp_b2[l])
            x = x + mod_x[5] * ffn.reshape(bsz, n_lat, d)
    return rms_norm(x, norm_final)
```

```python
import functools

import jax
import jax.numpy as jnp
from jax import lax
from jax.experimental import pallas as pl
from jax.experimental.pallas import tpu as pltpu

HEAD_DIM = 128
GRID_W = 64
ROPE_THETA = 10000.0
NORM_EPS = 1e-6
N_MOD = 6
A_Q_HEADS = 8
A_KV_HEADS = 2
B_HEADS = 8
C_HEADS = 16
TOP_K = 4
SWIGLU_ALPHA = 1.702
SWIGLU_LIMIT = 7.0

LANES = 128
MOD_ROWS = 16
GLA_CHUNK = 64
MOE_BM = 512
GATHER_CHUNK = 512
VMEM_LIMIT = 56 * 1024 * 1024

F32 = jnp.float32
BF16 = jnp.bfloat16


def _cp(sem, vmem=VMEM_LIMIT):
    return pltpu.CompilerParams(dimension_semantics=sem, vmem_limit_bytes=vmem)


def _tile(n, pref):
    t = pref
    while n % t:
        t //= 2
    return t


def _dot(a, b):
    return jnp.dot(a, b, preferred_element_type=F32)


def _dot_nt(a, b):
    return lax.dot_general(a, b, (((1,), (1,)), ((), ())), preferred_element_type=F32)


def _dot_tn(a, b):
    return lax.dot_general(a, b, (((0,), (0,)), ((), ())), preferred_element_type=F32)


def _sigmoid(x):
    return 1.0 / (1.0 + jnp.exp(-x))


def _rope(t, cos2, sin2):
    return t * cos2 + pltpu.roll(t, HEAD_DIM // 2, 1) * sin2


def _rms_rows(x):
    return x * lax.rsqrt(jnp.mean(x * x, axis=-1, keepdims=True) + NORM_EPS)


def _ada_kernel(c_ref, w_ref, b_ref, o_ref):
    c = c_ref[...]
    s = (c * _sigmoid(c)).astype(BF16)
    o_ref[...] = _dot(s, w_ref[...].astype(BF16)) + b_ref[...]


def _ada(cond, ada_w, ada_b, tn=1024):
    depth, d, n = ada_w.shape
    return pl.pallas_call(
        _ada_kernel,
        grid=(depth, n // tn),
        in_specs=[
            pl.BlockSpec((MOD_ROWS, d), lambda l, j: (0, 0)),
            pl.BlockSpec((None, d, tn), lambda l, j: (l, 0, j)),
            pl.BlockSpec((None, 1, tn), lambda l, j: (l, 0, j)),
        ],
        out_specs=pl.BlockSpec((None, MOD_ROWS, tn), lambda l, j: (l, 0, j)),
        out_shape=jax.ShapeDtypeStruct((depth, MOD_ROWS, n), F32),
        compiler_params=_cp(("arbitrary", "arbitrary")),
        name="ada_mod",
    )(cond, ada_w, ada_b.reshape(depth, 1, n))


def _norm_mod_kernel(z_ref, g_ref, sh_ref, sc_ref, o_ref):
    y = _rms_rows(z_ref[...]) * g_ref[...]
    o_ref[...] = (y * (1.0 + sc_ref[...]) + sh_ref[...]).astype(o_ref.dtype)


def _norm_mod(z, gain, mods, nseg, bsz, rb):
    t, d = z.shape

    def midx(i):
        return jnp.where(i % nseg == 0, bsz, i // nseg)

    return pl.pallas_call(
        _norm_mod_kernel,
        grid=(t // rb,),
        in_specs=[
            pl.BlockSpec((rb, d), lambda i: (i, 0)),
            pl.BlockSpec((1, d), lambda i: (0, 0)),
            pl.BlockSpec((None, None, 1, d), lambda i: (midx(i), 0, 0, 0)),
            pl.BlockSpec((None, None, 1, d), lambda i: (midx(i), 1, 0, 0)),
        ],
        out_specs=pl.BlockSpec((rb, d), lambda i: (i, 0)),
        out_shape=jax.ShapeDtypeStruct((t, d), BF16),
        compiler_params=_cp(("arbitrary",)),
        name="norm_mod",
    )(z, gain, mods, mods)


def _mm_kernel(x_ref, w_ref, o_ref):
    o_ref[...] = _dot(x_ref[...], w_ref[...]).astype(o_ref.dtype)


def _matmul(x, w, tm, tn, out_dtype=F32, name="matmul"):
    m, k = x.shape
    n = w.shape[1]
    tm, tn = _tile(m, tm), _tile(n, tn)
    return pl.pallas_call(
        _mm_kernel,
        grid=(m // tm, n // tn),
        in_specs=[
            pl.BlockSpec((tm, k), lambda i, j: (i, 0)),
            pl.BlockSpec((k, tn), lambda i, j: (0, j)),
        ],
        out_specs=pl.BlockSpec((tm, tn), lambda i, j: (i, j)),
        out_shape=jax.ShapeDtypeStruct((m, n), out_dtype),
        compiler_params=_cp(("arbitrary", "arbitrary")),
        name=name,
    )(x, w)


def _attn_kernel(q_ref, k_ref, v_ref, cosq_ref, sinq_ref, cosk_ref, sink_ref,
                 qg_ref, kg_ref, o_ref, ks, vs, *, lc, group):
    qi = pl.program_id(2)
    rb = q_ref.shape[0]
    scale = HEAD_DIM ** -0.5

    @pl.when(qi == 0)
    def _prep():
        kn = _rms_rows(k_ref[...]) * kg_ref[...]
        ks[...] = _rope(kn, cosk_ref[...], sink_ref[...]).astype(BF16)
        vs[...] = v_ref[...].astype(BF16)

    def q_heads():
        q = q_ref[...]
        hs = []
        for r in range(group):
            qh = _rms_rows(q[:, r * HEAD_DIM:(r + 1) * HEAD_DIM]) * qg_ref[...]
            hs.append(_rope(qh, cosq_ref[...], sinq_ref[...]).astype(BF16))
        return jnp.concatenate(hs, axis=0)

    def attend(k, v):
        s = _dot_nt(q_heads(), k)
        m = jnp.max(s, axis=-1, keepdims=True)
        p = jnp.exp((s - m) * scale)
        l = jnp.sum(p, axis=-1, keepdims=True)
        o = _dot(p.astype(BF16), v) / l
        for r in range(group):
            o_ref[:, r * HEAD_DIM:(r + 1) * HEAD_DIM] = o[r * rb:(r + 1) * rb].astype(o_ref.dtype)

    @pl.when(qi == 0)
    def _ctx():
        attend(ks[0:lc], vs[0:lc])

    @pl.when(qi > 0)
    def _lat():
        attend(ks[...], vs[...])


def _attention(proj, cos2, sin2, q_gain, k_gain, bsz, seg, lc):
    t = proj.shape[0]
    nseg = seg // lc
    group = A_Q_HEADS // A_KV_HEADS
    kv_w = A_KV_HEADS
    q_col0 = (2 * kv_w + 2 * B_HEADS) // group
    kern = functools.partial(_attn_kernel, lc=lc, group=group)
    return pl.pallas_call(
        kern,
        grid=(bsz, A_KV_HEADS, nseg),
        in_specs=[
            pl.BlockSpec((lc, group * HEAD_DIM), lambda b, g, i: (b * nseg + i, q_col0 + g)),
            pl.BlockSpec((seg, HEAD_DIM), lambda b, g, i: (b, g)),
            pl.BlockSpec((seg, HEAD_DIM), lambda b, g, i: (b, kv_w + g)),
            pl.BlockSpec((lc, HEAD_DIM), lambda b, g, i: (i, 0)),
            pl.BlockSpec((lc, HEAD_DIM), lambda b, g, i: (i, 0)),
            pl.BlockSpec((seg, HEAD_DIM), lambda b, g, i: (0, 0)),
            pl.BlockSpec((seg, HEAD_DIM), lambda b, g, i: (0, 0)),
            pl.BlockSpec((1, HEAD_DIM), lambda b, g, i: (0, 0)),
            pl.BlockSpec((1, HEAD_DIM), lambda b, g, i: (0, 0)),
        ],
        out_specs=pl.BlockSpec((lc, group * HEAD_DIM), lambda b, g, i: (b * nseg + i, g)),
        out_shape=jax.ShapeDtypeStruct((t, A_Q_HEADS * HEAD_DIM), BF16),
        scratch_shapes=[pltpu.VMEM((seg, HEAD_DIM), BF16), pltpu.VMEM((seg, HEAD_DIM), BF16)],
        compiler_params=_cp(("arbitrary", "arbitrary", "arbitrary")),
        name="gqa_attention",
    )(proj, proj, proj, cos2, sin2, cos2, sin2, q_gain, k_gain)


def _ret_kernel(lg_ref, q_ref, k_ref, v_ref, g_ref, cos_ref, sin_ref, gn_ref, o_ref,
                oacc, qf_s, qb_s, uf_s, ub_s, sf_s, sb_s, *, c, nchunk, nctx):
    h = pl.program_id(1)
    lgf = lg_ref[0, h]
    lgb = lg_ref[1, h]
    cf = float(c)
    n_i = lax.broadcasted_iota(jnp.int32, (c, 1), 0).astype(F32)
    m_i = lax.broadcasted_iota(jnp.int32, (1, c), 1).astype(F32)
    diff = n_i - m_i
    dmat = (jnp.where(diff >= 0, jnp.exp(lgf * jnp.maximum(diff, 0.0)), 0.0)
            + jnp.where(diff <= 0, jnp.exp(lgb * jnp.maximum(-diff, 0.0)), 0.0))
    qdf = jnp.exp(lgf * (n_i + 1.0))
    qdb = jnp.exp(lgb * (cf - n_i))
    kdf = jnp.exp(lgf * (cf - 1.0 - n_i))
    kdb = jnp.exp(lgb * n_i)
    cdf = jnp.exp(lgf * cf)
    cdb = jnp.exp(lgb * cf)
    kscale = HEAD_DIM ** -0.5

    def phase_a(ci, carry):
        r0 = pl.multiple_of(ci * c, c)
        rows = pl.ds(r0, c)
        cos2 = cos_ref[rows, :]
        sin2 = sin_ref[rows, :]
        q = _rope(q_ref[rows, :], cos2, sin2)
        k = _rope(k_ref[rows, :], cos2, sin2) * kscale
        vb = v_ref[rows, :].astype(BF16)
        s = _dot_nt(q.astype(BF16), k.astype(BF16)) * dmat
        oacc[rows, :] = _dot(s.astype(BF16), vb)
        qf_s[rows, :] = (q * qdf).astype(BF16)
        qb_s[rows, :] = (q * qdb).astype(BF16)
        uf_s[ci] = _dot_tn((k * kdf).astype(BF16), vb)
        ub_s[ci] = _dot_tn((k * kdb).astype(BF16), vb)
        return carry

    lax.fori_loop(0, nchunk, phase_a, 0)

    def scan_f(ci, s):
        sf_s[ci] = s.astype(BF16)
        return s * cdf + uf_s[ci]

    lax.fori_loop(0, nchunk, scan_f, jnp.zeros((HEAD_DIM, HEAD_DIM), F32))

    def scan_b(i, s):
        ci = jnp.where(i < nctx, nctx - 1 - i, nchunk - 1 - (i - nctx))
        sb_s[ci] = s.astype(BF16)
        return s * cdb + ub_s[ci]

    lax.fori_loop(0, nchunk, scan_b, jnp.zeros((HEAD_DIM, HEAD_DIM), F32))

    def phase_c(ci, carry):
        r0 = pl.multiple_of(ci * c, c)
        rows = pl.ds(r0, c)
        o = oacc[rows, :] + _dot(qf_s[rows, :], sf_s[ci]) + _dot(qb_s[rows, :], sb_s[ci])
        mu = jnp.mean(o, axis=-1, keepdims=True)
        d = o - mu
        on = d * lax.rsqrt(jnp.mean(d * d, axis=-1, keepdims=True) + NORM_EPS)
        g = g_ref[rows, :]
        o_ref[rows, :] = (on * gn_ref[...] * (g * _sigmoid(g))).astype(o_ref.dtype)
        return carry

    lax.fori_loop(0, nchunk, phase_c, 0)


def _retention(proj, log_g, cos2, sin2, gn_gain, bsz, seg, lc):
    t = proj.shape[0]
    nchunk = seg // lc
    kvw = 2 * A_KV_HEADS
    k0, v0 = kvw, kvw + B_HEADS
    q0 = kvw + 2 * B_HEADS + A_Q_HEADS
    g0 = q0 + B_HEADS
    kern = functools.partial(_ret_kernel, c=lc, nchunk=nchunk, nctx=1)
    col = lambda c0: pl.BlockSpec((seg, HEAD_DIM), lambda b, h: (b, c0 + h))
    return pl.pallas_call(
        kern,
        grid=(bsz, B_HEADS),
        in_specs=[
            pl.BlockSpec(memory_space=pltpu.SMEM),
            col(q0), col(k0), col(v0), col(g0),
            pl.BlockSpec((seg, HEAD_DIM), lambda b, h: (0, 0)),
            pl.BlockSpec((seg, HEAD_DIM), lambda b, h: (0, 0)),
            pl.BlockSpec((1, HEAD_DIM), lambda b, h: (0, h)),
        ],
        out_specs=pl.BlockSpec((seg, HEAD_DIM), lambda b, h: (b, h)),
        out_shape=jax.ShapeDtypeStruct((t, B_HEADS * HEAD_DIM), BF16),
        scratch_shapes=[
            pltpu.VMEM((seg, HEAD_DIM), F32),
            pltpu.VMEM((seg, HEAD_DIM), BF16),
            pltpu.VMEM((seg, HEAD_DIM), BF16),
            pltpu.VMEM((nchunk, HEAD_DIM, HEAD_DIM), F32),
            pltpu.VMEM((nchunk, HEAD_DIM, HEAD_DIM), F32),
            pltpu.VMEM((nchunk, HEAD_DIM, HEAD_DIM), BF16),
            pltpu.VMEM((nchunk, HEAD_DIM, HEAD_DIM), BF16),
        ],
        compiler_params=_cp(("arbitrary", "arbitrary")),
        name="retention",
    )(log_g, proj, proj, proj, proj, cos2, sin2, gn_gain)


def _split3(x):
    hi = x.astype(BF16)
    r1 = x - hi.astype(F32)
    mid = r1.astype(BF16)
    lo = (r1 - mid.astype(F32)).astype(BF16)
    return hi, mid, lo


def _hgrn_kernel(ff_ref, fb_ref, v_ref, q_ref, g_ref, lb_ref, gn_ref, o_ref,
                 oacc, qf_s, qb_s, uf_s, ub_s, df_s, db_s, sf_s, sb_s, *, c, nchunk, nctx):
    lb = lb_ref[...]
    one_m_lb = 1.0 - lb
    n_i = lax.broadcasted_iota(jnp.int32, (c, c), 0)
    m_i = lax.broadcasted_iota(jnp.int32, (c, c), 1)
    lower = n_i >= m_i
    upper = m_i >= n_i
    tri_l = lower.astype(BF16)
    tri_u = upper.astype(BF16)
    mid = c // 2

    def csum(tri, x):
        hi, md, lo = _split3(x)
        return _dot(tri, hi) + _dot(tri, md) + _dot(tri, lo)

    def phase_a(ci, carry):
        r0 = pl.multiple_of(ci * c, c)
        rows = pl.ds(r0, c)
        qr = q_ref[rows, :]
        q = qr * _sigmoid(qr)
        vb = v_ref[rows, :].astype(BF16)
        frf = ff_ref[rows, :]
        frb = fb_ref[rows, :]
        kf = one_m_lb * _sigmoid(-frf)
        kb = one_m_lb * _sigmoid(-frb)
        lff = jnp.log(lb + one_m_lb * _sigmoid(frf))
        lfb = jnp.log(lb + one_m_lb * _sigmoid(frb))
        cum_f = csum(tri_l, lff)
        cum_b = csum(tri_u, lfb)
        an_f = cum_f[mid:mid + 1, :]
        an_b = cum_b[mid:mid + 1, :]
        a_f = _dot_nt((q * jnp.exp(cum_f - an_f)).astype(BF16), (kf * jnp.exp(an_f - cum_f)).astype(BF16))
        a_b = _dot_nt((q * jnp.exp(cum_b - an_b)).astype(BF16), (kb * jnp.exp(an_b - cum_b)).astype(BF16))
        a = jnp.where(lower, a_f, 0.0) + jnp.where(upper, a_b, 0.0)
        oacc[rows, :] = _dot(a.astype(BF16), vb)
        last_f = cum_f[c - 1:c, :]
        last_b = cum_b[0:1, :]
        qf_s[rows, :] = (q * jnp.exp(cum_f)).astype(BF16)
        qb_s[rows, :] = (q * jnp.exp(cum_b)).astype(BF16)
        uf_s[ci] = _dot_tn(vb, (kf * jnp.exp(last_f - cum_f)).astype(BF16))
        ub_s[ci] = _dot_tn(vb, (kb * jnp.exp(last_b - cum_b)).astype(BF16))
        df_s[ci] = jnp.exp(last_f)
        db_s[ci] = jnp.exp(last_b)
        return carry

    lax.fori_loop(0, nchunk, phase_a, 0)

    def scan_f(ci, s):
        sf_s[ci] = s.astype(BF16)
        return s * df_s[ci] + uf_s[ci]

    lax.fori_loop(0, nchunk, scan_f, jnp.zeros((HEAD_DIM, HEAD_DIM), F32))

    def scan_b(i, s):
        ci = jnp.where(i < nctx, nctx - 1 - i, nchunk - 1 - (i - nctx))
        sb_s[ci] = s.astype(BF16)
        return s * db_s[ci] + ub_s[ci]

    lax.fori_loop(0, nchunk, scan_b, jnp.zeros((HEAD_DIM, HEAD_DIM), F32))

    def phase_c(ci, carry):
        r0 = pl.multiple_of(ci * c, c)
        rows = pl.ds(r0, c)
        o = oacc[rows, :] + _dot_nt(qf_s[rows, :], sf_s[ci]) + _dot_nt(qb_s[rows, :], sb_s[ci])
        on = _rms_rows(o)
        g = g_ref[rows, :]
        o_ref[rows, :] = (on * gn_ref[...] * (g * _sigmoid(g))).astype(o_ref.dtype)
        return carry

    lax.fori_loop(0, nchunk, phase_c, 0)


def _hgrn(proj, lb, gn_gain, bsz, seg, lc):
    t = proj.shape[0]
    c = GLA_CHUNK
    nchunk = seg // c
    nctx = lc // c
    kern = functools.partial(_hgrn_kernel, c=c, nchunk=nchunk, nctx=nctx)
    col = lambda c0: pl.BlockSpec((seg, HEAD_DIM), lambda b, h: (b, c0 + h))
    vec = pl.BlockSpec((1, HEAD_DIM), lambda b, h: (0, h))
    return pl.pallas_call(
        kern,
        grid=(bsz, C_HEADS),
        in_specs=[col(0), col(C_HEADS), col(2 * C_HEADS), col(3 * C_HEADS), col(4 * C_HEADS), vec, vec],
        out_specs=pl.BlockSpec((seg, HEAD_DIM), lambda b, h: (b, h)),
        out_shape=jax.ShapeDtypeStruct((t, C_HEADS * HEAD_DIM), BF16),
        scratch_shapes=[
            pltpu.VMEM((seg, HEAD_DIM), F32),
            pltpu.VMEM((seg, HEAD_DIM), BF16),
            pltpu.VMEM((seg, HEAD_DIM), BF16),
            pltpu.VMEM((nchunk, HEAD_DIM, HEAD_DIM), F32),
            pltpu.VMEM((nchunk, HEAD_DIM, HEAD_DIM), F32),
            pltpu.VMEM((nchunk, 1, HEAD_DIM), F32),
            pltpu.VMEM((nchunk, 1, HEAD_DIM), F32),
            pltpu.VMEM((nchunk, HEAD_DIM, HEAD_DIM), BF16),
            pltpu.VMEM((nchunk, HEAD_DIM, HEAD_DIM), BF16),
        ],
        compiler_params=_cp(("arbitrary", "arbitrary")),
        name="hgrn2",
    )(proj, proj, proj, proj, proj, lb, gn_gain)


def _out_kernel(*refs, n_in):
    xs = refs[:n_in]
    ws = refs[n_in:2 * n_in]
    z_ref, gate_ref, gain_ref, sh_ref, sc_ref, wr_ref, br_ref = refs[2 * n_in:2 * n_in + 7]
    z_out, h_out, lg_out = refs[2 * n_in + 7:]
    o = _dot(xs[0][...], ws[0][...])
    for x_ref, w_ref in zip(xs[1:], ws[1:]):
        o = o + _dot(x_ref[...], w_ref[...])
    z1 = z_ref[...] + gate_ref[...] * o
    z_out[...] = z1
    h = _rms_rows(z1) * gain_ref[...]
    h = h * (1.0 + sc_ref[...]) + sh_ref[...]
    h_out[...] = h
    lg_out[...] = _dot(h.astype(BF16), wr_ref[...]) + br_ref[...]


def _out_proj(xs, ws, z, mods, gain, wr, br, bsz, nseg, off, rb):
    d = z.shape[1]
    nout = nseg - off
    n_in = len(xs)
    npad = wr.shape[1]

    def rin(b, j):
        return (b * nseg + off + j, 0)

    def rout(b, j):
        return (b * nout + j, 0)

    def mod(which):
        return pl.BlockSpec((None, None, 1, d),
                            lambda b, j: (jnp.where(j + off == 0, bsz, b), which, 0, 0))

    in_specs = [pl.BlockSpec((rb, x.shape[1]), rin) for x in xs]
    in_specs += [pl.BlockSpec(w.shape, lambda b, j: (0, 0)) for w in ws]
    in_specs += [
        pl.BlockSpec((rb, d), rin),
        mod(2),
        pl.BlockSpec((1, d), lambda b, j: (0, 0)),
        mod(3), mod(4),
        pl.BlockSpec(wr.shape, lambda b, j: (0, 0)),
        pl.BlockSpec((1, npad), lambda b, j: (0, 0)),
    ]
    tm = bsz * nout * rb
    return pl.pallas_call(
        functools.partial(_out_kernel, n_in=n_in),
        grid=(bsz, nout),
        in_specs=in_specs,
        out_specs=[pl.BlockSpec((rb, d), rout), pl.BlockSpec((rb, d), rout), pl.BlockSpec((rb, npad), rout)],
        out_shape=[jax.ShapeDtypeStruct((tm, d), F32), jax.ShapeDtypeStruct((tm, d), F32),
                   jax.ShapeDtypeStruct((tm, npad), F32)],
        compiler_params=_cp(("arbitrary", "arbitrary")),
        name="out_proj",
    )(*xs, *ws, z, mods, gain, mods, mods, wr, br)


def _topk_kernel(lg_ref, idx_ref, gate_ref):
    l = lg_ref[...]
    lane = lax.broadcasted_iota(jnp.int32, l.shape, 1)
    vals, idxs = [], []
    for _ in range(TOP_K):
        m = jnp.max(l, axis=-1, keepdims=True)
        i = jnp.min(jnp.where(l == m, lane, LANES), axis=-1, keepdims=True)
        vals.append(m)
        idxs.append(i)
        l = jnp.where(lane == i, -jnp.inf, l)
    es = [jnp.exp(v - vals[0]) for v in vals]
    den = es[0] + es[1] + es[2] + es[3]
    io = jnp.zeros(l.shape, jnp.int32)
    go = jnp.zeros(l.shape, F32)
    for k in range(TOP_K):
        io = jnp.where(lane == k, idxs[k], io)
        go = jnp.where(lane == k, es[k] / den, go)
    idx_ref[...] = io
    gate_ref[...] = go


def _topk(logits, tm=512):
    t, n = logits.shape
    tm = _tile(t, tm)
    return pl.pallas_call(
        _topk_kernel,
        grid=(t // tm,),
        in_specs=[pl.BlockSpec((tm, n), lambda i: (i, 0))],
        out_specs=[pl.BlockSpec((tm, n), lambda i: (i, 0)), pl.BlockSpec((tm, n), lambda i: (i, 0))],
        out_shape=[jax.ShapeDtypeStruct((t, n), jnp.int32), jax.ShapeDtypeStruct((t, n), F32)],
        compiler_params=_cp(("arbitrary",)),
        name="router_topk",
    )(logits)


def _gather_kernel(idx_ref, src_ref, out_ref, sem, *, chunk):
    base = pl.program_id(0) * chunk

    def row_copy(r, src_row):
        return pltpu.make_async_copy(src_ref.at[pl.ds(src_row, 1)], out_ref.at[pl.ds(base + r, 1)], sem)

    def issue(r, carry):
        row_copy(r, idx_ref[0, r]).start()
        return carry

    lax.fori_loop(0, chunk, issue, 0)

    def drain(r, carry):
        row_copy(r, 0).wait()
        return carry

    lax.fori_loop(0, chunk, drain, 0)


def _gather_rows(src, idx, chunk=GATHER_CHUNK):
    r = idx.shape[0]
    d = src.shape[1]
    nchunks = r // chunk
    return pl.pallas_call(
        functools.partial(_gather_kernel, chunk=chunk),
        grid=(nchunks,),
        in_specs=[
            pl.BlockSpec((None, 1, chunk), lambda i: (i, 0, 0), memory_space=pltpu.SMEM),
            pl.BlockSpec(memory_space=pl.ANY),
        ],
        out_specs=pl.BlockSpec(memory_space=pl.ANY),
        out_shape=jax.ShapeDtypeStruct((r, d), src.dtype),
        scratch_shapes=[pltpu.SemaphoreType.DMA],
        compiler_params=_cp(("arbitrary",)),
        name="row_gather",
    )(idx.reshape(nchunks, 1, chunk), src)


def _gmm1_kernel(be_ref, first_ref, nused_ref, x_ref, w_ref, b_ref, o_ref, wsc):
    i = pl.program_id(1)
    tn = w_ref.shape[1]

    @pl.when(i < nused_ref[0])
    def _():
        @pl.when(first_ref[i] == 1)
        def _cast():
            wsc[...] = w_ref[...].astype(BF16)

        hid = _dot(x_ref[...].astype(BF16), wsc[...]) + b_ref[...]
        glu = jnp.minimum(hid, SWIGLU_LIMIT)
        glu = glu * _sigmoid(SWIGLU_ALPHA * glu)
        lin = jnp.clip(hid, -SWIGLU_LIMIT, SWIGLU_LIMIT) + 1.0
        w2 = 2 * LANES
        rsel = lax.broadcasted_iota(jnp.int32, (w2, LANES), 0)
        csel = lax.broadcasted_iota(jnp.int32, (w2, LANES), 1)
        sel = (rsel == 2 * csel).astype(BF16)
        for s in range(tn // w2):
            parts = []
            for u in range(2):
                lo = s * w2 + u * LANES
                lin_sh = pltpu.roll(lin[:, lo:lo + LANES], LANES - 1, 1)
                parts.append((glu[:, lo:lo + LANES] * lin_sh).astype(BF16))
            prod = jnp.concatenate(parts, axis=1)
            o_ref[:, s * LANES:(s + 1) * LANES] = _dot(prod, sel).astype(o_ref.dtype)

    @pl.when(i >= nused_ref[0])
    def _unused():
        o_ref[...] = jnp.zeros(o_ref.shape, o_ref.dtype)


def _gmm2_kernel(be_ref, first_ref, nused_ref, x_ref, w_ref, b_ref, o_ref, wsc):
    i = pl.program_id(1)

    @pl.when(i < nused_ref[0])
    def _():
        @pl.when(first_ref[i] == 1)
        def _cast():
            wsc[...] = w_ref[...].astype(BF16)

        o_ref[...] = _dot(x_ref[...], wsc[...]) + b_ref[...]

    @pl.when(i >= nused_ref[0])
    def _unused():
        o_ref[...] = jnp.zeros(o_ref.shape, o_ref.dtype)


def _gmm(kernel, x, w, b, be, first, nused, bm, tn, out_cols, out_dtype, name):
    r, k = x.shape
    n = w.shape[2]
    nb = r // bm
    n_tiles = n // tn
    oc = out_cols // n_tiles

    def blk(i, nu):
        return jnp.minimum(i, nu[0] - 1)

    grid_spec = pltpu.PrefetchScalarGridSpec(
        num_scalar_prefetch=3,
        grid=(n_tiles, nb),
        in_specs=[
            pl.BlockSpec((bm, k), lambda j, i, be, fi, nu: (blk(i, nu), 0)),
            pl.BlockSpec((None, k, tn), lambda j, i, be, fi, nu: (be[blk(i, nu)], 0, j)),
            pl.BlockSpec((None, 1, tn), lambda j, i, be, fi, nu: (be[blk(i, nu)], 0, j)),
        ],
        out_specs=pl.BlockSpec((bm, oc), lambda j, i, be, fi, nu: (i, j)),
        scratch_shapes=[pltpu.VMEM((k, tn), BF16)],
    )
    return pl.pallas_call(
        kernel,
        grid_spec=grid_spec,
        out_shape=jax.ShapeDtypeStruct((r, out_cols), out_dtype),
        compiler_params=_cp(("arbitrary", "arbitrary")),
        name=name,
    )(be, first, nused, x, w, b.reshape(b.shape[0], 1, n))


def _combine_kernel(z_ref, y_ref, gt_ref, gate_ref, gain_ref, sh_ref, sc_ref, *outs, final):
    d = z_ref.shape[1]
    gt = gt_ref[...]
    ffn = gt[:, 0:1] * y_ref[:, 0:d]
    for k in range(1, TOP_K):
        ffn = ffn + gt[:, k:k + 1] * y_ref[:, k * d:(k + 1) * d]
    z2 = z_ref[...] + gate_ref[...] * ffn
    h = _rms_rows(z2) * gain_ref[...]
    if final:
        outs[0][...] = h
    else:
        outs[0][...] = z2
        outs[1][...] = (h * (1.0 + sc_ref[...]) + sh_ref[...]).astype(outs[1].dtype)


def _combine(z1, yt, gates, mods_cur, mods_next, gain, bsz, nblk, rb, final):
    t, d = z1.shape
    has_ctx = not final

    def midx(i):
        if has_ctx:
            return jnp.where(i % nblk == 0, bsz, i // nblk)
        return i // nblk

    row = lambda i: (i, 0)
    out_specs = [pl.BlockSpec((rb, d), row)]
    out_shape = [jax.ShapeDtypeStruct((t, d), F32)]
    if not final:
        out_specs.append(pl.BlockSpec((rb, d), row))
        out_shape.append(jax.ShapeDtypeStruct((t, d), BF16))
    return pl.pallas_call(
        functools.partial(_combine_kernel, final=final),
        grid=(t // rb,),
        in_specs=[
            pl.BlockSpec((rb, d), row),
            pl.BlockSpec((rb, TOP_K * d), row),
            pl.BlockSpec((rb, gates.shape[1]), row),
            pl.BlockSpec((None, None, 1, d), lambda i: (midx(i), 5, 0, 0)),
            pl.BlockSpec((1, d), lambda i: (0, 0)),
            pl.BlockSpec((None, None, 1, d), lambda i: (midx(i), 0, 0, 0)),
            pl.BlockSpec((None, None, 1, d), lambda i: (midx(i), 1, 0, 0)),
        ],
        out_specs=out_specs,
        out_shape=out_shape,
        compiler_params=_cp(("arbitrary",)),
        name="moe_combine",
    )(z1, yt, gates, mods_cur, gain, mods_next, mods_next)


def _routing(idx, n_exp, bm, gchunk):
    tm = idx.shape[0]
    onehot = (idx[:, :, None] == jnp.arange(n_exp, dtype=jnp.int32)).astype(jnp.int32).sum(axis=1)
    csum = jnp.cumsum(onehot, axis=0)
    counts = csum[-1]
    rank = jnp.take_along_axis(csum - onehot, idx, axis=1)
    padded = (counts + bm - 1) // bm * bm
    pad_end = jnp.cumsum(padded)
    pad_start = pad_end - padded
    dest = pad_start[idx] + rank
    nb = (tm * TOP_K + n_exp * (bm - 1)) // bm
    nb = -(-(nb * bm) // gchunk) * gchunk // bm
    tok = jnp.repeat(jnp.arange(tm, dtype=jnp.int32), TOP_K)
    row_tok = jnp.zeros((nb * bm,), jnp.int32).at[dest.reshape(-1)].set(tok)
    nused = (pad_end[-1] // bm).astype(jnp.int32).reshape(1)
    block_start = jnp.arange(nb, dtype=jnp.int32) * bm
    be = jnp.minimum(jnp.searchsorted(pad_end, block_start, side='right'), n_exp - 1).astype(jnp.int32)
    first = jnp.concatenate([jnp.ones((1,), jnp.int32), (be[1:] != be[:-1]).astype(jnp.int32)])
    return dest.reshape(-1).astype(jnp.int32), row_tok, be, first, nused


def _moe(h2, logits, w1, b1, w2, b2):
    tm, d = h2.shape
    n_exp = w1.shape[0]
    idx_p, gate_p = _topk(logits)
    idx = idx_p[:, :TOP_K]
    dest, row_tok, be, first, nused = _routing(idx, n_exp, MOE_BM, GATHER_CHUNK)
    xs = _gather_rows(h2, row_tok)
    act = _gmm(_gmm1_kernel, xs, w1, b1, be, first, nused, MOE_BM, 1024, w1.shape[2] // 2, BF16, "moe_up")
    y = _gmm(_gmm2_kernel, act, w2, b2, be, first, nused, MOE_BM, 1024, w2.shape[2], F32, "moe_down")
    yt = _gather_rows(y, dest)
    return yt.reshape(tm, TOP_K * d), gate_p


def _rope_tables(n_lat, lc):
    n_rows = n_lat // GRID_W
    row = jnp.repeat(jnp.arange(n_rows, dtype=F32), GRID_W)
    col = jnp.tile(jnp.arange(GRID_W, dtype=F32), n_rows)
    n_freq = HEAD_DIM // 4
    inv_freq = ROPE_THETA ** (-jnp.arange(n_freq, dtype=F32) / n_freq)
    ang = jnp.concatenate([row[:, None] * inv_freq, col[:, None] * inv_freq], axis=-1)
    cos, sin = jnp.cos(ang), jnp.sin(ang)
    cos2 = jnp.concatenate([cos, cos], axis=-1)
    sin2 = jnp.concatenate([-sin, sin], axis=-1)
    cos2 = jnp.concatenate([jnp.ones((lc, HEAD_DIM), F32), cos2], axis=0)
    sin2 = jnp.concatenate([jnp.zeros((lc, HEAD_DIM), F32), sin2], axis=0)
    return cos2, sin2


def kernel(x, c, ctx, c_ctx, ada_w, ada_b, norm_mix, norm_ffn, ab_w_in, ab_w_out, a_q_norm, a_k_norm, b_decay_exp, b_gn, c_w_in, c_w_out, c_lb, c_gn, router_w, router_b, exp_w1, exp_b1, exp_w2, exp_b2, norm_final):
    bsz, n_lat, d = x.shape
    lc = ctx.shape[1]
    depth = ada_w.shape[0]
    assert depth == 2 and n_lat % lc == 0 and bsz < MOD_ROWS and lc % GLA_CHUNK == 0
    seg = lc + n_lat
    nseg = seg // lc
    n_exp = router_w.shape[2]

    z = jnp.concatenate([ctx, x], axis=1).reshape(bsz * seg, d)

    cond = jnp.zeros((MOD_ROWS, d), F32).at[:bsz].set(c).at[bsz].set(c_ctx)
    mods = _ada(cond, ada_w, ada_b).reshape(depth, MOD_ROWS, N_MOD, 1, d)
    cos2, sin2 = _rope_tables(n_lat, lc)

    lb_soft = jax.nn.softmax(c_lb.astype(F32), axis=0)
    lower_bounds = jnp.cumsum(lb_soft, axis=0) - lb_soft[0]
    log_g = jnp.log1p(-jnp.exp2(-b_decay_exp[0].astype(F32)))

    wr = jnp.zeros((depth, d, LANES), BF16).at[:, :, :n_exp].set(router_w.astype(BF16))
    br = jnp.full((depth, 1, LANES), -1e30, F32).at[:, 0, :n_exp].set(router_b)

    hz = _norm_mod(z, norm_mix[0:1], mods[0], nseg, bsz, lc)
    proj = _matmul(hz, ab_w_in[0].astype(BF16), 1024, 512, name="in_proj0")
    att = _attention(proj, cos2, sin2, a_q_norm[0:1], a_k_norm[0:1], bsz, seg, lc)
    ret = _retention(proj, log_g, cos2, sin2, b_gn[0:1], bsz, seg, lc)
    w_out = ab_w_out[0].astype(BF16)
    aw = A_Q_HEADS * HEAD_DIM
    z1, h2, logits = _out_proj([att, ret], [w_out[:aw], w_out[aw:]], z, mods[0], norm_ffn[0:1],
                               wr[0], br[0], bsz, nseg, 0, lc)
    yt, gates = _moe(h2, logits, exp_w1[0], exp_b1[0], exp_w2[0], exp_b2[0])
    z2, hz = _combine(z1, yt, gates, mods[0], mods[1], norm_mix[1:2], bsz, nseg, lc, final=False)

    proj = _matmul(hz, c_w_in[0].astype(BF16), 1024, 512, name="in_proj1")
    hg = _hgrn(proj, lower_bounds[1:2], c_gn[0:1], bsz, seg, lc)
    z1, h2, logits = _out_proj([hg], [c_w_out[0].astype(BF16)], z2, mods[1], norm_ffn[1:2],
                               wr[1], br[1], bsz, nseg, 1, lc)
    yt, gates = _moe(h2, logits, exp_w1[1], exp_b1[1], exp_w2[1], exp_b2[1])
    (out,) = _combine(z1, yt, gates, mods[1], mods[1], norm_final.reshape(1, d), bsz, nseg - 1, lc, final=True)
    return out.reshape(bsz, n_lat, d)
```

```python
import functools

import jax
import jax.numpy as jnp
from jax import lax
from jax.experimental import pallas as pl
from jax.experimental.pallas import tpu as pltpu

HEAD_DIM = 128
GRID_W = 64
ROPE_THETA = 10000.0
NORM_EPS = 1e-6
N_MOD = 6
A_Q_HEADS = 8
A_KV_HEADS = 2
B_HEADS = 8
C_HEADS = 16
TOP_K = 4
SWIGLU_ALPHA = 1.702
SWIGLU_LIMIT = 7.0

LANES = 128
MOD_ROWS = 16
GLA_CHUNK = 64
MOE_BM = 512
GATHER_CHUNK = 512
VMEM_LIMIT = 56 * 1024 * 1024

F32 = jnp.float32
BF16 = jnp.bfloat16


def _cp(sem, vmem=VMEM_LIMIT):
    return pltpu.CompilerParams(dimension_semantics=sem, vmem_limit_bytes=vmem)


def _tile(n, pref):
    t = pref
    while n % t:
        t //= 2
    return t


def _dot(a, b):
    return jnp.dot(a, b, preferred_element_type=F32)


def _dot_nt(a, b):
    return lax.dot_general(a, b, (((1,), (1,)), ((), ())), preferred_element_type=F32)


def _dot_tn(a, b):
    return lax.dot_general(a, b, (((0,), (0,)), ((), ())), preferred_element_type=F32)


def _sigmoid(x):
    return 1.0 / (1.0 + jnp.exp(-x))


def _rope(t, cos2, sin2):
    return t * cos2 + pltpu.roll(t, HEAD_DIM // 2, 1) * sin2


def _rms_rows(x):
    return x * lax.rsqrt(jnp.mean(x * x, axis=-1, keepdims=True) + NORM_EPS)


def _ada_kernel(c_ref, w_ref, b_ref, o_ref):
    c = c_ref[...]
    s = (c * _sigmoid(c)).astype(BF16)
    o_ref[...] = _dot(s, w_ref[...].astype(BF16)) + b_ref[...]


def _ada(cond, ada_w, ada_b, tn=1024):
    depth, d, n = ada_w.shape
    return pl.pallas_call(
        _ada_kernel,
        grid=(depth, n // tn),
        in_specs=[
            pl.BlockSpec((MOD_ROWS, d), lambda l, j: (0, 0)),
            pl.BlockSpec((None, d, tn), lambda l, j: (l, 0, j)),
            pl.BlockSpec((None, 1, tn), lambda l, j: (l, 0, j)),
        ],
        out_specs=pl.BlockSpec((None, MOD_ROWS, tn), lambda l, j: (l, 0, j)),
        out_shape=jax.ShapeDtypeStruct((depth, MOD_ROWS, n), F32),
        compiler_params=_cp(("arbitrary", "arbitrary")),
        name="ada_mod",
    )(cond, ada_w, ada_b.reshape(depth, 1, n))


def _norm_mod_kernel(z_ref, g_ref, sh_ref, sc_ref, o_ref):
    y = _rms_rows(z_ref[...]) * g_ref[...]
    o_ref[...] = (y * (1.0 + sc_ref[...]) + sh_ref[...]).astype(o_ref.dtype)


def _norm_mod(z, gain, mods, nseg, bsz, rb):
    t, d = z.shape

    def midx(i):
        return jnp.where(i % nseg == 0, bsz, i // nseg)

    return pl.pallas_call(
        _norm_mod_kernel,
        grid=(t // rb,),
        in_specs=[
            pl.BlockSpec((rb, d), lambda i: (i, 0)),
            pl.BlockSpec((1, d), lambda i: (0, 0)),
            pl.BlockSpec((None, None, 1, d), lambda i: (midx(i), 0, 0, 0)),
            pl.BlockSpec((None, None, 1, d), lambda i: (midx(i), 1, 0, 0)),
        ],
        out_specs=pl.BlockSpec((rb, d), lambda i: (i, 0)),
        out_shape=jax.ShapeDtypeStruct((t, d), BF16),
        compiler_params=_cp(("arbitrary",)),
        name="norm_mod",
    )(z, gain, mods, mods)


def _mm_kernel(x_ref, w_ref, o_ref):
    o_ref[...] = _dot(x_ref[...], w_ref[...]).astype(o_ref.dtype)


def _matmul(x, w, tm, tn, out_dtype=F32, name="matmul"):
    m, k = x.shape
    n = w.shape[1]
    tm, tn = _tile(m, tm), _tile(n, tn)
    return pl.pallas_call(
        _mm_kernel,
        grid=(m // tm, n // tn),
        in_specs=[
            pl.BlockSpec((tm, k), lambda i, j: (i, 0)),
            pl.BlockSpec((k, tn), lambda i, j: (0, j)),
        ],
        out_specs=pl.BlockSpec((tm, tn), lambda i, j: (i, j)),
        out_shape=jax.ShapeDtypeStruct((m, n), out_dtype),
        compiler_params=_cp(("arbitrary", "arbitrary")),
        name=name,
    )(x, w)


def _attn_kernel(q_ref, k_ref, v_ref, cosq_ref, sinq_ref, cosk_ref, sink_ref,
                 qg_ref, kg_ref, o_ref, ks, vs, *, lc, group):
    qi = pl.program_id(2)
    rb = q_ref.shape[0]
    scale = HEAD_DIM ** -0.5

    @pl.when(qi == 0)
    def _prep():
        kn = _rms_rows(k_ref[...]) * kg_ref[...]
        ks[...] = _rope(kn, cosk_ref[...], sink_ref[...]).astype(BF16)
        vs[...] = v_ref[...].astype(BF16)

    def q_heads():
        q = q_ref[...]
        hs = []
        for r in range(group):
            qh = _rms_rows(q[:, r * HEAD_DIM:(r + 1) * HEAD_DIM]) * qg_ref[...]
            hs.append(_rope(qh, cosq_ref[...], sinq_ref[...]).astype(BF16))
        return jnp.concatenate(hs, axis=0)

    def attend(k, v):
        s = _dot_nt(q_heads(), k)
        m = jnp.max(s, axis=-1, keepdims=True)
        p = jnp.exp((s - m) * scale)
        l = jnp.sum(p, axis=-1, keepdims=True)
        o = _dot(p.astype(BF16), v) / l
        for r in range(group):
            o_ref[:, r * HEAD_DIM:(r + 1) * HEAD_DIM] = o[r * rb:(r + 1) * rb].astype(o_ref.dtype)

    @pl.when(qi == 0)
    def _ctx():
        attend(ks[0:lc], vs[0:lc])

    @pl.when(qi > 0)
    def _lat():
        attend(ks[...], vs[...])


def _attention(proj, cos2, sin2, q_gain, k_gain, bsz, seg, lc):
    t = proj.shape[0]
    nseg = seg // lc
    group = A_Q_HEADS // A_KV_HEADS
    kv_w = A_KV_HEADS
    q_col0 = (2 * kv_w + 2 * B_HEADS) // group
    kern = functools.partial(_attn_kernel, lc=lc, group=group)
    return pl.pallas_call(
        kern,
        grid=(bsz, A_KV_HEADS, nseg),
        in_specs=[
            pl.BlockSpec((lc, group * HEAD_DIM), lambda b, g, i: (b * nseg + i, q_col0 + g)),
            pl.BlockSpec((seg, HEAD_DIM), lambda b, g, i: (b, g)),
            pl.BlockSpec((seg, HEAD_DIM), lambda b, g, i: (b, kv_w + g)),
            pl.BlockSpec((lc, HEAD_DIM), lambda b, g, i: (i, 0)),
            pl.BlockSpec((lc, HEAD_DIM), lambda b, g, i: (i, 0)),
            pl.BlockSpec((seg, HEAD_DIM), lambda b, g, i: (0, 0)),
            pl.BlockSpec((seg, HEAD_DIM), lambda b, g, i: (0, 0)),
            pl.BlockSpec((1, HEAD_DIM), lambda b, g, i: (0, 0)),
            pl.BlockSpec((1, HEAD_DIM), lambda b, g, i: (0, 0)),
        ],
        out_specs=pl.BlockSpec((lc, group * HEAD_DIM), lambda b, g, i: (b * nseg + i, g)),
        out_shape=jax.ShapeDtypeStruct((t, A_Q_HEADS * HEAD_DIM), BF16),
        scratch_shapes=[pltpu.VMEM((seg, HEAD_DIM), BF16), pltpu.VMEM((seg, HEAD_DIM), BF16)],
        compiler_params=_cp(("arbitrary", "arbitrary", "arbitrary")),
        name="gqa_attention",
    )(proj, proj, proj, cos2, sin2, cos2, sin2, q_gain, k_gain)


def _ret_kernel(lg_ref, q_ref, k_ref, v_ref, g_ref, cos_ref, sin_ref, gn_ref, o_ref,
                oacc, qf_s, qb_s, uf_s, ub_s, sf_s, sb_s, *, c, nchunk, nctx):
    h = pl.program_id(1)
    lgf = lg_ref[0, h]
    lgb = lg_ref[1, h]
    cf = float(c)
    n_i = lax.broadcasted_iota(jnp.int32, (c, 1), 0).astype(F32)
    m_i = lax.broadcasted_iota(jnp.int32, (1, c), 1).astype(F32)
    diff = n_i - m_i
    dmat = (jnp.where(diff >= 0, jnp.exp(lgf * jnp.maximum(diff, 0.0)), 0.0)
            + jnp.where(diff <= 0, jnp.exp(lgb * jnp.maximum(-diff, 0.0)), 0.0))
    qdf = jnp.exp(lgf * (n_i + 1.0))
    qdb = jnp.exp(lgb * (cf - n_i))
    kdf = jnp.exp(lgf * (cf - 1.0 - n_i))
    kdb = jnp.exp(lgb * n_i)
    cdf = jnp.exp(lgf * cf)
    cdb = jnp.exp(lgb * cf)
    kscale = HEAD_DIM ** -0.5

    def phase_a(ci, carry):
        r0 = pl.multiple_of(ci * c, c)
        rows = pl.ds(r0, c)
        cos2 = cos_ref[rows, :]
        sin2 = sin_ref[rows, :]
        q = _rope(q_ref[rows, :], cos2, sin2)
        k = _rope(k_ref[rows, :], cos2, sin2) * kscale
        vb = v_ref[rows, :].astype(BF16)
        s = _dot_nt(q.astype(BF16), k.astype(BF16)) * dmat
        oacc[rows, :] = _dot(s.astype(BF16), vb)
        qf_s[rows, :] = (q * qdf).astype(BF16)
        qb_s[rows, :] = (q * qdb).astype(BF16)
        uf_s[ci] = _dot_tn((k * kdf).astype(BF16), vb)
        ub_s[ci] = _dot_tn((k * kdb).astype(BF16), vb)
        return carry

    lax.fori_loop(0, nchunk, phase_a, 0)

    def scan_f(ci, s):
        sf_s[ci] = s.astype(BF16)
        return s * cdf + uf_s[ci]

    lax.fori_loop(0, nchunk, scan_f, jnp.zeros((HEAD_DIM, HEAD_DIM), F32))

    def scan_b(i, s):
        ci = jnp.where(i < nctx, nctx - 1 - i, nchunk - 1 - (i - nctx))
        sb_s[ci] = s.astype(BF16)
        return s * cdb + ub_s[ci]

    lax.fori_loop(0, nchunk, scan_b, jnp.zeros((HEAD_DIM, HEAD_DIM), F32))

    def phase_c(ci, carry):
        r0 = pl.multiple_of(ci * c, c)
        rows = pl.ds(r0, c)
        o = oacc[rows, :] + _dot(qf_s[rows, :], sf_s[ci]) + _dot(qb_s[rows, :], sb_s[ci])
        mu = jnp.mean(o, axis=-1, keepdims=True)
        d = o - mu
        on = d * lax.rsqrt(jnp.mean(d * d, axis=-1, keepdims=True) + NORM_EPS)
        g = g_ref[rows, :]
        o_ref[rows, :] = (on * gn_ref[...] * (g * _sigmoid(g))).astype(o_ref.dtype)
        return carry

    lax.fori_loop(0, nchunk, phase_c, 0)


def _retention(proj, log_g, cos2, sin2, gn_gain, bsz, seg, lc):
    t = proj.shape[0]
    nchunk = seg // lc
    kvw = 2 * A_KV_HEADS
    k0, v0 = kvw, kvw + B_HEADS
    q0 = kvw + 2 * B_HEADS + A_Q_HEADS
    g0 = q0 + B_HEADS
    kern = functools.partial(_ret_kernel, c=lc, nchunk=nchunk, nctx=1)
    col = lambda c0: pl.BlockSpec((seg, HEAD_DIM), lambda b, h: (b, c0 + h))
    return pl.pallas_call(
        kern,
        grid=(bsz, B_HEADS),
        in_specs=[
            pl.BlockSpec(memory_space=pltpu.SMEM),
            col(q0), col(k0), col(v0), col(g0),
            pl.BlockSpec((seg, HEAD_DIM), lambda b, h: (0, 0)),
            pl.BlockSpec((seg, HEAD_DIM), lambda b, h: (0, 0)),
            pl.BlockSpec((1, HEAD_DIM), lambda b, h: (0, h)),
        ],
        out_specs=pl.BlockSpec((seg, HEAD_DIM), lambda b, h: (b, h)),
        out_shape=jax.ShapeDtypeStruct((t, B_HEADS * HEAD_DIM), BF16),
        scratch_shapes=[
            pltpu.VMEM((seg, HEAD_DIM), F32),
            pltpu.VMEM((seg, HEAD_DIM), BF16),
            pltpu.VMEM((seg, HEAD_DIM), BF16),
            pltpu.VMEM((nchunk, HEAD_DIM, HEAD_DIM), F32),
            pltpu.VMEM((nchunk, HEAD_DIM, HEAD_DIM), F32),
            pltpu.VMEM((nchunk, HEAD_DIM, HEAD_DIM), BF16),
            pltpu.VMEM((nchunk, HEAD_DIM, HEAD_DIM), BF16),
        ],
        compiler_params=_cp(("arbitrary", "arbitrary")),
        name="retention",
    )(log_g, proj, proj, proj, proj, cos2, sin2, gn_gain)


def _split3(x):
    hi = x.astype(BF16)
    r1 = x - hi.astype(F32)
    mid = r1.astype(BF16)
    lo = (r1 - mid.astype(F32)).astype(BF16)
    return hi, mid, lo


def _hgrn_kernel(ff_ref, fb_ref, v_ref, q_ref, g_ref, lb_ref, gn_ref, o_ref,
                 oacc, qf_s, qb_s, uf_s, ub_s, df_s, db_s, sf_s, sb_s, *, c, nchunk, nctx):
    lb = lb_ref[...]
    one_m_lb = 1.0 - lb
    n_i = lax.broadcasted_iota(jnp.int32, (c, c), 0)
    m_i = lax.broadcasted_iota(jnp.int32, (c, c), 1)
    lower = n_i >= m_i
    upper = m_i >= n_i
    tri_l = lower.astype(BF16)
    tri_u = upper.astype(BF16)
    mid = c // 2

    def csum(tri, x):
        hi, md, lo = _split3(x)
        return _dot(tri, hi) + _dot(tri, md) + _dot(tri, lo)

    def phase_a(ci, carry):
        r0 = pl.multiple_of(ci * c, c)
        rows = pl.ds(r0, c)
        qr = q_ref[rows, :]
        q = qr * _sigmoid(qr)
        vb = v_ref[rows, :].astype(BF16)
        frf = ff_ref[rows, :]
        frb = fb_ref[rows, :]
        kf = one_m_lb * _sigmoid(-frf)
        kb = one_m_lb * _sigmoid(-frb)
        lff = jnp.log(lb + one_m_lb * _sigmoid(frf))
        lfb = jnp.log(lb + one_m_lb * _sigmoid(frb))
        cum_f = csum(tri_l, lff)
        cum_b = csum(tri_u, lfb)
        an_f = cum_f[mid:mid + 1, :]
        an_b = cum_b[mid:mid + 1, :]
        a_f = _dot_nt((q * jnp.exp(cum_f - an_f)).astype(BF16), (kf * jnp.exp(an_f - cum_f)).astype(BF16))
        a_b = _dot_nt((q * jnp.exp(cum_b - an_b)).astype(BF16), (kb * jnp.exp(an_b - cum_b)).astype(BF16))
        a = jnp.where(lower, a_f, 0.0) + jnp.where(upper, a_b, 0.0)
        oacc[rows, :] = _dot(a.astype(BF16), vb)
        last_f = cum_f[c - 1:c, :]
        last_b = cum_b[0:1, :]
        qf_s[rows, :] = (q * jnp.exp(cum_f)).astype(BF16)
        qb_s[rows, :] = (q * jnp.exp(cum_b)).astype(BF16)
        uf_s[ci] = _dot_tn(vb, (kf * jnp.exp(last_f - cum_f)).astype(BF16))
        ub_s[ci] = _dot_tn(vb, (kb * jnp.exp(last_b - cum_b)).astype(BF16))
        df_s[ci] = jnp.exp(last_f)
        db_s[ci] = jnp.exp(last_b)
        return carry

    lax.fori_loop(0, nchunk, phase_a, 0)

    def scan_f(ci, s):
        sf_s[ci] = s.astype(BF16)
        return s * df_s[ci] + uf_s[ci]

    lax.fori_loop(0, nchunk, scan_f, jnp.zeros((HEAD_DIM, HEAD_DIM), F32))

    def scan_b(i, s):
        ci = jnp.where(i < nctx, nctx - 1 - i, nchunk - 1 - (i - nctx))
        sb_s[ci] = s.astype(BF16)
        return s * db_s[ci] + ub_s[ci]

    lax.fori_loop(0, nchunk, scan_b, jnp.zeros((HEAD_DIM, HEAD_DIM), F32))

    def phase_c(ci, carry):
        r0 = pl.multiple_of(ci * c, c)
        rows = pl.ds(r0, c)
        o = oacc[rows, :] + _dot_nt(qf_s[rows, :], sf_s[ci]) + _dot_nt(qb_s[rows, :], sb_s[ci])
        on = _rms_rows(o)
        g = g_ref[rows, :]
        o_ref[rows, :] = (on * gn_ref[...] * (g * _sigmoid(g))).astype(o_ref.dtype)
        return carry

    lax.fori_loop(0, nchunk, phase_c, 0)


def _hgrn(proj, lb, gn_gain, bsz, seg, lc):
    t = proj.shape[0]
    c = GLA_CHUNK
    nchunk = seg // c
    nctx = lc // c
    kern = functools.partial(_hgrn_kernel, c=c, nchunk=nchunk, nctx=nctx)
    col = lambda c0: pl.BlockSpec((seg, HEAD_DIM), lambda b, h: (b, c0 + h))
    vec = pl.BlockSpec((1, HEAD_DIM), lambda b, h: (0, h))
    return pl.pallas_call(
        kern,
        grid=(bsz, C_HEADS),
        in_specs=[col(0), col(C_HEADS), col(2 * C_HEADS), col(3 * C_HEADS), col(4 * C_HEADS), vec, vec],
        out_specs=pl.BlockSpec((seg, HEAD_DIM), lambda b, h: (b, h)),
        out_shape=jax.ShapeDtypeStruct((t, C_HEADS * HEAD_DIM), BF16),
        scratch_shapes=[
            pltpu.VMEM((seg, HEAD_DIM), F32),
            pltpu.VMEM((seg, HEAD_DIM), BF16),
            pltpu.VMEM((seg, HEAD_DIM), BF16),
            pltpu.VMEM((nchunk, HEAD_DIM, HEAD_DIM), F32),
            pltpu.VMEM((nchunk, HEAD_DIM, HEAD_DIM), F32),
            pltpu.VMEM((nchunk, 1, HEAD_DIM), F32),
            pltpu.VMEM((nchunk, 1, HEAD_DIM), F32),
            pltpu.VMEM((nchunk, HEAD_DIM, HEAD_DIM), BF16),
            pltpu.VMEM((nchunk, HEAD_DIM, HEAD_DIM), BF16),
        ],
        compiler_params=_cp(("arbitrary", "arbitrary")),
        name="hgrn2",
    )(proj, proj, proj, proj, proj, lb, gn_gain)


def _out_kernel(*refs, n_in):
    xs = refs[:n_in]
    ws = refs[n_in:2 * n_in]
    z_ref, gate_ref, gain_ref, sh_ref, sc_ref, wr_ref, br_ref = refs[2 * n_in:2 * n_in + 7]
    z_out, h_out, lg_out = refs[2 * n_in + 7:]
    o = _dot(xs[0][...], ws[0][...])
    for x_ref, w_ref in zip(xs[1:], ws[1:]):
        o = o + _dot(x_ref[...], w_ref[...])
    z1 = z_ref[...] + gate_ref[...] * o
    z_out[...] = z1
    h = _rms_rows(z1) * gain_ref[...]
    h = h * (1.0 + sc_ref[...]) + sh_ref[...]
    hb = h.astype(BF16)
    h_out[...] = hb
    lg_out[...] = _dot(hb, wr_ref[...]) + br_ref[...]


def _out_proj(xs, ws, z, mods, gain, wr, br, bsz, nseg, off, rb):
    d = z.shape[1]
    nout = nseg - off
    n_in = len(xs)
    npad = wr.shape[1]

    def rin(b, j):
        return (b * nseg + off + j, 0)

    def rout(b, j):
        return (b * nout + j, 0)

    def mod(which):
        return pl.BlockSpec((None, None, 1, d),
                            lambda b, j: (jnp.where(j + off == 0, bsz, b), which, 0, 0))

    in_specs = [pl.BlockSpec((rb, x.shape[1]), rin) for x in xs]
    in_specs += [pl.BlockSpec(w.shape, lambda b, j: (0, 0)) for w in ws]
    in_specs += [
        pl.BlockSpec((rb, d), rin),
        mod(2),
        pl.BlockSpec((1, d), lambda b, j: (0, 0)),
        mod(3), mod(4),
        pl.BlockSpec(wr.shape, lambda b, j: (0, 0)),
        pl.BlockSpec((1, npad), lambda b, j: (0, 0)),
    ]
    tm = bsz * nout * rb
    return pl.pallas_call(
        functools.partial(_out_kernel, n_in=n_in),
        grid=(bsz, nout),
        in_specs=in_specs,
        out_specs=[pl.BlockSpec((rb, d), rout), pl.BlockSpec((rb, d), rout), pl.BlockSpec((rb, npad), rout)],
        out_shape=[jax.ShapeDtypeStruct((tm, d), F32), jax.ShapeDtypeStruct((tm, d), BF16),
                   jax.ShapeDtypeStruct((tm, npad), F32)],
        compiler_params=_cp(("arbitrary", "arbitrary")),
        name="out_proj",
    )(*xs, *ws, z, mods, gain, mods, mods, wr, br)


def _topk_kernel(lg_ref, idx_ref, gate_ref):
    l = lg_ref[...]
    lane = lax.broadcasted_iota(jnp.int32, l.shape, 1)
    vals, idxs = [], []
    for _ in range(TOP_K):
        m = jnp.max(l, axis=-1, keepdims=True)
        i = jnp.min(jnp.where(l == m, lane, LANES), axis=-1, keepdims=True)
        vals.append(m)
        idxs.append(i)
        l = jnp.where(lane == i, -jnp.inf, l)
    es = [jnp.exp(v - vals[0]) for v in vals]
    den = es[0] + es[1] + es[2] + es[3]
    io = jnp.zeros(l.shape, jnp.int32)
    go = jnp.zeros(l.shape, F32)
    for k in range(TOP_K):
        io = jnp.where(lane == k, idxs[k], io)
        go = jnp.where(lane == k, es[k] / den, go)
    idx_ref[...] = io
    gate_ref[...] = go


def _topk(logits, tm=512):
    t, n = logits.shape
    tm = _tile(t, tm)
    return pl.pallas_call(
        _topk_kernel,
        grid=(t // tm,),
        in_specs=[pl.BlockSpec((tm, n), lambda i: (i, 0))],
        out_specs=[pl.BlockSpec((tm, n), lambda i: (i, 0)), pl.BlockSpec((tm, n), lambda i: (i, 0))],
        out_shape=[jax.ShapeDtypeStruct((t, n), jnp.int32), jax.ShapeDtypeStruct((t, n), F32)],
        compiler_params=_cp(("arbitrary",)),
        name="router_topk",
    )(logits)


def _gather_kernel(idx_ref, src_ref, out_ref, sem, *, chunk):
    base = pl.program_id(0) * chunk

    def row_copy(r, src_row):
        return pltpu.make_async_copy(src_ref.at[src_row], out_ref.at[base + r], sem)

    def issue(p, carry):
        r = 2 * p
        row_copy(r, idx_ref[0, r]).start(priority=0)
        row_copy(r + 1, idx_ref[0, r + 1]).start(priority=1)
        return carry

    lax.fori_loop(0, chunk // 2, issue, 0)

    def drain(r, carry):
        row_copy(r, 0).wait()
        return carry

    lax.fori_loop(0, chunk, drain, 0)


def _gather_rows(src, idx, chunk=GATHER_CHUNK):
    r = idx.shape[0]
    d = src.shape[1:]
    nchunks = r // chunk
    return pl.pallas_call(
        functools.partial(_gather_kernel, chunk=chunk),
        grid=(nchunks,),
        in_specs=[
            pl.BlockSpec((None, 1, chunk), lambda i: (i, 0, 0), memory_space=pltpu.SMEM),
            pl.BlockSpec(memory_space=pl.ANY),
        ],
        out_specs=pl.BlockSpec(memory_space=pl.ANY),
        out_shape=jax.ShapeDtypeStruct((r,) + d, src.dtype),
        scratch_shapes=[pltpu.SemaphoreType.DMA],
        compiler_params=_cp(("arbitrary",)),
        name="row_gather",
    )(idx.reshape(nchunks, 1, chunk), src)


def _gmm1_kernel(be_ref, first_ref, nused_ref, x_ref, w_ref, b_ref, o_ref, wsc):
    i = pl.program_id(1)
    tn = w_ref.shape[1]

    @pl.when(i < nused_ref[0])
    def _():
        @pl.when(first_ref[i] == 1)
        def _cast():
            wsc[...] = w_ref[...].astype(BF16)

        hid = _dot(x_ref[...].astype(BF16), wsc[...]) + b_ref[...]
        glu = jnp.minimum(hid, SWIGLU_LIMIT)
        glu = glu * _sigmoid(SWIGLU_ALPHA * glu)
        lin = jnp.clip(hid, -SWIGLU_LIMIT, SWIGLU_LIMIT) + 1.0
        w2 = 2 * LANES
        rsel = lax.broadcasted_iota(jnp.int32, (w2, LANES), 0)
        csel = lax.broadcasted_iota(jnp.int32, (w2, LANES), 1)
        sel = (rsel == 2 * csel).astype(BF16)
        for s in range(tn // w2):
            parts = []
            for u in range(2):
                lo = s * w2 + u * LANES
                lin_sh = pltpu.roll(lin[:, lo:lo + LANES], LANES - 1, 1)
                parts.append((glu[:, lo:lo + LANES] * lin_sh).astype(BF16))
            prod = jnp.concatenate(parts, axis=1)
            o_ref[:, s * LANES:(s + 1) * LANES] = _dot(prod, sel).astype(o_ref.dtype)

    @pl.when(i >= nused_ref[0])
    def _unused():
        o_ref[...] = jnp.zeros(o_ref.shape, o_ref.dtype)


def _gmm2_kernel(be_ref, first_ref, nused_ref, x_ref, w_ref, b_ref, o_ref, wsc):
    i = pl.program_id(1)

    @pl.when(i < nused_ref[0])
    def _():
        @pl.when(first_ref[i] == 1)
        def _cast():
            wsc[...] = w_ref[...].astype(BF16)

        o_ref[...] = (_dot(x_ref[...], wsc[...]) + b_ref[...]).astype(o_ref.dtype)

    @pl.when(i >= nused_ref[0])
    def _unused():
        o_ref[...] = jnp.zeros(o_ref.shape, o_ref.dtype)


def _gmm(kernel, x, w, b, layer, be, first, nused, bm, tn, out_cols, out_dtype, name):
    r, k = x.shape
    n = w.shape[3]
    tn = _tile(n, tn)
    nb = r // bm
    n_tiles = n // tn
    oc = out_cols // n_tiles

    def blk(i, nu):
        return jnp.minimum(i, nu[0] - 1)

    grid_spec = pltpu.PrefetchScalarGridSpec(
        num_scalar_prefetch=3,
        grid=(n_tiles, nb),
        in_specs=[
            pl.BlockSpec((bm, k), lambda j, i, be, fi, nu: (blk(i, nu), 0)),
            pl.BlockSpec((None, None, k, tn), lambda j, i, be, fi, nu: (layer, be[blk(i, nu)], 0, j)),
            pl.BlockSpec((None, None, 1, tn), lambda j, i, be, fi, nu: (layer, be[blk(i, nu)], 0, j)),
        ],
        out_specs=pl.BlockSpec((bm, oc), lambda j, i, be, fi, nu: (i, j)),
        scratch_shapes=[pltpu.VMEM((k, tn), BF16)],
    )
    return pl.pallas_call(
        kernel,
        grid_spec=grid_spec,
        out_shape=jax.ShapeDtypeStruct((r, out_cols), out_dtype),
        compiler_params=_cp(("arbitrary", "arbitrary")),
        name=name,
    )(be, first, nused, x, w, b.reshape(b.shape[0], b.shape[1], 1, n))


def _combine_kernel(z_ref, y0_ref, y1_ref, y2_ref, y3_ref, gt_ref, gate_ref, gain_ref, sh_ref, sc_ref,
                    *outs, final):
    gt = gt_ref[...]
    ffn = gt[:, 0:1] * y0_ref[...].astype(F32)
    for k, y_ref in enumerate((y1_ref, y2_ref, y3_ref), start=1):
        ffn = ffn + gt[:, k:k + 1] * y_ref[...].astype(F32)
    z2 = z_ref[...] + gate_ref[...] * ffn
    h = _rms_rows(z2) * gain_ref[...]
    if final:
        outs[0][...] = h
    else:
        outs[0][...] = z2
        outs[1][...] = (h * (1.0 + sc_ref[...]) + sh_ref[...]).astype(outs[1].dtype)


def _combine(z1, yt, gates, mods_cur, mods_next, gain, bsz, nblk, rb, final):
    t, d = z1.shape
    has_ctx = not final

    def midx(i):
        if has_ctx:
            return jnp.where(i % nblk == 0, bsz, i // nblk)
        return i // nblk

    row = lambda i: (i, 0)
    nrow = t // rb
    out_specs = [pl.BlockSpec((rb, d), row)]
    out_shape = [jax.ShapeDtypeStruct((t, d), F32)]
    if not final:
        out_specs.append(pl.BlockSpec((rb, d), row))
        out_shape.append(jax.ShapeDtypeStruct((t, d), BF16))
    y_specs = [pl.BlockSpec((rb, d), functools.partial(lambda i, k: (k * nrow + i, 0), k=k))
               for k in range(TOP_K)]
    return pl.pallas_call(
        functools.partial(_combine_kernel, final=final),
        grid=(nrow,),
        in_specs=[
            pl.BlockSpec((rb, d), row),
            *y_specs,
            pl.BlockSpec((rb, gates.shape[1]), row),
            pl.BlockSpec((None, None, 1, d), lambda i: (midx(i), 5, 0, 0)),
            pl.BlockSpec((1, d), lambda i: (0, 0)),
            pl.BlockSpec((None, None, 1, d), lambda i: (midx(i), 0, 0, 0)),
            pl.BlockSpec((None, None, 1, d), lambda i: (midx(i), 1, 0, 0)),
        ],
        out_specs=out_specs,
        out_shape=out_shape,
        compiler_params=_cp(("arbitrary",)),
        name="moe_combine",
    )(z1, yt, yt, yt, yt, gates, mods_cur, gain, mods_next, mods_next)


def _routing(idx, n_exp, bm, gchunk):
    tm = idx.shape[0]
    onehot = (idx[:, :, None] == jnp.arange(n_exp, dtype=jnp.int32)).astype(jnp.int32).sum(axis=1)
    csum = jnp.cumsum(onehot, axis=0)
    counts = csum[-1]
    rank = jnp.take_along_axis(csum - onehot, idx, axis=1)
    padded = (counts + bm - 1) // bm * bm
    pad_end = jnp.cumsum(padded)
    pad_start = pad_end - padded
    dest = pad_start[idx] + rank
    nb = (tm * TOP_K + n_exp * (bm - 1)) // bm
    nb = -(-(nb * bm) // gchunk) * gchunk // bm
    tok = jnp.repeat(jnp.arange(tm, dtype=jnp.int32), TOP_K)
    row_tok = jnp.zeros((nb * bm,), jnp.int32).at[dest.reshape(-1)].set(tok)
    nused = (pad_end[-1] // bm).astype(jnp.int32).reshape(1)
    block_start = jnp.arange(nb, dtype=jnp.int32) * bm
    be = jnp.minimum((pad_end[None, :] <= block_start[:, None]).astype(jnp.int32).sum(axis=1), n_exp - 1)
    first = jnp.concatenate([jnp.ones((1,), jnp.int32), (be[1:] != be[:-1]).astype(jnp.int32)])
    dest_kmajor = dest.T.reshape(-1).astype(jnp.int32)
    return dest_kmajor, row_tok, be, first, nused


def _moe(h2, logits, w1, b1, w2, b2, layer):
    tm, d = h2.shape
    n_exp = w1.shape[1]
    sub = d // LANES
    idx_p, gate_p = _topk(logits)
    idx = idx_p[:, :TOP_K]
    dest, row_tok, be, first, nused = _routing(idx, n_exp, MOE_BM, GATHER_CHUNK)
    xs = _gather_rows(h2.reshape(tm, sub, LANES), row_tok).reshape(-1, d)
    act = _gmm(_gmm1_kernel, xs, w1, b1, layer, be, first, nused, MOE_BM, 1024, w1.shape[3] // 2, BF16, "moe_up")
    y = _gmm(_gmm2_kernel, act, w2, b2, layer, be, first, nused, MOE_BM, 1024, w2.shape[3], BF16, "moe_down")
    yt = _gather_rows(y.reshape(-1, sub, LANES), dest).reshape(TOP_K * tm, d)
    return yt, gate_p


def _rope_tables(n_lat, lc):
    n_rows = n_lat // GRID_W
    row = jnp.repeat(jnp.arange(n_rows, dtype=F32), GRID_W)
    col = jnp.tile(jnp.arange(GRID_W, dtype=F32), n_rows)
    n_freq = HEAD_DIM // 4
    inv_freq = ROPE_THETA ** (-jnp.arange(n_freq, dtype=F32) / n_freq)
    ang = jnp.concatenate([row[:, None] * inv_freq, col[:, None] * inv_freq], axis=-1)
    cos, sin = jnp.cos(ang), jnp.sin(ang)
    cos2 = jnp.concatenate([cos, cos], axis=-1)
    sin2 = jnp.concatenate([-sin, sin], axis=-1)
    cos2 = jnp.concatenate([jnp.ones((lc, HEAD_DIM), F32), cos2], axis=0)
    sin2 = jnp.concatenate([jnp.zeros((lc, HEAD_DIM), F32), sin2], axis=0)
    return cos2, sin2


def kernel(x, c, ctx, c_ctx, ada_w, ada_b, norm_mix, norm_ffn, ab_w_in, ab_w_out, a_q_norm, a_k_norm, b_decay_exp, b_gn, c_w_in, c_w_out, c_lb, c_gn, router_w, router_b, exp_w1, exp_b1, exp_w2, exp_b2, norm_final):
    bsz, n_lat, d = x.shape
    lc = ctx.shape[1]
    depth = ada_w.shape[0]
    assert depth == 2 and n_lat % lc == 0 and bsz < MOD_ROWS and lc % GLA_CHUNK == 0
    seg = lc + n_lat
    nseg = seg // lc
    n_exp = router_w.shape[2]

    z = jnp.concatenate([ctx, x], axis=1).reshape(bsz * seg, d)

    cond = jnp.zeros((MOD_ROWS, d), F32).at[:bsz].set(c).at[bsz].set(c_ctx)
    mods = _ada(cond, ada_w, ada_b).reshape(depth, MOD_ROWS, N_MOD, 1, d)
    cos2, sin2 = _rope_tables(n_lat, lc)

    lb_soft = jax.nn.softmax(c_lb.astype(F32), axis=0)
    lower_bounds = jnp.cumsum(lb_soft, axis=0) - lb_soft[0]
    log_g = jnp.log1p(-jnp.exp2(-b_decay_exp[0].astype(F32)))

    wr = jnp.zeros((depth, d, LANES), BF16).at[:, :, :n_exp].set(router_w.astype(BF16))
    br = jnp.full((depth, 1, LANES), -1e30, F32).at[:, 0, :n_exp].set(router_b)

    hz = _norm_mod(z, norm_mix[0:1], mods[0], nseg, bsz, lc)
    proj = _matmul(hz, ab_w_in[0].astype(BF16), 1024, 512, name="in_proj0")
    att = _attention(proj, cos2, sin2, a_q_norm[0:1], a_k_norm[0:1], bsz, seg, lc)
    ret = _retention(proj, log_g, cos2, sin2, b_gn[0:1], bsz, seg, lc)
    w_out = ab_w_out[0].astype(BF16)
    aw = A_Q_HEADS * HEAD_DIM
    z1, h2, logits = _out_proj([att, ret], [w_out[:aw], w_out[aw:]], z, mods[0], norm_ffn[0:1],
                               wr[0], br[0], bsz, nseg, 0, lc)
    yt, gates = _moe(h2, logits, exp_w1, exp_b1, exp_w2, exp_b2, 0)
    z2, hz = _combine(z1, yt, gates, mods[0], mods[1], norm_mix[1:2], bsz, nseg, lc, final=False)

    proj = _matmul(hz, c_w_in[0].astype(BF16), 1024, 512, name="in_proj1")
    hg = _hgrn(proj, lower_bounds[1:2], c_gn[0:1], bsz, seg, lc)
    z1, h2, logits = _out_proj([hg], [c_w_out[0].astype(BF16)], z2, mods[1], norm_ffn[1:2],
                               wr[1], br[1], bsz, nseg, 1, lc)
    yt, gates = _moe(h2, logits, exp_w1, exp_b1, exp_w2, exp_b2, 1)
    (out,) = _combine(z1, yt, gates, mods[1], mods[1], norm_final.reshape(1, d), bsz, nseg - 1, lc, final=True)
    return out.reshape(bsz, n_lat, d)
```

```python
import functools

import jax
import jax.numpy as jnp
from jax import lax
from jax.experimental import pallas as pl
from jax.experimental.pallas import tpu as pltpu

HEAD_DIM = 128
GRID_W = 64
ROPE_THETA = 10000.0
NORM_EPS = 1e-6
N_MOD = 6
A_Q_HEADS = 8
A_KV_HEADS = 2
B_HEADS = 8
C_HEADS = 16
TOP_K = 4
SWIGLU_ALPHA = 1.702
SWIGLU_LIMIT = 7.0

LANES = 128
MOD_ROWS = 16
GLA_CHUNK = 64
MOE_BM = 512
GATHER_CHUNK = 512
VMEM_LIMIT = 56 * 1024 * 1024

F32 = jnp.float32
BF16 = jnp.bfloat16


def _cp(sem, vmem=VMEM_LIMIT):
    return pltpu.CompilerParams(dimension_semantics=sem, vmem_limit_bytes=vmem)


def _tile(n, pref):
    t = pref
    while n % t:
        t //= 2
    return t


def _dot(a, b):
    return jnp.dot(a, b, preferred_element_type=F32)


def _dot_nt(a, b):
    return lax.dot_general(a, b, (((1,), (1,)), ((), ())), preferred_element_type=F32)


def _dot_tn(a, b):
    return lax.dot_general(a, b, (((0,), (0,)), ((), ())), preferred_element_type=F32)


def _sigmoid(x):
    return 1.0 / (1.0 + jnp.exp(-x))


def _rope(t, cos2, sin2):
    return t * cos2 + pltpu.roll(t, HEAD_DIM // 2, 1) * sin2


def _rms_rows(x):
    return x * lax.rsqrt(jnp.mean(x * x, axis=-1, keepdims=True) + NORM_EPS)


def _ada_kernel(c_ref, w_ref, b_ref, o_ref):
    c = c_ref[...]
    s = (c * _sigmoid(c)).astype(BF16)
    o_ref[...] = _dot(s, w_ref[...].astype(BF16)) + b_ref[...]


def _ada(cond, ada_w, ada_b, tn=1024):
    depth, d, n = ada_w.shape
    return pl.pallas_call(
        _ada_kernel,
        grid=(depth, n // tn),
        in_specs=[
            pl.BlockSpec((MOD_ROWS, d), lambda l, j: (0, 0)),
            pl.BlockSpec((None, d, tn), lambda l, j: (l, 0, j)),
            pl.BlockSpec((None, 1, tn), lambda l, j: (l, 0, j)),
        ],
        out_specs=pl.BlockSpec((None, MOD_ROWS, tn), lambda l, j: (l, 0, j)),
        out_shape=jax.ShapeDtypeStruct((depth, MOD_ROWS, n), F32),
        compiler_params=_cp(("arbitrary", "arbitrary")),
        name="ada_mod",
    )(cond, ada_w, ada_b.reshape(depth, 1, n))


def _norm_mod_kernel(z_ref, g_ref, sh_ref, sc_ref, o_ref):
    y = _rms_rows(z_ref[...]) * g_ref[...]
    o_ref[...] = (y * (1.0 + sc_ref[...]) + sh_ref[...]).astype(o_ref.dtype)


def _norm_mod(z, gain, mods, nseg, bsz, rb):
    t, d = z.shape

    def midx(i):
        return jnp.where(i % nseg == 0, bsz, i // nseg)

    return pl.pallas_call(
        _norm_mod_kernel,
        grid=(t // rb,),
        in_specs=[
            pl.BlockSpec((rb, d), lambda i: (i, 0)),
            pl.BlockSpec((1, d), lambda i: (0, 0)),
            pl.BlockSpec((None, None, 1, d), lambda i: (midx(i), 0, 0, 0)),
            pl.BlockSpec((None, None, 1, d), lambda i: (midx(i), 1, 0, 0)),
        ],
        out_specs=pl.BlockSpec((rb, d), lambda i: (i, 0)),
        out_shape=jax.ShapeDtypeStruct((t, d), BF16),
        compiler_params=_cp(("arbitrary",)),
        name="norm_mod",
    )(z, gain, mods, mods)


def _mm_kernel(x_ref, w_ref, o_ref):
    o_ref[...] = _dot(x_ref[...], w_ref[...]).astype(o_ref.dtype)


def _matmul(x, w, tm, tn, out_dtype=F32, name="matmul"):
    m, k = x.shape
    n = w.shape[1]
    tm, tn = _tile(m, tm), _tile(n, tn)
    return pl.pallas_call(
        _mm_kernel,
        grid=(m // tm, n // tn),
        in_specs=[
            pl.BlockSpec((tm, k), lambda i, j: (i, 0)),
            pl.BlockSpec((k, tn), lambda i, j: (0, j)),
        ],
        out_specs=pl.BlockSpec((tm, tn), lambda i, j: (i, j)),
        out_shape=jax.ShapeDtypeStruct((m, n), out_dtype),
        compiler_params=_cp(("arbitrary", "arbitrary")),
        name=name,
    )(x, w)


def _attn_kernel(q_ref, k_ref, v_ref, cosq_ref, sinq_ref, cosk_ref, sink_ref,
                 qg_ref, kg_ref, o_ref, ks, vs, *, lc, group):
    qi = pl.program_id(2)
    rb = q_ref.shape[0]
    scale = HEAD_DIM ** -0.5

    @pl.when(qi == 0)
    def _prep():
        kn = _rms_rows(k_ref[...]) * kg_ref[...]
        ks[...] = _rope(kn, cosk_ref[...], sink_ref[...]).astype(BF16)
        vs[...] = v_ref[...].astype(BF16)

    def q_heads():
        q = q_ref[...]
        hs = []
        for r in range(group):
            qh = _rms_rows(q[:, r * HEAD_DIM:(r + 1) * HEAD_DIM]) * qg_ref[...]
            hs.append(_rope(qh, cosq_ref[...], sinq_ref[...]).astype(BF16))
        return jnp.concatenate(hs, axis=0)

    def attend(k, v):
        s = _dot_nt(q_heads(), k)
        m = jnp.max(s, axis=-1, keepdims=True)
        p = jnp.exp((s - m) * scale)
        l = jnp.sum(p, axis=-1, keepdims=True)
        o = _dot(p.astype(BF16), v) / l
        for r in range(group):
            o_ref[:, r * HEAD_DIM:(r + 1) * HEAD_DIM] = o[r * rb:(r + 1) * rb].astype(o_ref.dtype)

    @pl.when(qi == 0)
    def _ctx():
        attend(ks[0:lc], vs[0:lc])

    @pl.when(qi > 0)
    def _lat():
        attend(ks[...], vs[...])


def _attention(proj, cos2, sin2, q_gain, k_gain, bsz, seg, lc):
    t = proj.shape[0]
    nseg = seg // lc
    group = A_Q_HEADS // A_KV_HEADS
    kv_w = A_KV_HEADS
    q_col0 = (2 * kv_w + 2 * B_HEADS) // group
    kern = functools.partial(_attn_kernel, lc=lc, group=group)
    return pl.pallas_call(
        kern,
        grid=(bsz, A_KV_HEADS, nseg),
        in_specs=[
            pl.BlockSpec((lc, group * HEAD_DIM), lambda b, g, i: (b * nseg + i, q_col0 + g)),
            pl.BlockSpec((seg, HEAD_DIM), lambda b, g, i: (b, g)),
            pl.BlockSpec((seg, HEAD_DIM), lambda b, g, i: (b, kv_w + g)),
            pl.BlockSpec((lc, HEAD_DIM), lambda b, g, i: (i, 0)),
            pl.BlockSpec((lc, HEAD_DIM), lambda b, g, i: (i, 0)),
            pl.BlockSpec((seg, HEAD_DIM), lambda b, g, i: (0, 0)),
            pl.BlockSpec((seg, HEAD_DIM), lambda b, g, i: (0, 0)),
            pl.BlockSpec((1, HEAD_DIM), lambda b, g, i: (0, 0)),
            pl.BlockSpec((1, HEAD_DIM), lambda b, g, i: (0, 0)),
        ],
        out_specs=pl.BlockSpec((lc, group * HEAD_DIM), lambda b, g, i: (b * nseg + i, g)),
        out_shape=jax.ShapeDtypeStruct((t, A_Q_HEADS * HEAD_DIM), BF16),
        scratch_shapes=[pltpu.VMEM((seg, HEAD_DIM), BF16), pltpu.VMEM((seg, HEAD_DIM), BF16)],
        compiler_params=_cp(("arbitrary", "arbitrary", "arbitrary")),
        name="gqa_attention",
    )(proj, proj, proj, cos2, sin2, cos2, sin2, q_gain, k_gain)


def _ret_kernel(lg_ref, q_ref, k_ref, v_ref, g_ref, cos_ref, sin_ref, gn_ref, o_ref,
                oacc, qf_s, qb_s, uf_s, ub_s, sf_s, sb_s, *, c, nchunk, nctx):
    h = pl.program_id(1)
    lgf = lg_ref[0, h]
    lgb = lg_ref[1, h]
    cf = float(c)
    n_i = lax.broadcasted_iota(jnp.int32, (c, 1), 0).astype(F32)
    m_i = lax.broadcasted_iota(jnp.int32, (1, c), 1).astype(F32)
    diff = n_i - m_i
    dmat = (jnp.where(diff >= 0, jnp.exp(lgf * jnp.maximum(diff, 0.0)), 0.0)
            + jnp.where(diff <= 0, jnp.exp(lgb * jnp.maximum(-diff, 0.0)), 0.0))
    qdf = jnp.exp(lgf * (n_i + 1.0))
    qdb = jnp.exp(lgb * (cf - n_i))
    kdf = jnp.exp(lgf * (cf - 1.0 - n_i))
    kdb = jnp.exp(lgb * n_i)
    cdf = jnp.exp(lgf * cf)
    cdb = jnp.exp(lgb * cf)
    kscale = HEAD_DIM ** -0.5

    def phase_a(ci, carry):
        r0 = pl.multiple_of(ci * c, c)
        rows = pl.ds(r0, c)
        cos2 = cos_ref[rows, :]
        sin2 = sin_ref[rows, :]
        q = _rope(q_ref[rows, :], cos2, sin2)
        k = _rope(k_ref[rows, :], cos2, sin2) * kscale
        vb = v_ref[rows, :].astype(BF16)
        s = _dot_nt(q.astype(BF16), k.astype(BF16)) * dmat
        oacc[rows, :] = _dot(s.astype(BF16), vb)
        qf_s[rows, :] = (q * qdf).astype(BF16)
        qb_s[rows, :] = (q * qdb).astype(BF16)
        uf_s[ci] = _dot_tn((k * kdf).astype(BF16), vb)
        ub_s[ci] = _dot_tn((k * kdb).astype(BF16), vb)
        return carry

    lax.fori_loop(0, nchunk, phase_a, 0)

    def scan_f(ci, s):
        sf_s[ci] = s.astype(BF16)
        return s * cdf + uf_s[ci]

    lax.fori_loop(0, nchunk, scan_f, jnp.zeros((HEAD_DIM, HEAD_DIM), F32))

    def scan_b(i, s):
        ci = jnp.where(i < nctx, nctx - 1 - i, nchunk - 1 - (i - nctx))
        sb_s[ci] = s.astype(BF16)
        return s * cdb + ub_s[ci]

    lax.fori_loop(0, nchunk, scan_b, jnp.zeros((HEAD_DIM, HEAD_DIM), F32))

    def phase_c(ci, carry):
        r0 = pl.multiple_of(ci * c, c)
        rows = pl.ds(r0, c)
        o = oacc[rows, :] + _dot(qf_s[rows, :], sf_s[ci]) + _dot(qb_s[rows, :], sb_s[ci])
        mu = jnp.mean(o, axis=-1, keepdims=True)
        d = o - mu
        on = d * lax.rsqrt(jnp.mean(d * d, axis=-1, keepdims=True) + NORM_EPS)
        g = g_ref[rows, :]
        o_ref[rows, :] = (on * gn_ref[...] * (g * _sigmoid(g))).astype(o_ref.dtype)
        return carry

    lax.fori_loop(0, nchunk, phase_c, 0)


def _retention(proj, log_g, cos2, sin2, gn_gain, bsz, seg, lc):
    t = proj.shape[0]
    nchunk = seg // lc
    kvw = 2 * A_KV_HEADS
    k0, v0 = kvw, kvw + B_HEADS
    q0 = kvw + 2 * B_HEADS + A_Q_HEADS
    g0 = q0 + B_HEADS
    kern = functools.partial(_ret_kernel, c=lc, nchunk=nchunk, nctx=1)
    col = lambda c0: pl.BlockSpec((seg, HEAD_DIM), lambda b, h: (b, c0 + h))
    return pl.pallas_call(
        kern,
        grid=(bsz, B_HEADS),
        in_specs=[
            pl.BlockSpec(memory_space=pltpu.SMEM),
            col(q0), col(k0), col(v0), col(g0),
            pl.BlockSpec((seg, HEAD_DIM), lambda b, h: (0, 0)),
            pl.BlockSpec((seg, HEAD_DIM), lambda b, h: (0, 0)),
            pl.BlockSpec((1, HEAD_DIM), lambda b, h: (0, h)),
        ],
        out_specs=pl.BlockSpec((seg, HEAD_DIM), lambda b, h: (b, h)),
        out_shape=jax.ShapeDtypeStruct((t, B_HEADS * HEAD_DIM), BF16),
        scratch_shapes=[
            pltpu.VMEM((seg, HEAD_DIM), F32),
            pltpu.VMEM((seg, HEAD_DIM), BF16),
            pltpu.VMEM((seg, HEAD_DIM), BF16),
            pltpu.VMEM((nchunk, HEAD_DIM, HEAD_DIM), F32),
            pltpu.VMEM((nchunk, HEAD_DIM, HEAD_DIM), F32),
            pltpu.VMEM((nchunk, HEAD_DIM, HEAD_DIM), BF16),
            pltpu.VMEM((nchunk, HEAD_DIM, HEAD_DIM), BF16),
        ],
        compiler_params=_cp(("arbitrary", "arbitrary")),
        name="retention",
    )(log_g, proj, proj, proj, proj, cos2, sin2, gn_gain)


def _split3(x):
    hi = x.astype(BF16)
    r1 = x - hi.astype(F32)
    mid = r1.astype(BF16)
    lo = (r1 - mid.astype(F32)).astype(BF16)
    return hi, mid, lo


def _hgrn_kernel(ff_ref, fb_ref, v_ref, q_ref, g_ref, lb_ref, gn_ref, o_ref,
                 oacc, qf_s, qb_s, uf_s, ub_s, df_s, db_s, sf_s, sb_s, *, c, nchunk, nctx):
    lb = lb_ref[...]
    one_m_lb = 1.0 - lb
    n_i = lax.broadcasted_iota(jnp.int32, (c, c), 0)
    m_i = lax.broadcasted_iota(jnp.int32, (c, c), 1)
    lower = n_i >= m_i
    upper = m_i >= n_i
    tri_l = lower.astype(BF16)
    tri_u = upper.astype(BF16)
    mid = c // 2

    def csum(tri, x):
        hi, md, lo = _split3(x)
        return _dot(tri, hi) + _dot(tri, md) + _dot(tri, lo)

    def phase_a(ci, carry):
        r0 = pl.multiple_of(ci * c, c)
        rows = pl.ds(r0, c)
        qr = q_ref[rows, :]
        q = qr * _sigmoid(qr)
        vb = v_ref[rows, :].astype(BF16)
        frf = ff_ref[rows, :]
        frb = fb_ref[rows, :]
        kf = one_m_lb * _sigmoid(-frf)
        kb = one_m_lb * _sigmoid(-frb)
        lff = jnp.log(lb + one_m_lb * _sigmoid(frf))
        lfb = jnp.log(lb + one_m_lb * _sigmoid(frb))
        cum_f = csum(tri_l, lff)
        cum_b = csum(tri_u, lfb)
        an_f = cum_f[mid:mid + 1, :]
        an_b = cum_b[mid:mid + 1, :]
        a_f = _dot_nt((q * jnp.exp(cum_f - an_f)).astype(BF16), (kf * jnp.exp(an_f - cum_f)).astype(BF16))
        a_b = _dot_nt((q * jnp.exp(cum_b - an_b)).astype(BF16), (kb * jnp.exp(an_b - cum_b)).astype(BF16))
        a = jnp.where(lower, a_f, 0.0) + jnp.where(upper, a_b, 0.0)
        oacc[rows, :] = _dot(a.astype(BF16), vb)
        last_f = cum_f[c - 1:c, :]
        last_b = cum_b[0:1, :]
        qf_s[rows, :] = (q * jnp.exp(cum_f)).astype(BF16)
        qb_s[rows, :] = (q * jnp.exp(cum_b)).astype(BF16)
        uf_s[ci] = _dot_tn(vb, (kf * jnp.exp(last_f - cum_f)).astype(BF16))
        ub_s[ci] = _dot_tn(vb, (kb * jnp.exp(last_b - cum_b)).astype(BF16))
        df_s[ci] = jnp.exp(last_f)
        db_s[ci] = jnp.exp(last_b)
        return carry

    unroll = 4 if nchunk % 4 == 0 else (2 if nchunk % 2 == 0 else 1)
    lax.fori_loop(0, nchunk, phase_a, 0, unroll=unroll)

    def scan_f(ci, s):
        sf_s[ci] = s.astype(BF16)
        return s * df_s[ci] + uf_s[ci]

    lax.fori_loop(0, nchunk, scan_f, jnp.zeros((HEAD_DIM, HEAD_DIM), F32))

    def scan_b(i, s):
        ci = jnp.where(i < nctx, nctx - 1 - i, nchunk - 1 - (i - nctx))
        sb_s[ci] = s.astype(BF16)
        return s * db_s[ci] + ub_s[ci]

    lax.fori_loop(0, nchunk, scan_b, jnp.zeros((HEAD_DIM, HEAD_DIM), F32))

    def phase_c(ci, carry):
        r0 = pl.multiple_of(ci * c, c)
        rows = pl.ds(r0, c)
        o = oacc[rows, :] + _dot_nt(qf_s[rows, :], sf_s[ci]) + _dot_nt(qb_s[rows, :], sb_s[ci])
        on = _rms_rows(o)
        g = g_ref[rows, :]
        o_ref[rows, :] = (on * gn_ref[...] * (g * _sigmoid(g))).astype(o_ref.dtype)
        return carry

    lax.fori_loop(0, nchunk, phase_c, 0, unroll=unroll)


def _hgrn(proj, lb, gn_gain, bsz, seg, lc):
    t = proj.shape[0]
    c = GLA_CHUNK
    nchunk = seg // c
    nctx = lc // c
    kern = functools.partial(_hgrn_kernel, c=c, nchunk=nchunk, nctx=nctx)
    col = lambda c0: pl.BlockSpec((seg, HEAD_DIM), lambda b, h: (b, c0 + h))
    vec = pl.BlockSpec((1, HEAD_DIM), lambda b, h: (0, h))
    return pl.pallas_call(
        kern,
        grid=(bsz, C_HEADS),
        in_specs=[col(0), col(C_HEADS), col(2 * C_HEADS), col(3 * C_HEADS), col(4 * C_HEADS), vec, vec],
        out_specs=pl.BlockSpec((seg, HEAD_DIM), lambda b, h: (b, h)),
        out_shape=jax.ShapeDtypeStruct((t, C_HEADS * HEAD_DIM), BF16),
        scratch_shapes=[
            pltpu.VMEM((seg, HEAD_DIM), F32),
            pltpu.VMEM((seg, HEAD_DIM), BF16),
            pltpu.VMEM((seg, HEAD_DIM), BF16),
            pltpu.VMEM((nchunk, HEAD_DIM, HEAD_DIM), F32),
            pltpu.VMEM((nchunk, HEAD_DIM, HEAD_DIM), F32),
            pltpu.VMEM((nchunk, 1, HEAD_DIM), F32),
            pltpu.VMEM((nchunk, 1, HEAD_DIM), F32),
            pltpu.VMEM((nchunk, HEAD_DIM, HEAD_DIM), BF16),
            pltpu.VMEM((nchunk, HEAD_DIM, HEAD_DIM), BF16),
        ],
        compiler_params=_cp(("arbitrary", "arbitrary")),
        name="hgrn2",
    )(proj, proj, proj, proj, proj, lb, gn_gain)


def _out_kernel(*refs, n_in):
    xs = refs[:n_in]
    ws = refs[n_in:2 * n_in]
    z_ref, gate_ref, gain_ref, sh_ref, sc_ref, wr_ref, br_ref = refs[2 * n_in:2 * n_in + 7]
    z_out, h_out, lg_out = refs[2 * n_in + 7:]
    o = _dot(xs[0][...], ws[0][...])
    for x_ref, w_ref in zip(xs[1:], ws[1:]):
        o = o + _dot(x_ref[...], w_ref[...])
    z1 = z_ref[...] + gate_ref[...] * o
    z_out[...] = z1
    h = _rms_rows(z1) * gain_ref[...]
    h = h * (1.0 + sc_ref[...]) + sh_ref[...]
    hb = h.astype(BF16)
    h_out[...] = hb
    lg_out[...] = _dot(hb, wr_ref[...]) + br_ref[...]


def _out_proj(xs, ws, z, mods, gain, wr, br, bsz, nseg, off, rb):
    d = z.shape[1]
    nout = nseg - off
    n_in = len(xs)
    npad = wr.shape[1]

    def rin(b, j):
        return (b * nseg + off + j, 0)

    def rout(b, j):
        return (b * nout + j, 0)

    def mod(which):
        return pl.BlockSpec((None, None, 1, d),
                            lambda b, j: (jnp.where(j + off == 0, bsz, b), which, 0, 0))

    in_specs = [pl.BlockSpec((rb, x.shape[1]), rin) for x in xs]
    in_specs += [pl.BlockSpec(w.shape, lambda b, j: (0, 0)) for w in ws]
    in_specs += [
        pl.BlockSpec((rb, d), rin),
        mod(2),
        pl.BlockSpec((1, d), lambda b, j: (0, 0)),
        mod(3), mod(4),
        pl.BlockSpec(wr.shape, lambda b, j: (0, 0)),
        pl.BlockSpec((1, npad), lambda b, j: (0, 0)),
    ]
    tm = bsz * nout * rb
    return pl.pallas_call(
        functools.partial(_out_kernel, n_in=n_in),
        grid=(bsz, nout),
        in_specs=in_specs,
        out_specs=[pl.BlockSpec((rb, d), rout), pl.BlockSpec((rb, d), rout), pl.BlockSpec((rb, npad), rout)],
        out_shape=[jax.ShapeDtypeStruct((tm, d), F32), jax.ShapeDtypeStruct((tm, d), BF16),
                   jax.ShapeDtypeStruct((tm, npad), F32)],
        compiler_params=_cp(("arbitrary", "arbitrary")),
        name="out_proj",
    )(*xs, *ws, z, mods, gain, mods, mods, wr, br)


def _topk_kernel(lg_ref, idx_ref, gate_ref):
    l = lg_ref[...]
    lane = lax.broadcasted_iota(jnp.int32, l.shape, 1)
    vals, idxs = [], []
    for _ in range(TOP_K):
        m = jnp.max(l, axis=-1, keepdims=True)
        i = jnp.min(jnp.where(l == m, lane, LANES), axis=-1, keepdims=True)
        vals.append(m)
        idxs.append(i)
        l = jnp.where(lane == i, -jnp.inf, l)
    es = [jnp.exp(v - vals[0]) for v in vals]
    den = es[0] + es[1] + es[2] + es[3]
    io = jnp.zeros(l.shape, jnp.int32)
    go = jnp.zeros(l.shape, F32)
    for k in range(TOP_K):
        io = jnp.where(lane == k, idxs[k], io)
        go = jnp.where(lane == k, es[k] / den, go)
    idx_ref[...] = io
    gate_ref[...] = go


def _topk(logits, tm=512):
    t, n = logits.shape
    tm = _tile(t, tm)
    return pl.pallas_call(
        _topk_kernel,
        grid=(t // tm,),
        in_specs=[pl.BlockSpec((tm, n), lambda i: (i, 0))],
        out_specs=[pl.BlockSpec((tm, n), lambda i: (i, 0)), pl.BlockSpec((tm, n), lambda i: (i, 0))],
        out_shape=[jax.ShapeDtypeStruct((t, n), jnp.int32), jax.ShapeDtypeStruct((t, n), F32)],
        compiler_params=_cp(("arbitrary",)),
        name="router_topk",
    )(logits)


def _gather_kernel(idx_ref, src_ref, out_ref, buf, gsem, osem, *, chunk):
    i = pl.program_id(0)
    last = pl.num_programs(0) - 1
    slot = i % 2

    def out_copy(step, s):
        return pltpu.make_async_copy(buf.at[s], out_ref.at[pl.ds(step * chunk, chunk)], osem.at[s])

    def row_copy(r, src_row):
        return pltpu.make_async_copy(src_ref.at[src_row], buf.at[slot, r], gsem)

    @pl.when(i >= 2)
    def _free_slot():
        out_copy(i - 2, slot).wait()

    def issue(p, carry):
        r = 2 * p
        row_copy(r, idx_ref[0, r]).start(priority=0)
        row_copy(r + 1, idx_ref[0, r + 1]).start(priority=1)
        return carry

    lax.fori_loop(0, chunk // 2, issue, 0)

    def drain(r, carry):
        row_copy(r, 0).wait()
        return carry

    lax.fori_loop(0, chunk, drain, 0)
    out_copy(i, slot).start()

    @pl.when(i == last)
    def _flush():
        out_copy(i, slot).wait()

        @pl.when(i >= 1)
        def _prev():
            out_copy(i - 1, 1 - slot).wait()


def _gather_rows(src, idx, chunk=GATHER_CHUNK):
    r = idx.shape[0]
    d = src.shape[1:]
    nchunks = r // chunk
    return pl.pallas_call(
        functools.partial(_gather_kernel, chunk=chunk),
        grid=(nchunks,),
        in_specs=[
            pl.BlockSpec((None, 1, chunk), lambda i: (i, 0, 0), memory_space=pltpu.SMEM),
            pl.BlockSpec(memory_space=pl.ANY),
        ],
        out_specs=pl.BlockSpec(memory_space=pl.ANY),
        out_shape=jax.ShapeDtypeStruct((r,) + d, src.dtype),
        scratch_shapes=[pltpu.VMEM((2, chunk) + d, src.dtype), pltpu.SemaphoreType.DMA,
                        pltpu.SemaphoreType.DMA((2,))],
        compiler_params=_cp(("arbitrary",)),
        name="row_gather",
    )(idx.reshape(nchunks, 1, chunk), src)


def _gmm1_kernel(be_ref, first_ref, nused_ref, x_ref, w_ref, b_ref, o_ref, wsc):
    i = pl.program_id(1)
    tn = w_ref.shape[1]

    @pl.when(i < nused_ref[0])
    def _():
        @pl.when(first_ref[i] == 1)
        def _cast():
            wsc[...] = w_ref[...].astype(BF16)

        hid = _dot(x_ref[...].astype(BF16), wsc[...]) + b_ref[...]
        glu = jnp.minimum(hid, SWIGLU_LIMIT)
        glu = glu * _sigmoid(SWIGLU_ALPHA * glu)
        lin = jnp.clip(hid, -SWIGLU_LIMIT, SWIGLU_LIMIT) + 1.0
        w2 = 2 * LANES
        rsel = lax.broadcasted_iota(jnp.int32, (w2, LANES), 0)
        csel = lax.broadcasted_iota(jnp.int32, (w2, LANES), 1)
        sel = (rsel == 2 * csel).astype(BF16)
        for s in range(tn // w2):
            parts = []
            for u in range(2):
                lo = s * w2 + u * LANES
                lin_sh = pltpu.roll(lin[:, lo:lo + LANES], LANES - 1, 1)
                parts.append((glu[:, lo:lo + LANES] * lin_sh).astype(BF16))
            prod = jnp.concatenate(parts, axis=1)
            o_ref[:, s * LANES:(s + 1) * LANES] = _dot(prod, sel).astype(o_ref.dtype)

    @pl.when(i >= nused_ref[0])
    def _unused():
        o_ref[...] = jnp.zeros(o_ref.shape, o_ref.dtype)


def _gmm2_kernel(be_ref, first_ref, nused_ref, x_ref, w_ref, b_ref, o_ref, wsc):
    i = pl.program_id(1)

    @pl.when(i < nused_ref[0])
    def _():
        @pl.when(first_ref[i] == 1)
        def _cast():
            wsc[...] = w_ref[...].astype(BF16)

        o_ref[...] = (_dot(x_ref[...], wsc[...]) + b_ref[...]).astype(o_ref.dtype)

    @pl.when(i >= nused_ref[0])
    def _unused():
        o_ref[...] = jnp.zeros(o_ref.shape, o_ref.dtype)


def _gmm(kernel, x, w, b, layer, be, first, nused, bm, tn, out_cols, out_dtype, name):
    r, k = x.shape
    n = w.shape[3]
    tn = _tile(n, tn)
    nb = r // bm
    n_tiles = n // tn
    oc = out_cols // n_tiles

    def blk(i, nu):
        return jnp.minimum(i, nu[0] - 1)

    grid_spec = pltpu.PrefetchScalarGridSpec(
        num_scalar_prefetch=3,
        grid=(n_tiles, nb),
        in_specs=[
            pl.BlockSpec((bm, k), lambda j, i, be, fi, nu: (blk(i, nu), 0)),
            pl.BlockSpec((None, None, k, tn), lambda j, i, be, fi, nu: (layer, be[blk(i, nu)], 0, j)),
            pl.BlockSpec((None, None, 1, tn), lambda j, i, be, fi, nu: (layer, be[blk(i, nu)], 0, j)),
        ],
        out_specs=pl.BlockSpec((bm, oc), lambda j, i, be, fi, nu: (i, j)),
        scratch_shapes=[pltpu.VMEM((k, tn), BF16)],
    )
    return pl.pallas_call(
        kernel,
        grid_spec=grid_spec,
        out_shape=jax.ShapeDtypeStruct((r, out_cols), out_dtype),
        compiler_params=_cp(("arbitrary", "arbitrary")),
        name=name,
    )(be, first, nused, x, w, b.reshape(b.shape[0], b.shape[1], 1, n))


def _combine_kernel(z_ref, y0_ref, y1_ref, y2_ref, y3_ref, gt_ref, gate_ref, gain_ref, sh_ref, sc_ref,
                    *outs, final):
    gt = gt_ref[...]
    ffn = gt[:, 0:1] * y0_ref[...].astype(F32)
    for k, y_ref in enumerate((y1_ref, y2_ref, y3_ref), start=1):
        ffn = ffn + gt[:, k:k + 1] * y_ref[...].astype(F32)
    z2 = z_ref[...] + gate_ref[...] * ffn
    h = _rms_rows(z2) * gain_ref[...]
    if final:
        outs[0][...] = h
    else:
        outs[0][...] = z2
        outs[1][...] = (h * (1.0 + sc_ref[...]) + sh_ref[...]).astype(outs[1].dtype)


def _combine(z1, yt, gates, mods_cur, mods_next, gain, bsz, nblk, rb, final):
    t, d = z1.shape
    has_ctx = not final

    def midx(i):
        if has_ctx:
            return jnp.where(i % nblk == 0, bsz, i // nblk)
        return i // nblk

    row = lambda i: (i, 0)
    nrow = t // rb
    out_specs = [pl.BlockSpec((rb, d), row)]
    out_shape = [jax.ShapeDtypeStruct((t, d), F32)]
    if not final:
        out_specs.append(pl.BlockSpec((rb, d), row))
        out_shape.append(jax.ShapeDtypeStruct((t, d), BF16))
    y_specs = [pl.BlockSpec((rb, d), functools.partial(lambda i, k: (k * nrow + i, 0), k=k))
               for k in range(TOP_K)]
    return pl.pallas_call(
        functools.partial(_combine_kernel, final=final),
        grid=(nrow,),
        in_specs=[
            pl.BlockSpec((rb, d), row),
            *y_specs,
            pl.BlockSpec((rb, gates.shape[1]), row),
            pl.BlockSpec((None, None, 1, d), lambda i: (midx(i), 5, 0, 0)),
            pl.BlockSpec((1, d), lambda i: (0, 0)),
            pl.BlockSpec((None, None, 1, d), lambda i: (midx(i), 0, 0, 0)),
            pl.BlockSpec((None, None, 1, d), lambda i: (midx(i), 1, 0, 0)),
        ],
        out_specs=out_specs,
        out_shape=out_shape,
        compiler_params=_cp(("arbitrary",)),
        name="moe_combine",
    )(z1, yt, yt, yt, yt, gates, mods_cur, gain, mods_next, mods_next)


def _routing(idx, n_exp, bm, gchunk):
    tm = idx.shape[0]
    onehot = (idx[:, :, None] == jnp.arange(n_exp, dtype=jnp.int32)).astype(jnp.int32).sum(axis=1)
    csum = jnp.cumsum(onehot, axis=0)
    counts = csum[-1]
    rank = jnp.take_along_axis(csum - onehot, idx, axis=1)
    padded = (counts + bm - 1) // bm * bm
    pad_end = jnp.cumsum(padded)
    pad_start = pad_end - padded
    dest = pad_start[idx] + rank
    nb = (tm * TOP_K + n_exp * (bm - 1)) // bm
    nb = -(-(nb * bm) // gchunk) * gchunk // bm
    tok = jnp.repeat(jnp.arange(tm, dtype=jnp.int32), TOP_K)
    row_tok = jnp.zeros((nb * bm,), jnp.int32).at[dest.reshape(-1)].set(tok)
    nused = (pad_end[-1] // bm).astype(jnp.int32).reshape(1)
    block_start = jnp.arange(nb, dtype=jnp.int32) * bm
    be = jnp.minimum((pad_end[None, :] <= block_start[:, None]).astype(jnp.int32).sum(axis=1), n_exp - 1)
    first = jnp.concatenate([jnp.ones((1,), jnp.int32), (be[1:] != be[:-1]).astype(jnp.int32)])
    dest_kmajor = dest.T.reshape(-1).astype(jnp.int32)
    return dest_kmajor, row_tok, be, first, nused


def _moe(h2, logits, w1, b1, w2, b2, layer):
    tm, d = h2.shape
    n_exp = w1.shape[1]
    sub = d // LANES
    idx_p, gate_p = _topk(logits)
    idx = idx_p[:, :TOP_K]
    dest, row_tok, be, first, nused = _routing(idx, n_exp, MOE_BM, GATHER_CHUNK)
    xs = _gather_rows(h2.reshape(tm, sub, LANES), row_tok).reshape(-1, d)
    act = _gmm(_gmm1_kernel, xs, w1, b1, layer, be, first, nused, MOE_BM, 1024, w1.shape[3] // 2, BF16, "moe_up")
    y = _gmm(_gmm2_kernel, act, w2, b2, layer, be, first, nused, MOE_BM, 1024, w2.shape[3], BF16, "moe_down")
    yt = _gather_rows(y.reshape(-1, sub, LANES), dest).reshape(TOP_K * tm, d)
    return yt, gate_p


def _rope_tables(n_lat, lc):
    n_rows = n_lat // GRID_W
    row = jnp.repeat(jnp.arange(n_rows, dtype=F32), GRID_W)
    col = jnp.tile(jnp.arange(GRID_W, dtype=F32), n_rows)
    n_freq = HEAD_DIM // 4
    inv_freq = ROPE_THETA ** (-jnp.arange(n_freq, dtype=F32) / n_freq)
    ang = jnp.concatenate([row[:, None] * inv_freq, col[:, None] * inv_freq], axis=-1)
    cos, sin = jnp.cos(ang), jnp.sin(ang)
    cos2 = jnp.concatenate([cos, cos], axis=-1)
    sin2 = jnp.concatenate([-sin, sin], axis=-1)
    cos2 = jnp.concatenate([jnp.ones((lc, HEAD_DIM), F32), cos2], axis=0)
    sin2 = jnp.concatenate([jnp.zeros((lc, HEAD_DIM), F32), sin2], axis=0)
    return cos2, sin2


def kernel(x, c, ctx, c_ctx, ada_w, ada_b, norm_mix, norm_ffn, ab_w_in, ab_w_out, a_q_norm, a_k_norm, b_decay_exp, b_gn, c_w_in, c_w_out, c_lb, c_gn, router_w, router_b, exp_w1, exp_b1, exp_w2, exp_b2, norm_final):
    bsz, n_lat, d = x.shape
    lc = ctx.shape[1]
    depth = ada_w.shape[0]
    assert depth == 2 and n_lat % lc == 0 and bsz < MOD_ROWS and lc % GLA_CHUNK == 0
    seg = lc + n_lat
    nseg = seg // lc
    n_exp = router_w.shape[2]

    z = jnp.concatenate([ctx, x], axis=1).reshape(bsz * seg, d)

    cond = jnp.zeros((MOD_ROWS, d), F32).at[:bsz].set(c).at[bsz].set(c_ctx)
    mods = _ada(cond, ada_w, ada_b).reshape(depth, MOD_ROWS, N_MOD, 1, d)
    cos2, sin2 = _rope_tables(n_lat, lc)

    lb_soft = jax.nn.softmax(c_lb.astype(F32), axis=0)
    lower_bounds = jnp.cumsum(lb_soft, axis=0) - lb_soft[0]
    log_g = jnp.log1p(-jnp.exp2(-b_decay_exp[0].astype(F32)))

    wr = jnp.zeros((depth, d, LANES), BF16).at[:, :, :n_exp].set(router_w.astype(BF16))
    br = jnp.full((depth, 1, LANES), -1e30, F32).at[:, 0, :n_exp].set(router_b)

    hz = _norm_mod(z, norm_mix[0:1], mods[0], nseg, bsz, lc)
    proj = _matmul(hz, ab_w_in[0].astype(BF16), 1024, 512, name="in_proj0")
    att = _attention(proj, cos2, sin2, a_q_norm[0:1], a_k_norm[0:1], bsz, seg, lc)
    ret = _retention(proj, log_g, cos2, sin2, b_gn[0:1], bsz, seg, lc)
    w_out = ab_w_out[0].astype(BF16)
    aw = A_Q_HEADS * HEAD_DIM
    z1, h2, logits = _out_proj([att, ret], [w_out[:aw], w_out[aw:]], z, mods[0], norm_ffn[0:1],
                               wr[0], br[0], bsz, nseg, 0, lc)
    yt, gates = _moe(h2, logits, exp_w1, exp_b1, exp_w2, exp_b2, 0)
    z2, hz = _combine(z1, yt, gates, mods[0], mods[1], norm_mix[1:2], bsz, nseg, lc, final=False)

    proj = _matmul(hz, c_w_in[0].astype(BF16), 1024, 512, name="in_proj1")
    hg = _hgrn(proj, lower_bounds[1:2], c_gn[0:1], bsz, seg, lc)
    z1, h2, logits = _out_proj([hg], [c_w_out[0].astype(BF16)], z2, mods[1], norm_ffn[1:2],
                               wr[1], br[1], bsz, nseg, 1, lc)
    yt, gates = _moe(h2, logits, exp_w1, exp_b1, exp_w2, exp_b2, 1)
    (out,) = _combine(z1, yt, gates, mods[1], mods[1], norm_final.reshape(1, d), bsz, nseg - 1, lc, final=True)
    return out.reshape(bsz, n_lat, d)
```

```python
import functools

import jax
import jax.numpy as jnp
from jax import lax
from jax.experimental import pallas as pl
from jax.experimental.pallas import tpu as pltpu

HEAD_DIM = 128
GRID_W = 64
ROPE_THETA = 10000.0
NORM_EPS = 1e-6
N_MOD = 6
A_Q_HEADS = 8
A_KV_HEADS = 2
B_HEADS = 8
C_HEADS = 16
TOP_K = 4
SWIGLU_ALPHA = 1.702
SWIGLU_LIMIT = 7.0

LANES = 128
MOD_ROWS = 16
GLA_CHUNK = 64
MOE_BM = 512
GATHER_CHUNK = 512
VMEM_LIMIT = 56 * 1024 * 1024

F32 = jnp.float32
BF16 = jnp.bfloat16


def _cp(sem, vmem=VMEM_LIMIT):
    return pltpu.CompilerParams(dimension_semantics=sem, vmem_limit_bytes=vmem)


def _tile(n, pref):
    t = pref
    while n % t:
        t //= 2
    return t


def _dot(a, b):
    return jnp.dot(a, b, preferred_element_type=F32)


def _dot_nt(a, b):
    return lax.dot_general(a, b, (((1,), (1,)), ((), ())), preferred_element_type=F32)


def _dot_tn(a, b):
    return lax.dot_general(a, b, (((0,), (0,)), ((), ())), preferred_element_type=F32)


def _sigmoid(x):
    return 1.0 / (1.0 + jnp.exp(-x))


def _rope(t, cos2, sin2):
    return t * cos2 + pltpu.roll(t, HEAD_DIM // 2, 1) * sin2


def _rms_rows(x):
    return x * lax.rsqrt(jnp.mean(x * x, axis=-1, keepdims=True) + NORM_EPS)


def _ada_kernel(c_ref, w_ref, b_ref, o_ref):
    c = c_ref[...]
    s = (c * _sigmoid(c)).astype(BF16)
    o_ref[...] = _dot(s, w_ref[...].astype(BF16)) + b_ref[...]


def _ada(cond, ada_w, ada_b, tn=1024):
    depth, d, n = ada_w.shape
    return pl.pallas_call(
        _ada_kernel,
        grid=(depth, n // tn),
        in_specs=[
            pl.BlockSpec((MOD_ROWS, d), lambda l, j: (0, 0)),
            pl.BlockSpec((None, d, tn), lambda l, j: (l, 0, j)),
            pl.BlockSpec((None, 1, tn), lambda l, j: (l, 0, j)),
        ],
        out_specs=pl.BlockSpec((None, MOD_ROWS, tn), lambda l, j: (l, 0, j)),
        out_shape=jax.ShapeDtypeStruct((depth, MOD_ROWS, n), F32),
        compiler_params=_cp(("arbitrary", "arbitrary")),
        name="ada_mod",
    )(cond, ada_w, ada_b.reshape(depth, 1, n))


def _norm_mod_kernel(z_ref, g_ref, sh_ref, sc_ref, o_ref):
    y = _rms_rows(z_ref[...]) * g_ref[...]
    o_ref[...] = (y * (1.0 + sc_ref[...]) + sh_ref[...]).astype(o_ref.dtype)


def _norm_mod(z, gain, mods, nseg, bsz, rb):
    t, d = z.shape

    def midx(i):
        return jnp.where(i % nseg == 0, bsz, i // nseg)

    return pl.pallas_call(
        _norm_mod_kernel,
        grid=(t // rb,),
        in_specs=[
            pl.BlockSpec((rb, d), lambda i: (i, 0)),
            pl.BlockSpec((1, d), lambda i: (0, 0)),
            pl.BlockSpec((None, None, 1, d), lambda i: (midx(i), 0, 0, 0)),
            pl.BlockSpec((None, None, 1, d), lambda i: (midx(i), 1, 0, 0)),
        ],
        out_specs=pl.BlockSpec((rb, d), lambda i: (i, 0)),
        out_shape=jax.ShapeDtypeStruct((t, d), BF16),
        compiler_params=_cp(("arbitrary",)),
        name="norm_mod",
    )(z, gain, mods, mods)


def _mm_kernel(x_ref, w_ref, o_ref):
    o_ref[...] = _dot(x_ref[...], w_ref[...]).astype(o_ref.dtype)


def _matmul(x, w, tm, tn, out_dtype=F32, name="matmul"):
    m, k = x.shape
    n = w.shape[1]
    tm, tn = _tile(m, tm), _tile(n, tn)
    return pl.pallas_call(
        _mm_kernel,
        grid=(m // tm, n // tn),
        in_specs=[
            pl.BlockSpec((tm, k), lambda i, j: (i, 0)),
            pl.BlockSpec((k, tn), lambda i, j: (0, j)),
        ],
        out_specs=pl.BlockSpec((tm, tn), lambda i, j: (i, j)),
        out_shape=jax.ShapeDtypeStruct((m, n), out_dtype),
        compiler_params=_cp(("arbitrary", "arbitrary")),
        name=name,
    )(x, w)


def _attn_kernel(q_ref, k_ref, v_ref, cosq_ref, sinq_ref, cosk_ref, sink_ref,
                 qg_ref, kg_ref, o_ref, ks, vs, *, lc, group):
    qi = pl.program_id(2)
    rb = q_ref.shape[0]
    scale = HEAD_DIM ** -0.5

    @pl.when(qi == 0)
    def _prep():
        kn = _rms_rows(k_ref[...]) * kg_ref[...]
        ks[...] = _rope(kn, cosk_ref[...], sink_ref[...]).astype(BF16)
        vs[...] = v_ref[...].astype(BF16)

    def q_heads():
        q = q_ref[...]
        hs = []
        for r in range(group):
            qh = _rms_rows(q[:, r * HEAD_DIM:(r + 1) * HEAD_DIM]) * qg_ref[...]
            hs.append(_rope(qh, cosq_ref[...], sinq_ref[...]).astype(BF16))
        return jnp.concatenate(hs, axis=0)

    def attend(k, v):
        s = _dot_nt(q_heads(), k)
        m = jnp.max(s, axis=-1, keepdims=True)
        p = jnp.exp((s - m) * scale)
        l = jnp.sum(p, axis=-1, keepdims=True)
        o = _dot(p.astype(BF16), v) / l
        for r in range(group):
            o_ref[:, r * HEAD_DIM:(r + 1) * HEAD_DIM] = o[r * rb:(r + 1) * rb].astype(o_ref.dtype)

    @pl.when(qi == 0)
    def _ctx():
        attend(ks[0:lc], vs[0:lc])

    @pl.when(qi > 0)
    def _lat():
        attend(ks[...], vs[...])


def _attention(proj, cos2, sin2, q_gain, k_gain, bsz, seg, lc):
    t = proj.shape[0]
    nseg = seg // lc
    group = A_Q_HEADS // A_KV_HEADS
    kv_w = A_KV_HEADS
    q_col0 = (2 * kv_w + 2 * B_HEADS) // group
    kern = functools.partial(_attn_kernel, lc=lc, group=group)
    return pl.pallas_call(
        kern,
        grid=(bsz, A_KV_HEADS, nseg),
        in_specs=[
            pl.BlockSpec((lc, group * HEAD_DIM), lambda b, g, i: (b * nseg + i, q_col0 + g)),
            pl.BlockSpec((seg, HEAD_DIM), lambda b, g, i: (b, g)),
            pl.BlockSpec((seg, HEAD_DIM), lambda b, g, i: (b, kv_w + g)),
            pl.BlockSpec((lc, HEAD_DIM), lambda b, g, i: (i, 0)),
            pl.BlockSpec((lc, HEAD_DIM), lambda b, g, i: (i, 0)),
            pl.BlockSpec((seg, HEAD_DIM), lambda b, g, i: (0, 0)),
            pl.BlockSpec((seg, HEAD_DIM), lambda b, g, i: (0, 0)),
            pl.BlockSpec((1, HEAD_DIM), lambda b, g, i: (0, 0)),
            pl.BlockSpec((1, HEAD_DIM), lambda b, g, i: (0, 0)),
        ],
        out_specs=pl.BlockSpec((lc, group * HEAD_DIM), lambda b, g, i: (b * nseg + i, g)),
        out_shape=jax.ShapeDtypeStruct((t, A_Q_HEADS * HEAD_DIM), BF16),
        scratch_shapes=[pltpu.VMEM((seg, HEAD_DIM), BF16), pltpu.VMEM((seg, HEAD_DIM), BF16)],
        compiler_params=_cp(("arbitrary", "arbitrary", "arbitrary")),
        name="gqa_attention",
    )(proj, proj, proj, cos2, sin2, cos2, sin2, q_gain, k_gain)


def _ret_kernel(lg_ref, q_ref, k_ref, v_ref, g_ref, cos_ref, sin_ref, gn_ref, o_ref,
                oacc, qf_s, qb_s, uf_s, ub_s, sf_s, sb_s, *, c, nchunk, nctx):
    h = pl.program_id(1)
    lgf = lg_ref[0, h]
    lgb = lg_ref[1, h]
    cf = float(c)
    n_i = lax.broadcasted_iota(jnp.int32, (c, 1), 0).astype(F32)
    m_i = lax.broadcasted_iota(jnp.int32, (1, c), 1).astype(F32)
    diff = n_i - m_i
    dmat = (jnp.where(diff >= 0, jnp.exp(lgf * jnp.maximum(diff, 0.0)), 0.0)
            + jnp.where(diff <= 0, jnp.exp(lgb * jnp.maximum(-diff, 0.0)), 0.0))
    qdf = jnp.exp(lgf * (n_i + 1.0))
    qdb = jnp.exp(lgb * (cf - n_i))
    kdf = jnp.exp(lgf * (cf - 1.0 - n_i))
    kdb = jnp.exp(lgb * n_i)
    cdf = jnp.exp(lgf * cf)
    cdb = jnp.exp(lgb * cf)
    kscale = HEAD_DIM ** -0.5

    def phase_a(ci, carry):
        r0 = pl.multiple_of(ci * c, c)
        rows = pl.ds(r0, c)
        cos2 = cos_ref[rows, :]
        sin2 = sin_ref[rows, :]
        q = _rope(q_ref[rows, :], cos2, sin2)
        k = _rope(k_ref[rows, :], cos2, sin2) * kscale
        vb = v_ref[rows, :].astype(BF16)
        s = _dot_nt(q.astype(BF16), k.astype(BF16)) * dmat
        oacc[rows, :] = _dot(s.astype(BF16), vb)
        qf_s[rows, :] = (q * qdf).astype(BF16)
        qb_s[rows, :] = (q * qdb).astype(BF16)
        uf_s[ci] = _dot_tn((k * kdf).astype(BF16), vb)
        ub_s[ci] = _dot_tn((k * kdb).astype(BF16), vb)
        return carry

    lax.fori_loop(0, nchunk, phase_a, 0)

    def scan_f(ci, s):
        sf_s[ci] = s.astype(BF16)
        return s * cdf + uf_s[ci]

    lax.fori_loop(0, nchunk, scan_f, jnp.zeros((HEAD_DIM, HEAD_DIM), F32))

    def scan_b(i, s):
        ci = jnp.where(i < nctx, nctx - 1 - i, nchunk - 1 - (i - nctx))
        sb_s[ci] = s.astype(BF16)
        return s * cdb + ub_s[ci]

    lax.fori_loop(0, nchunk, scan_b, jnp.zeros((HEAD_DIM, HEAD_DIM), F32))

    def phase_c(ci, carry):
        r0 = pl.multiple_of(ci * c, c)
        rows = pl.ds(r0, c)
        o = oacc[rows, :] + _dot(qf_s[rows, :], sf_s[ci]) + _dot(qb_s[rows, :], sb_s[ci])
        mu = jnp.mean(o, axis=-1, keepdims=True)
        d = o - mu
        on = d * lax.rsqrt(jnp.mean(d * d, axis=-1, keepdims=True) + NORM_EPS)
        g = g_ref[rows, :]
        o_ref[rows, :] = (on * gn_ref[...] * (g * _sigmoid(g))).astype(o_ref.dtype)
        return carry

    lax.fori_loop(0, nchunk, phase_c, 0)


def _retention(proj, log_g, cos2, sin2, gn_gain, bsz, seg, lc):
    t = proj.shape[0]
    nchunk = seg // lc
    kvw = 2 * A_KV_HEADS
    k0, v0 = kvw, kvw + B_HEADS
    q0 = kvw + 2 * B_HEADS + A_Q_HEADS
    g0 = q0 + B_HEADS
    kern = functools.partial(_ret_kernel, c=lc, nchunk=nchunk, nctx=1)
    col = lambda c0: pl.BlockSpec((seg, HEAD_DIM), lambda b, h: (b, c0 + h))
    return pl.pallas_call(
        kern,
        grid=(bsz, B_HEADS),
        in_specs=[
            pl.BlockSpec(memory_space=pltpu.SMEM),
            col(q0), col(k0), col(v0), col(g0),
            pl.BlockSpec((seg, HEAD_DIM), lambda b, h: (0, 0)),
            pl.BlockSpec((seg, HEAD_DIM), lambda b, h: (0, 0)),
            pl.BlockSpec((1, HEAD_DIM), lambda b, h: (0, h)),
        ],
        out_specs=pl.BlockSpec((seg, HEAD_DIM), lambda b, h: (b, h)),
        out_shape=jax.ShapeDtypeStruct((t, B_HEADS * HEAD_DIM), BF16),
        scratch_shapes=[
            pltpu.VMEM((seg, HEAD_DIM), F32),
            pltpu.VMEM((seg, HEAD_DIM), BF16),
            pltpu.VMEM((seg, HEAD_DIM), BF16),
            pltpu.VMEM((nchunk, HEAD_DIM, HEAD_DIM), F32),
            pltpu.VMEM((nchunk, HEAD_DIM, HEAD_DIM), F32),
            pltpu.VMEM((nchunk, HEAD_DIM, HEAD_DIM), BF16),
            pltpu.VMEM((nchunk, HEAD_DIM, HEAD_DIM), BF16),
        ],
        compiler_params=_cp(("arbitrary", "arbitrary")),
        name="retention",
    )(log_g, proj, proj, proj, proj, cos2, sin2, gn_gain)


def _split3(x):
    hi = x.astype(BF16)
    r1 = x - hi.astype(F32)
    mid = r1.astype(BF16)
    lo = (r1 - mid.astype(F32)).astype(BF16)
    return hi, mid, lo


def _hgrn_kernel(ff_ref, fb_ref, v_ref, q_ref, g_ref, lb_ref, gn_ref, o_ref,
                 oacc, qf_s, qb_s, uf_s, ub_s, df_s, db_s, sf_s, sb_s, *, c, nchunk, nctx):
    g = 4 if nchunk % 4 == 0 else (2 if nchunk % 2 == 0 else 1)
    tc = g * c
    lb = lb_ref[...]
    one_m_lb = 1.0 - lb
    n_i = lax.broadcasted_iota(jnp.int32, (tc, tc), 0)
    m_i = lax.broadcasted_iota(jnp.int32, (tc, tc), 1)
    shift = c.bit_length() - 1
    same = jnp.right_shift(n_i, shift) == jnp.right_shift(m_i, shift)
    lower = same & (n_i >= m_i)
    upper = same & (m_i >= n_i)
    tri_l = lower.astype(BF16)
    tri_u = upper.astype(BF16)
    mid = c // 2

    def csum(tri, x):
        hi, md, lo = _split3(x)
        return _dot(tri, hi) + _dot(tri, md) + _dot(tri, lo)

    def chunk_row(x, row):
        return jnp.concatenate(
            [jnp.broadcast_to(x[j * c + row:j * c + row + 1, :], (c, HEAD_DIM)) for j in range(g)], axis=0)

    def phase_a(ti, carry):
        r0 = pl.multiple_of(ti * tc, tc)
        rows = pl.ds(r0, tc)
        qr = q_ref[rows, :]
        q = qr * _sigmoid(qr)
        vb = v_ref[rows, :].astype(BF16)
        frf = ff_ref[rows, :]
        frb = fb_ref[rows, :]
        kf = one_m_lb * _sigmoid(-frf)
        kb = one_m_lb * _sigmoid(-frb)
        lff = jnp.log(lb + one_m_lb * _sigmoid(frf))
        lfb = jnp.log(lb + one_m_lb * _sigmoid(frb))
        cum_f = csum(tri_l, lff)
        cum_b = csum(tri_u, lfb)
        an_f = chunk_row(cum_f, mid)
        an_b = chunk_row(cum_b, mid)
        a_f = _dot_nt((q * jnp.exp(cum_f - an_f)).astype(BF16), (kf * jnp.exp(an_f - cum_f)).astype(BF16))
        a_b = _dot_nt((q * jnp.exp(cum_b - an_b)).astype(BF16), (kb * jnp.exp(an_b - cum_b)).astype(BF16))
        a = jnp.where(lower, a_f, 0.0) + jnp.where(upper, a_b, 0.0)
        oacc[rows, :] = _dot(a.astype(BF16), vb)
        last_f = chunk_row(cum_f, c - 1)
        last_b = chunk_row(cum_b, 0)
        qf_s[rows, :] = (q * jnp.exp(cum_f)).astype(BF16)
        qb_s[rows, :] = (q * jnp.exp(cum_b)).astype(BF16)
        khf = (kf * jnp.exp(last_f - cum_f)).astype(BF16)
        khb = (kb * jnp.exp(last_b - cum_b)).astype(BF16)
        for j in range(g):
            sl = slice(j * c, (j + 1) * c)
            ci = ti * g + j
            uf_s[ci] = _dot_tn(vb[sl], khf[sl])
            ub_s[ci] = _dot_tn(vb[sl], khb[sl])
            df_s[ci] = jnp.exp(last_f[j * c:j * c + 1, :])
            db_s[ci] = jnp.exp(last_b[j * c:j * c + 1, :])
        return carry

    lax.fori_loop(0, nchunk // g, phase_a, 0)
    unroll = g

    def scan_f(ci, s):
        sf_s[ci] = s.astype(BF16)
        return s * df_s[ci] + uf_s[ci]

    lax.fori_loop(0, nchunk, scan_f, jnp.zeros((HEAD_DIM, HEAD_DIM), F32))

    def scan_b(i, s):
        ci = jnp.where(i < nctx, nctx - 1 - i, nchunk - 1 - (i - nctx))
        sb_s[ci] = s.astype(BF16)
        return s * db_s[ci] + ub_s[ci]

    lax.fori_loop(0, nchunk, scan_b, jnp.zeros((HEAD_DIM, HEAD_DIM), F32))

    def phase_c(ci, carry):
        r0 = pl.multiple_of(ci * c, c)
        rows = pl.ds(r0, c)
        o = oacc[rows, :] + _dot_nt(qf_s[rows, :], sf_s[ci]) + _dot_nt(qb_s[rows, :], sb_s[ci])
        on = _rms_rows(o)
        g = g_ref[rows, :]
        o_ref[rows, :] = (on * gn_ref[...] * (g * _sigmoid(g))).astype(o_ref.dtype)
        return carry

    lax.fori_loop(0, nchunk, phase_c, 0, unroll=unroll)


def _hgrn(proj, lb, gn_gain, bsz, seg, lc):
    t = proj.shape[0]
    c = GLA_CHUNK
    nchunk = seg // c
    nctx = lc // c
    kern = functools.partial(_hgrn_kernel, c=c, nchunk=nchunk, nctx=nctx)
    col = lambda c0: pl.BlockSpec((seg, HEAD_DIM), lambda b, h: (b, c0 + h))
    vec = pl.BlockSpec((1, HEAD_DIM), lambda b, h: (0, h))
    return pl.pallas_call(
        kern,
        grid=(bsz, C_HEADS),
        in_specs=[col(0), col(C_HEADS), col(2 * C_HEADS), col(3 * C_HEADS), col(4 * C_HEADS), vec, vec],
        out_specs=pl.BlockSpec((seg, HEAD_DIM), lambda b, h: (b, h)),
        out_shape=jax.ShapeDtypeStruct((t, C_HEADS * HEAD_DIM), BF16),
        scratch_shapes=[
            pltpu.VMEM((seg, HEAD_DIM), F32),
            pltpu.VMEM((seg, HEAD_DIM), BF16),
            pltpu.VMEM((seg, HEAD_DIM), BF16),
            pltpu.VMEM((nchunk, HEAD_DIM, HEAD_DIM), F32),
            pltpu.VMEM((nchunk, HEAD_DIM, HEAD_DIM), F32),
            pltpu.VMEM((nchunk, 1, HEAD_DIM), F32),
            pltpu.VMEM((nchunk, 1, HEAD_DIM), F32),
            pltpu.VMEM((nchunk, HEAD_DIM, HEAD_DIM), BF16),
            pltpu.VMEM((nchunk, HEAD_DIM, HEAD_DIM), BF16),
        ],
        compiler_params=_cp(("arbitrary", "arbitrary")),
        name="hgrn2",
    )(proj, proj, proj, proj, proj, lb, gn_gain)


def _out_kernel(*refs, n_in):
    xs = refs[:n_in]
    ws = refs[n_in:2 * n_in]
    z_ref, gate_ref, gain_ref, sh_ref, sc_ref, wr_ref, br_ref = refs[2 * n_in:2 * n_in + 7]
    z_out, h_out, lg_out = refs[2 * n_in + 7:]
    o = _dot(xs[0][...], ws[0][...])
    for x_ref, w_ref in zip(xs[1:], ws[1:]):
        o = o + _dot(x_ref[...], w_ref[...])
    z1 = z_ref[...] + gate_ref[...] * o
    z_out[...] = z1
    h = _rms_rows(z1) * gain_ref[...]
    h = h * (1.0 + sc_ref[...]) + sh_ref[...]
    hb = h.astype(BF16)
    h_out[...] = hb
    lg_out[...] = _dot(hb, wr_ref[...]) + br_ref[...]


def _out_proj(xs, ws, z, mods, gain, wr, br, bsz, nseg, off, rb):
    d = z.shape[1]
    nout = nseg - off
    n_in = len(xs)
    npad = wr.shape[1]

    def rin(b, j):
        return (b * nseg + off + j, 0)

    def rout(b, j):
        return (b * nout + j, 0)

    def mod(which):
        return pl.BlockSpec((None, None, 1, d),
                            lambda b, j: (jnp.where(j + off == 0, bsz, b), which, 0, 0))

    in_specs = [pl.BlockSpec((rb, x.shape[1]), rin) for x in xs]
    in_specs += [pl.BlockSpec(w.shape, lambda b, j: (0, 0)) for w in ws]
    in_specs += [
        pl.BlockSpec((rb, d), rin),
        mod(2),
        pl.BlockSpec((1, d), lambda b, j: (0, 0)),
        mod(3), mod(4),
        pl.BlockSpec(wr.shape, lambda b, j: (0, 0)),
        pl.BlockSpec((1, npad), lambda b, j: (0, 0)),
    ]
    tm = bsz * nout * rb
    return pl.pallas_call(
        functools.partial(_out_kernel, n_in=n_in),
        grid=(bsz, nout),
        in_specs=in_specs,
        out_specs=[pl.BlockSpec((rb, d), rout), pl.BlockSpec((rb, d), rout), pl.BlockSpec((rb, npad), rout)],
        out_shape=[jax.ShapeDtypeStruct((tm, d), F32), jax.ShapeDtypeStruct((tm, d), BF16),
                   jax.ShapeDtypeStruct((tm, npad), F32)],
        compiler_params=_cp(("arbitrary", "arbitrary")),
        name="out_proj",
    )(*xs, *ws, z, mods, gain, mods, mods, wr, br)


def _topk_kernel(lg_ref, idx_ref, gate_ref):
    l = lg_ref[...]
    lane = lax.broadcasted_iota(jnp.int32, l.shape, 1)
    vals, idxs = [], []
    for _ in range(TOP_K):
        m = jnp.max(l, axis=-1, keepdims=True)
        i = jnp.min(jnp.where(l == m, lane, LANES), axis=-1, keepdims=True)
        vals.append(m)
        idxs.append(i)
        l = jnp.where(lane == i, -jnp.inf, l)
    es = [jnp.exp(v - vals[0]) for v in vals]
    den = es[0] + es[1] + es[2] + es[3]
    io = jnp.zeros(l.shape, jnp.int32)
    go = jnp.zeros(l.shape, F32)
    for k in range(TOP_K):
        io = jnp.where(lane == k, idxs[k], io)
        go = jnp.where(lane == k, es[k] / den, go)
    idx_ref[...] = io
    gate_ref[...] = go


def _topk(logits, tm=512):
    t, n = logits.shape
    tm = _tile(t, tm)
    return pl.pallas_call(
        _topk_kernel,
        grid=(t // tm,),
        in_specs=[pl.BlockSpec((tm, n), lambda i: (i, 0))],
        out_specs=[pl.BlockSpec((tm, n), lambda i: (i, 0)), pl.BlockSpec((tm, n), lambda i: (i, 0))],
        out_shape=[jax.ShapeDtypeStruct((t, n), jnp.int32), jax.ShapeDtypeStruct((t, n), F32)],
        compiler_params=_cp(("arbitrary",)),
        name="router_topk",
    )(logits)


def _gather_kernel(idx0_ref, idxn_ref, src_ref, out_ref, buf, gsem, osem, *, chunk):
    i = pl.program_id(0)
    n = pl.num_programs(0)
    slot = i % 2
    other = 1 - slot

    def out_copy(step, s):
        return pltpu.make_async_copy(buf.at[s], out_ref.at[pl.ds(step * chunk, chunk)], osem.at[s])

    def issue_rows(idx_ref, s):
        def body(p, carry):
            r = 2 * p
            pltpu.make_async_copy(src_ref.at[idx_ref[0, r]], buf.at[s, r], gsem.at[s]).start(priority=0)
            pltpu.make_async_copy(src_ref.at[idx_ref[0, r + 1]], buf.at[s, r + 1], gsem.at[s]).start(priority=1)
            return carry

        lax.fori_loop(0, chunk // 2, body, 0)

    @pl.when(i == 0)
    def _first():
        issue_rows(idx0_ref, 0)

    @pl.when(i + 1 < n)
    def _next():
        @pl.when(i >= 1)
        def _free_slot():
            out_copy(i - 1, other).wait()

        issue_rows(idxn_ref, other)

    pltpu.make_async_copy(src_ref.at[pl.ds(0, chunk)], buf.at[slot], gsem.at[slot]).wait()
    out_copy(i, slot).start()

    @pl.when(i == n - 1)
    def _flush():
        out_copy(i, slot).wait()

        @pl.when(i >= 1)
        def _prev():
            out_copy(i - 1, other).wait()


def _gather_rows(src, idx, chunk=GATHER_CHUNK):
    r = idx.shape[0]
    d = src.shape[1:]
    nchunks = r // chunk
    assert src.shape[0] >= chunk
    idx3 = idx.reshape(nchunks, 1, chunk)
    return pl.pallas_call(
        functools.partial(_gather_kernel, chunk=chunk),
        grid=(nchunks,),
        in_specs=[
            pl.BlockSpec((None, 1, chunk), lambda i: (0, 0, 0), memory_space=pltpu.SMEM),
            pl.BlockSpec((None, 1, chunk), lambda i: (jnp.minimum(i + 1, nchunks - 1), 0, 0),
                         memory_space=pltpu.SMEM),
            pl.BlockSpec(memory_space=pl.ANY),
        ],
        out_specs=pl.BlockSpec(memory_space=pl.ANY),
        out_shape=jax.ShapeDtypeStruct((r,) + d, src.dtype),
        scratch_shapes=[pltpu.VMEM((2, chunk) + d, src.dtype), pltpu.SemaphoreType.DMA((2,)),
                        pltpu.SemaphoreType.DMA((2,))],
        compiler_params=_cp(("arbitrary",)),
        name="row_gather",
    )(idx3, idx3, src)


def _gmm1_kernel(be_ref, first_ref, nused_ref, x_ref, w_ref, b_ref, o_ref, wsc):
    i = pl.program_id(1)
    tn = w_ref.shape[1]

    @pl.when(i < nused_ref[0])
    def _():
        @pl.when(first_ref[i] == 1)
        def _cast():
            wsc[...] = w_ref[...].astype(BF16)

        w2 = 2 * LANES
        rsel = lax.broadcasted_iota(jnp.int32, (w2, LANES), 0)
        csel = lax.broadcasted_iota(jnp.int32, (w2, LANES), 1)
        sel = (rsel == 2 * csel).astype(BF16)
        hid = _dot(x_ref[...], wsc[...]) + b_ref[...]
        glu = jnp.minimum(hid, SWIGLU_LIMIT)
        glu = glu * _sigmoid(SWIGLU_ALPHA * glu)
        lin = jnp.clip(hid, -SWIGLU_LIMIT, SWIGLU_LIMIT) + 1.0
        for s in range(tn // w2):
            parts = []
            for u in range(2):
                lo = s * w2 + u * LANES
                lin_sh = pltpu.roll(lin[:, lo:lo + LANES], LANES - 1, 1)
                parts.append((glu[:, lo:lo + LANES] * lin_sh).astype(BF16))
            prod = jnp.concatenate(parts, axis=1)
            o_ref[:, s * LANES:(s + 1) * LANES] = _dot(prod, sel).astype(o_ref.dtype)

    @pl.when(i >= nused_ref[0])
    def _unused():
        o_ref[...] = jnp.zeros(o_ref.shape, o_ref.dtype)


def _gmm2_kernel(be_ref, first_ref, nused_ref, x_ref, w_ref, b_ref, o_ref, wsc):
    i = pl.program_id(1)

    @pl.when(i < nused_ref[0])
    def _():
        @pl.when(first_ref[i] == 1)
        def _cast():
            wsc[...] = w_ref[...].astype(BF16)

        o_ref[...] = (_dot(x_ref[...], wsc[...]) + b_ref[...]).astype(o_ref.dtype)

    @pl.when(i >= nused_ref[0])
    def _unused():
        o_ref[...] = jnp.zeros(o_ref.shape, o_ref.dtype)


def _gmm(kernel, x, w, b, layer, be, first, nused, bm, tn, out_cols, out_dtype, name):
    r, k = x.shape
    n = w.shape[3]
    tn = _tile(n, tn)
    nb = r // bm
    n_tiles = n // tn
    oc = out_cols // n_tiles

    def blk(i, nu):
        return jnp.minimum(i, nu[0] - 1)

    grid_spec = pltpu.PrefetchScalarGridSpec(
        num_scalar_prefetch=3,
        grid=(n_tiles, nb),
        in_specs=[
            pl.BlockSpec((bm, k), lambda j, i, be, fi, nu: (blk(i, nu), 0)),
            pl.BlockSpec((None, None, k, tn), lambda j, i, be, fi, nu: (layer, be[blk(i, nu)], 0, j)),
            pl.BlockSpec((None, None, 1, tn), lambda j, i, be, fi, nu: (layer, be[blk(i, nu)], 0, j)),
        ],
        out_specs=pl.BlockSpec((bm, oc), lambda j, i, be, fi, nu: (i, j)),
        scratch_shapes=[pltpu.VMEM((k, tn), BF16)],
    )
    return pl.pallas_call(
        kernel,
        grid_spec=grid_spec,
        out_shape=jax.ShapeDtypeStruct((r, out_cols), out_dtype),
        compiler_params=_cp(("arbitrary", "arbitrary")),
        name=name,
    )(be, first, nused, x, w, b.reshape(b.shape[0], b.shape[1], 1, n))


def _combine_kernel(z_ref, y0_ref, y1_ref, y2_ref, y3_ref, gt_ref, gate_ref, gain_ref, sh_ref, sc_ref,
                    *outs, final):
    gt = gt_ref[...]
    ffn = gt[:, 0:1] * y0_ref[...].astype(F32)
    for k, y_ref in enumerate((y1_ref, y2_ref, y3_ref), start=1):
        ffn = ffn + gt[:, k:k + 1] * y_ref[...].astype(F32)
    z2 = z_ref[...] + gate_ref[...] * ffn
    h = _rms_rows(z2) * gain_ref[...]
    if final:
        outs[0][...] = h
    else:
        outs[0][...] = z2
        outs[1][...] = (h * (1.0 + sc_ref[...]) + sh_ref[...]).astype(outs[1].dtype)


def _combine(z1, yt, gates, mods_cur, mods_next, gain, bsz, nblk, rb, final):
    t, d = z1.shape
    has_ctx = not final

    def midx(i):
        if has_ctx:
            return jnp.where(i % nblk == 0, bsz, i // nblk)
        return i // nblk

    row = lambda i: (i, 0)
    nrow = t // rb
    out_specs = [pl.BlockSpec((rb, d), row)]
    out_shape = [jax.ShapeDtypeStruct((t, d), F32)]
    if not final:
        out_specs.append(pl.BlockSpec((rb, d), row))
        out_shape.append(jax.ShapeDtypeStruct((t, d), BF16))
    y_specs = [pl.BlockSpec((rb, d), functools.partial(lambda i, k: (k * nrow + i, 0), k=k))
               for k in range(TOP_K)]
    return pl.pallas_call(
        functools.partial(_combine_kernel, final=final),
        grid=(nrow,),
        in_specs=[
            pl.BlockSpec((rb, d), row),
            *y_specs,
            pl.BlockSpec((rb, gates.shape[1]), row),
            pl.BlockSpec((None, None, 1, d), lambda i: (midx(i), 5, 0, 0)),
            pl.BlockSpec((1, d), lambda i: (0, 0)),
            pl.BlockSpec((None, None, 1, d), lambda i: (midx(i), 0, 0, 0)),
            pl.BlockSpec((None, None, 1, d), lambda i: (midx(i), 1, 0, 0)),
        ],
        out_specs=out_specs,
        out_shape=out_shape,
        compiler_params=_cp(("arbitrary",)),
        name="moe_combine",
    )(z1, yt, yt, yt, yt, gates, mods_cur, gain, mods_next, mods_next)


def _routing(idx, n_exp, bm, gchunk):
    tm = idx.shape[0]
    onehot = (idx[:, :, None] == jnp.arange(n_exp, dtype=jnp.int32)).astype(jnp.int32).sum(axis=1)
    csum = jnp.cumsum(onehot, axis=0)
    counts = csum[-1]
    rank = jnp.take_along_axis(csum - onehot, idx, axis=1)
    padded = (counts + bm - 1) // bm * bm
    pad_end = jnp.cumsum(padded)
    pad_start = pad_end - padded
    dest = pad_start[idx] + rank
    nb = (tm * TOP_K + n_exp * (bm - 1)) // bm
    nb = -(-(nb * bm) // gchunk) * gchunk // bm
    tok = jnp.repeat(jnp.arange(tm, dtype=jnp.int32), TOP_K)
    row_tok = jnp.zeros((nb * bm,), jnp.int32).at[dest.reshape(-1)].set(tok)
    nused = (pad_end[-1] // bm).astype(jnp.int32).reshape(1)
    block_start = jnp.arange(nb, dtype=jnp.int32) * bm
    be = jnp.minimum((pad_end[None, :] <= block_start[:, None]).astype(jnp.int32).sum(axis=1), n_exp - 1)
    first = jnp.concatenate([jnp.ones((1,), jnp.int32), (be[1:] != be[:-1]).astype(jnp.int32)])
    dest_kmajor = dest.T.reshape(-1).astype(jnp.int32)
    return dest_kmajor, row_tok, be, first, nused


def _moe(h2, logits, w1, b1, w2, b2, layer):
    tm, d = h2.shape
    n_exp = w1.shape[1]
    sub = d // LANES
    idx_p, gate_p = _topk(logits)
    idx = idx_p[:, :TOP_K]
    dest, row_tok, be, first, nused = _routing(idx, n_exp, MOE_BM, GATHER_CHUNK)
    xs = _gather_rows(h2.reshape(tm, sub, LANES), row_tok).reshape(-1, d)
    act = _gmm(_gmm1_kernel, xs, w1, b1, layer, be, first, nused, MOE_BM, 1024, w1.shape[3] // 2, BF16, "moe_up")
    y = _gmm(_gmm2_kernel, act, w2, b2, layer, be, first, nused, MOE_BM, 1024, w2.shape[3], BF16, "moe_down")
    yt = _gather_rows(y.reshape(-1, sub, LANES), dest).reshape(TOP_K * tm, d)
    return yt, gate_p


def _rope_tables(n_lat, lc):
    n_rows = n_lat // GRID_W
    row = jnp.repeat(jnp.arange(n_rows, dtype=F32), GRID_W)
    col = jnp.tile(jnp.arange(GRID_W, dtype=F32), n_rows)
    n_freq = HEAD_DIM // 4
    inv_freq = ROPE_THETA ** (-jnp.arange(n_freq, dtype=F32) / n_freq)
    ang = jnp.concatenate([row[:, None] * inv_freq, col[:, None] * inv_freq], axis=-1)
    cos, sin = jnp.cos(ang), jnp.sin(ang)
    cos2 = jnp.concatenate([cos, cos], axis=-1)
    sin2 = jnp.concatenate([-sin, sin], axis=-1)
    cos2 = jnp.concatenate([jnp.ones((lc, HEAD_DIM), F32), cos2], axis=0)
    sin2 = jnp.concatenate([jnp.zeros((lc, HEAD_DIM), F32), sin2], axis=0)
    return cos2, sin2


def kernel(x, c, ctx, c_ctx, ada_w, ada_b, norm_mix, norm_ffn, ab_w_in, ab_w_out, a_q_norm, a_k_norm, b_decay_exp, b_gn, c_w_in, c_w_out, c_lb, c_gn, router_w, router_b, exp_w1, exp_b1, exp_w2, exp_b2, norm_final):
    bsz, n_lat, d = x.shape
    lc = ctx.shape[1]
    depth = ada_w.shape[0]
    assert depth == 2 and n_lat % lc == 0 and bsz < MOD_ROWS and lc % GLA_CHUNK == 0
    seg = lc + n_lat
    nseg = seg // lc
    n_exp = router_w.shape[2]

    z = jnp.concatenate([ctx, x], axis=1).reshape(bsz * seg, d)

    cond = jnp.zeros((MOD_ROWS, d), F32).at[:bsz].set(c).at[bsz].set(c_ctx)
    mods = _ada(cond, ada_w, ada_b).reshape(depth, MOD_ROWS, N_MOD, 1, d)
    cos2, sin2 = _rope_tables(n_lat, lc)

    lb_soft = jax.nn.softmax(c_lb.astype(F32), axis=0)
    lower_bounds = jnp.cumsum(lb_soft, axis=0) - lb_soft[0]
    log_g = jnp.log1p(-jnp.exp2(-b_decay_exp[0].astype(F32)))

    wr = jnp.zeros((depth, d, LANES), BF16).at[:, :, :n_exp].set(router_w.astype(BF16))
    br = jnp.full((depth, 1, LANES), -1e30, F32).at[:, 0, :n_exp].set(router_b)

    hz = _norm_mod(z, norm_mix[0:1], mods[0], nseg, bsz, lc)
    proj = _matmul(hz, ab_w_in[0].astype(BF16), 1024, 512, name="in_proj0")
    att = _attention(proj, cos2, sin2, a_q_norm[0:1], a_k_norm[0:1], bsz, seg, lc)
    ret = _retention(proj, log_g, cos2, sin2, b_gn[0:1], bsz, seg, lc)
    w_out = ab_w_out[0].astype(BF16)
    aw = A_Q_HEADS * HEAD_DIM
    z1, h2, logits = _out_proj([att, ret], [w_out[:aw], w_out[aw:]], z, mods[0], norm_ffn[0:1],
                               wr[0], br[0], bsz, nseg, 0, lc)
    yt, gates = _moe(h2, logits, exp_w1, exp_b1, exp_w2, exp_b2, 0)
    z2, hz = _combine(z1, yt, gates, mods[0], mods[1], norm_mix[1:2], bsz, nseg, lc, final=False)

    proj = _matmul(hz, c_w_in[0].astype(BF16), 1024, 512, name="in_proj1")
    hg = _hgrn(proj, lower_bounds[1:2], c_gn[0:1], bsz, seg, lc)
    z1, h2, logits = _out_proj([hg], [c_w_out[0].astype(BF16)], z2, mods[1], norm_ffn[1:2],
                               wr[1], br[1], bsz, nseg, 1, lc)
    yt, gates = _moe(h2, logits, exp_w1, exp_b1, exp_w2, exp_b2, 1)
    (out,) = _combine(z1, yt, gates, mods[1], mods[1], norm_final.reshape(1, d), bsz, nseg - 1, lc, final=True)
    return out.reshape(bsz, n_lat, d)
```

```python
import functools

import jax
import jax.numpy as jnp
from jax import lax
from jax.experimental import pallas as pl
from jax.experimental.pallas import tpu as pltpu

HEAD_DIM = 128
GRID_W = 64
ROPE_THETA = 10000.0
NORM_EPS = 1e-6
N_MOD = 6
A_Q_HEADS = 8
A_KV_HEADS = 2
B_HEADS = 8
C_HEADS = 16
TOP_K = 4
SWIGLU_ALPHA = 1.702
SWIGLU_LIMIT = 7.0

LANES = 128
MOD_ROWS = 16
GLA_CHUNK = 64
MOE_BM = 512
GATHER_CHUNK = 512
VMEM_LIMIT = 56 * 1024 * 1024

LOG2E = 1.4426950408889634

F32 = jnp.float32
BF16 = jnp.bfloat16


def _cp(sem, vmem=VMEM_LIMIT):
    return pltpu.CompilerParams(dimension_semantics=sem, vmem_limit_bytes=vmem)


def _tile(n, pref):
    t = pref
    while n % t:
        t //= 2
    return t


def _dot(a, b):
    return jnp.dot(a, b, preferred_element_type=F32)


def _dot_nt(a, b):
    return lax.dot_general(a, b, (((1,), (1,)), ((), ())), preferred_element_type=F32)


def _dot_tn(a, b):
    return lax.dot_general(a, b, (((0,), (0,)), ((), ())), preferred_element_type=F32)


def _sigmoid(x):
    return 1.0 / (1.0 + jnp.exp(-x))


def _rope(t, cos2, sin2):
    return t * cos2 + pltpu.roll(t, HEAD_DIM // 2, 1) * sin2


def _rms_rows(x):
    return x * lax.rsqrt(jnp.mean(x * x, axis=-1, keepdims=True) + NORM_EPS)


def _ada_kernel(c_ref, w_ref, b_ref, o_ref):
    c = c_ref[...]
    s = (c * _sigmoid(c)).astype(BF16)
    o_ref[...] = _dot(s, w_ref[...].astype(BF16)) + b_ref[...]


def _ada(cond, ada_w, ada_b, tn=1024):
    depth, d, n = ada_w.shape
    return pl.pallas_call(
        _ada_kernel,
        grid=(depth, n // tn),
        in_specs=[
            pl.BlockSpec((MOD_ROWS, d), lambda l, j: (0, 0)),
            pl.BlockSpec((None, d, tn), lambda l, j: (l, 0, j)),
            pl.BlockSpec((None, 1, tn), lambda l, j: (l, 0, j)),
        ],
        out_specs=pl.BlockSpec((None, MOD_ROWS, tn), lambda l, j: (l, 0, j)),
        out_shape=jax.ShapeDtypeStruct((depth, MOD_ROWS, n), F32),
        compiler_params=_cp(("arbitrary", "arbitrary")),
        name="ada_mod",
    )(cond, ada_w, ada_b.reshape(depth, 1, n))


def _norm_mod_kernel(z_ref, g_ref, sh_ref, sc_ref, o_ref):
    y = _rms_rows(z_ref[...]) * g_ref[...]
    o_ref[...] = (y * (1.0 + sc_ref[...]) + sh_ref[...]).astype(o_ref.dtype)


def _norm_mod(z, gain, mods, nseg, bsz, rb):
    t, d = z.shape

    def midx(i):
        return jnp.where(i % nseg == 0, bsz, i // nseg)

    return pl.pallas_call(
        _norm_mod_kernel,
        grid=(t // rb,),
        in_specs=[
            pl.BlockSpec((rb, d), lambda i: (i, 0)),
            pl.BlockSpec((1, d), lambda i: (0, 0)),
            pl.BlockSpec((None, None, 1, d), lambda i: (midx(i), 0, 0, 0)),
            pl.BlockSpec((None, None, 1, d), lambda i: (midx(i), 1, 0, 0)),
        ],
        out_specs=pl.BlockSpec((rb, d), lambda i: (i, 0)),
        out_shape=jax.ShapeDtypeStruct((t, d), BF16),
        compiler_params=_cp(("arbitrary",)),
        name="norm_mod",
    )(z, gain, mods, mods)


def _mm_kernel(x_ref, w_ref, o_ref):
    o_ref[...] = _dot(x_ref[...], w_ref[...]).astype(o_ref.dtype)


def _matmul(x, w, tm, tn, out_dtype=F32, name="matmul"):
    m, k = x.shape
    n = w.shape[1]
    tm, tn = _tile(m, tm), _tile(n, tn)
    return pl.pallas_call(
        _mm_kernel,
        grid=(m // tm, n // tn),
        in_specs=[
            pl.BlockSpec((tm, k), lambda i, j: (i, 0)),
            pl.BlockSpec((k, tn), lambda i, j: (0, j)),
        ],
        out_specs=pl.BlockSpec((tm, tn), lambda i, j: (i, j)),
        out_shape=jax.ShapeDtypeStruct((m, n), out_dtype),
        compiler_params=_cp(("arbitrary", "arbitrary")),
        name=name,
    )(x, w)


def _attn_kernel(q_ref, k_ref, v_ref, cosq_ref, sinq_ref, cosk_ref, sink_ref,
                 qg_ref, kg_ref, o_ref, ks, vs, *, lc, group):
    qi = pl.program_id(2)
    rb = q_ref.shape[0]
    scale = HEAD_DIM ** -0.5

    @pl.when(qi == 0)
    def _prep():
        kn = _rms_rows(k_ref[...]) * kg_ref[...]
        ks[...] = _rope(kn, cosk_ref[...], sink_ref[...]).astype(BF16)
        vs[...] = v_ref[...].astype(BF16)

    def attend(k, v):
        for r in range(group):
            cols = slice(r * HEAD_DIM, (r + 1) * HEAD_DIM)
            qh = _rms_rows(q_ref[:, cols]) * qg_ref[...]
            qh = _rope(qh, cosq_ref[...], sinq_ref[...]).astype(BF16)
            s = _dot_nt(qh, k)
            m = jnp.max(s, axis=-1, keepdims=True)
            p = jnp.exp2((s - m) * (scale * LOG2E))
            l = jnp.sum(p, axis=-1, keepdims=True)
            o_ref[:, cols] = (_dot(p.astype(BF16), v) / l).astype(o_ref.dtype)

    @pl.when(qi == 0)
    def _ctx():
        attend(ks[0:lc], vs[0:lc])

    @pl.when(qi > 0)
    def _lat():
        attend(ks[...], vs[...])


def _attention(proj, cos2, sin2, q_gain, k_gain, bsz, seg, lc):
    t = proj.shape[0]
    nseg = seg // lc
    group = A_Q_HEADS // A_KV_HEADS
    kv_w = A_KV_HEADS
    q_col0 = (2 * kv_w + 2 * B_HEADS) // group
    kern = functools.partial(_attn_kernel, lc=lc, group=group)
    return pl.pallas_call(
        kern,
        grid=(bsz, A_KV_HEADS, nseg),
        in_specs=[
            pl.BlockSpec((lc, group * HEAD_DIM), lambda b, g, i: (b * nseg + i, q_col0 + g)),
            pl.BlockSpec((seg, HEAD_DIM), lambda b, g, i: (b, g)),
            pl.BlockSpec((seg, HEAD_DIM), lambda b, g, i: (b, kv_w + g)),
            pl.BlockSpec((lc, HEAD_DIM), lambda b, g, i: (i, 0)),
            pl.BlockSpec((lc, HEAD_DIM), lambda b, g, i: (i, 0)),
            pl.BlockSpec((seg, HEAD_DIM), lambda b, g, i: (0, 0)),
            pl.BlockSpec((seg, HEAD_DIM), lambda b, g, i: (0, 0)),
            pl.BlockSpec((1, HEAD_DIM), lambda b, g, i: (0, 0)),
            pl.BlockSpec((1, HEAD_DIM), lambda b, g, i: (0, 0)),
        ],
        out_specs=pl.BlockSpec((lc, group * HEAD_DIM), lambda b, g, i: (b * nseg + i, g)),
        out_shape=jax.ShapeDtypeStruct((t, A_Q_HEADS * HEAD_DIM), BF16),
        scratch_shapes=[pltpu.VMEM((seg, HEAD_DIM), BF16), pltpu.VMEM((seg, HEAD_DIM), BF16)],
        compiler_params=_cp(("arbitrary", "arbitrary", "arbitrary")),
        name="gqa_attention",
    )(proj, proj, proj, cos2, sin2, cos2, sin2, q_gain, k_gain)


def _ret_kernel(lg_ref, q_ref, k_ref, v_ref, g_ref, cos_ref, sin_ref, gn_ref, o_ref,
                oacc, qf_s, qb_s, uf_s, ub_s, sf_s, sb_s, *, c, nchunk, nctx):
    h = pl.program_id(1)
    lgf = lg_ref[0, h]
    lgb = lg_ref[1, h]
    cf = float(c)
    n_i = lax.broadcasted_iota(jnp.int32, (c, 1), 0).astype(F32)
    m_i = lax.broadcasted_iota(jnp.int32, (1, c), 1).astype(F32)
    diff = n_i - m_i
    dmat = (jnp.where(diff >= 0, jnp.exp(lgf * jnp.maximum(diff, 0.0)), 0.0)
            + jnp.where(diff <= 0, jnp.exp(lgb * jnp.maximum(-diff, 0.0)), 0.0))
    qdf = jnp.exp(lgf * (n_i + 1.0))
    qdb = jnp.exp(lgb * (cf - n_i))
    kdf = jnp.exp(lgf * (cf - 1.0 - n_i))
    kdb = jnp.exp(lgb * n_i)
    cdf = jnp.exp(lgf * cf)
    cdb = jnp.exp(lgb * cf)
    kscale = HEAD_DIM ** -0.5

    def phase_a(ci, carry):
        r0 = pl.multiple_of(ci * c, c)
        rows = pl.ds(r0, c)
        cos2 = cos_ref[rows, :]
        sin2 = sin_ref[rows, :]
        q = _rope(q_ref[rows, :], cos2, sin2)
        k = _rope(k_ref[rows, :], cos2, sin2) * kscale
        vb = v_ref[rows, :].astype(BF16)
        s = _dot_nt(q.astype(BF16), k.astype(BF16)) * dmat
        oacc[rows, :] = _dot(s.astype(BF16), vb)
        qf_s[rows, :] = (q * qdf).astype(BF16)
        qb_s[rows, :] = (q * qdb).astype(BF16)
        uf_s[ci] = _dot_tn((k * kdf).astype(BF16), vb)
        ub_s[ci] = _dot_tn((k * kdb).astype(BF16), vb)
        return carry

    unroll = 3 if nchunk % 3 == 0 else 1
    lax.fori_loop(0, nchunk, phase_a, 0, unroll=unroll)

    def scan_f(ci, s):
        sf_s[ci] = s.astype(BF16)
        return s * cdf + uf_s[ci]

    lax.fori_loop(0, nchunk, scan_f, jnp.zeros((HEAD_DIM, HEAD_DIM), F32))

    def scan_b(i, s):
        ci = jnp.where(i < nctx, nctx - 1 - i, nchunk - 1 - (i - nctx))
        sb_s[ci] = s.astype(BF16)
        return s * cdb + ub_s[ci]

    lax.fori_loop(0, nchunk, scan_b, jnp.zeros((HEAD_DIM, HEAD_DIM), F32))

    def phase_c(ci, carry):
        r0 = pl.multiple_of(ci * c, c)
        rows = pl.ds(r0, c)
        o = oacc[rows, :] + _dot(qf_s[rows, :], sf_s[ci]) + _dot(qb_s[rows, :], sb_s[ci])
        mu = jnp.mean(o, axis=-1, keepdims=True)
        d = o - mu
        on = d * lax.rsqrt(jnp.mean(d * d, axis=-1, keepdims=True) + NORM_EPS)
        g = g_ref[rows, :]
        o_ref[rows, :] = (on * gn_ref[...] * (g * _sigmoid(g))).astype(o_ref.dtype)
        return carry

    lax.fori_loop(0, nchunk, phase_c, 0, unroll=unroll)


def _retention(proj, log_g, cos2, sin2, gn_gain, bsz, seg, lc):
    t = proj.shape[0]
    nchunk = seg // lc
    kvw = 2 * A_KV_HEADS
    k0, v0 = kvw, kvw + B_HEADS
    q0 = kvw + 2 * B_HEADS + A_Q_HEADS
    g0 = q0 + B_HEADS
    kern = functools.partial(_ret_kernel, c=lc, nchunk=nchunk, nctx=1)
    col = lambda c0: pl.BlockSpec((seg, HEAD_DIM), lambda b, h: (b, c0 + h))
    return pl.pallas_call(
        kern,
        grid=(bsz, B_HEADS),
        in_specs=[
            pl.BlockSpec(memory_space=pltpu.SMEM),
            col(q0), col(k0), col(v0), col(g0),
            pl.BlockSpec((seg, HEAD_DIM), lambda b, h: (0, 0)),
            pl.BlockSpec((seg, HEAD_DIM), lambda b, h: (0, 0)),
            pl.BlockSpec((1, HEAD_DIM), lambda b, h: (0, h)),
        ],
        out_specs=pl.BlockSpec((seg, HEAD_DIM), lambda b, h: (b, h)),
        out_shape=jax.ShapeDtypeStruct((t, B_HEADS * HEAD_DIM), BF16),
        scratch_shapes=[
            pltpu.VMEM((seg, HEAD_DIM), F32),
            pltpu.VMEM((seg, HEAD_DIM), BF16),
            pltpu.VMEM((seg, HEAD_DIM), BF16),
            pltpu.VMEM((nchunk, HEAD_DIM, HEAD_DIM), F32),
            pltpu.VMEM((nchunk, HEAD_DIM, HEAD_DIM), F32),
            pltpu.VMEM((nchunk, HEAD_DIM, HEAD_DIM), BF16),
            pltpu.VMEM((nchunk, HEAD_DIM, HEAD_DIM), BF16),
        ],
        compiler_params=_cp(("arbitrary", "arbitrary")),
        name="retention",
    )(log_g, proj, proj, proj, proj, cos2, sin2, gn_gain)


def _split3(x):
    hi = x.astype(BF16)
    r1 = x - hi.astype(F32)
    mid = r1.astype(BF16)
    lo = (r1 - mid.astype(F32)).astype(BF16)
    return hi, mid, lo


def _hgrn_kernel(ff_ref, fb_ref, v_ref, q_ref, g_ref, lb_ref, gn_ref, o_ref,
                 oacc, qf_s, qb_s, uf_s, ub_s, df_s, db_s, sf_s, sb_s, *, c, nchunk, nctx):
    g = 4 if nchunk % 4 == 0 else (2 if nchunk % 2 == 0 else 1)
    tc = g * c
    lb = lb_ref[...]
    one_m_lb = 1.0 - lb
    n_i = lax.broadcasted_iota(jnp.int32, (tc, tc), 0)
    m_i = lax.broadcasted_iota(jnp.int32, (tc, tc), 1)
    shift = c.bit_length() - 1
    same = jnp.right_shift(n_i, shift) == jnp.right_shift(m_i, shift)
    lower = same & (n_i >= m_i)
    upper = same & (m_i >= n_i)
    tri_l = lower.astype(BF16)
    tri_u = upper.astype(BF16)
    mid = c // 2

    def csum(tri, x):
        hi, md, lo = _split3(x)
        return _dot(tri, hi) + _dot(tri, md) + _dot(tri, lo)

    def chunk_row(x, row):
        return jnp.concatenate(
            [jnp.broadcast_to(x[j * c + row:j * c + row + 1, :], (c, HEAD_DIM)) for j in range(g)], axis=0)

    def phase_a(ti, carry):
        r0 = pl.multiple_of(ti * tc, tc)
        rows = pl.ds(r0, tc)
        qr = q_ref[rows, :]
        q = qr * _sigmoid(qr)
        vb = v_ref[rows, :].astype(BF16)
        frf = ff_ref[rows, :]
        frb = fb_ref[rows, :]
        kf = one_m_lb * _sigmoid(-frf)
        kb = one_m_lb * _sigmoid(-frb)
        lff = jnp.log(lb + one_m_lb * _sigmoid(frf))
        lfb = jnp.log(lb + one_m_lb * _sigmoid(frb))
        cum_f = csum(tri_l, lff)
        cum_b = csum(tri_u, lfb)
        an_f = chunk_row(cum_f, mid)
        an_b = chunk_row(cum_b, mid)
        a_f = _dot_nt((q * jnp.exp(cum_f - an_f)).astype(BF16), (kf * jnp.exp(an_f - cum_f)).astype(BF16))
        a_b = _dot_nt((q * jnp.exp(cum_b - an_b)).astype(BF16), (kb * jnp.exp(an_b - cum_b)).astype(BF16))
        a = jnp.where(lower, a_f, 0.0) + jnp.where(upper, a_b, 0.0)
        oacc[rows, :] = _dot(a.astype(BF16), vb)
        last_f = chunk_row(cum_f, c - 1)
        last_b = chunk_row(cum_b, 0)
        qf_s[rows, :] = (q * jnp.exp(cum_f)).astype(BF16)
        qb_s[rows, :] = (q * jnp.exp(cum_b)).astype(BF16)
        khf = (kf * jnp.exp(last_f - cum_f)).astype(BF16)
        khb = (kb * jnp.exp(last_b - cum_b)).astype(BF16)
        for j in range(g):
            sl = slice(j * c, (j + 1) * c)
            ci = ti * g + j
            uf_s[ci] = _dot_tn(vb[sl], khf[sl])
            ub_s[ci] = _dot_tn(vb[sl], khb[sl])
            df_s[ci] = jnp.exp(last_f[j * c:j * c + 1, :])
            db_s[ci] = jnp.exp(last_b[j * c:j * c + 1, :])
        return carry

    lax.fori_loop(0, nchunk // g, phase_a, 0)
    unroll = g

    def scan_f(ci, s):
        sf_s[ci] = s.astype(BF16)
        return s * df_s[ci] + uf_s[ci]

    lax.fori_loop(0, nchunk, scan_f, jnp.zeros((HEAD_DIM, HEAD_DIM), F32))

    def scan_b(i, s):
        ci = jnp.where(i < nctx, nctx - 1 - i, nchunk - 1 - (i - nctx))
        sb_s[ci] = s.astype(BF16)
        return s * db_s[ci] + ub_s[ci]

    lax.fori_loop(0, nchunk, scan_b, jnp.zeros((HEAD_DIM, HEAD_DIM), F32))

    def phase_c(ci, carry):
        r0 = pl.multiple_of(ci * c, c)
        rows = pl.ds(r0, c)
        o = oacc[rows, :] + _dot_nt(qf_s[rows, :], sf_s[ci]) + _dot_nt(qb_s[rows, :], sb_s[ci])
        on = _rms_rows(o)
        g = g_ref[rows, :]
        o_ref[rows, :] = (on * gn_ref[...] * (g * _sigmoid(g))).astype(o_ref.dtype)
        return carry

    lax.fori_loop(0, nchunk, phase_c, 0, unroll=unroll)


def _hgrn(proj, lb, gn_gain, bsz, seg, lc):
    t = proj.shape[0]
    c = GLA_CHUNK
    nchunk = seg // c
    nctx = lc // c
    kern = functools.partial(_hgrn_kernel, c=c, nchunk=nchunk, nctx=nctx)
    col = lambda c0: pl.BlockSpec((seg, HEAD_DIM), lambda b, h: (b, c0 + h))
    vec = pl.BlockSpec((1, HEAD_DIM), lambda b, h: (0, h))
    return pl.pallas_call(
        kern,
        grid=(bsz, C_HEADS),
        in_specs=[col(0), col(C_HEADS), col(2 * C_HEADS), col(3 * C_HEADS), col(4 * C_HEADS), vec, vec],
        out_specs=pl.BlockSpec((seg, HEAD_DIM), lambda b, h: (b, h)),
        out_shape=jax.ShapeDtypeStruct((t, C_HEADS * HEAD_DIM), BF16),
        scratch_shapes=[
            pltpu.VMEM((seg, HEAD_DIM), F32),
            pltpu.VMEM((seg, HEAD_DIM), BF16),
            pltpu.VMEM((seg, HEAD_DIM), BF16),
            pltpu.VMEM((nchunk, HEAD_DIM, HEAD_DIM), F32),
            pltpu.VMEM((nchunk, HEAD_DIM, HEAD_DIM), F32),
            pltpu.VMEM((nchunk, 1, HEAD_DIM), F32),
            pltpu.VMEM((nchunk, 1, HEAD_DIM), F32),
            pltpu.VMEM((nchunk, HEAD_DIM, HEAD_DIM), BF16),
            pltpu.VMEM((nchunk, HEAD_DIM, HEAD_DIM), BF16),
        ],
        compiler_params=_cp(("arbitrary", "arbitrary")),
        name="hgrn2",
    )(proj, proj, proj, proj, proj, lb, gn_gain)


def _out_kernel(*refs, n_in):
    xs = refs[:n_in]
    ws = refs[n_in:2 * n_in]
    z_ref, gate_ref, gain_ref, sh_ref, sc_ref, wr_ref, br_ref = refs[2 * n_in:2 * n_in + 7]
    z_out, h_out, lg_out = refs[2 * n_in + 7:]
    o = _dot(xs[0][...], ws[0][...])
    for x_ref, w_ref in zip(xs[1:], ws[1:]):
        o = o + _dot(x_ref[...], w_ref[...])
    z1 = z_ref[...] + gate_ref[...] * o
    z_out[...] = z1
    h = _rms_rows(z1) * gain_ref[...]
    h = h * (1.0 + sc_ref[...]) + sh_ref[...]
    hb = h.astype(BF16)
    h_out[...] = hb
    lg_out[...] = _dot(hb, wr_ref[...]) + br_ref[...]


def _out_proj(xs, ws, z, mods, gain, wr, br, bsz, nseg, off, rb):
    d = z.shape[1]
    nout = nseg - off
    n_in = len(xs)
    npad = wr.shape[1]

    def rin(b, j):
        return (b * nseg + off + j, 0)

    def rout(b, j):
        return (b * nout + j, 0)

    def mod(which):
        return pl.BlockSpec((None, None, 1, d),
                            lambda b, j: (jnp.where(j + off == 0, bsz, b), which, 0, 0))

    in_specs = [pl.BlockSpec((rb, x.shape[1]), rin) for x in xs]
    in_specs += [pl.BlockSpec(w.shape, lambda b, j: (0, 0)) for w in ws]
    in_specs += [
        pl.BlockSpec((rb, d), rin),
        mod(2),
        pl.BlockSpec((1, d), lambda b, j: (0, 0)),
        mod(3), mod(4),
        pl.BlockSpec(wr.shape, lambda b, j: (0, 0)),
        pl.BlockSpec((1, npad), lambda b, j: (0, 0)),
    ]
    tm = bsz * nout * rb
    return pl.pallas_call(
        functools.partial(_out_kernel, n_in=n_in),
        grid=(bsz, nout),
        in_specs=in_specs,
        out_specs=[pl.BlockSpec((rb, d), rout), pl.BlockSpec((rb, d), rout), pl.BlockSpec((rb, npad), rout)],
        out_shape=[jax.ShapeDtypeStruct((tm, d), F32), jax.ShapeDtypeStruct((tm, d), BF16),
                   jax.ShapeDtypeStruct((tm, npad), F32)],
        compiler_params=_cp(("arbitrary", "arbitrary")),
        name="out_proj",
    )(*xs, *ws, z, mods, gain, mods, mods, wr, br)


def _topk_kernel(lg_ref, idx_ref, gate_ref):
    l = lg_ref[...]
    lane = lax.broadcasted_iota(jnp.int32, l.shape, 1)
    vals, idxs = [], []
    for _ in range(TOP_K):
        m = jnp.max(l, axis=-1, keepdims=True)
        i = jnp.min(jnp.where(l == m, lane, LANES), axis=-1, keepdims=True)
        vals.append(m)
        idxs.append(i)
        l = jnp.where(lane == i, -jnp.inf, l)
    es = [jnp.exp(v - vals[0]) for v in vals]
    den = es[0] + es[1] + es[2] + es[3]
    io = jnp.zeros(l.shape, jnp.int32)
    go = jnp.zeros(l.shape, F32)
    for k in range(TOP_K):
        io = jnp.where(lane == k, idxs[k], io)
        go = jnp.where(lane == k, es[k] / den, go)
    idx_ref[...] = io
    gate_ref[...] = go


def _topk(logits, tm=512):
    t, n = logits.shape
    tm = _tile(t, tm)
    return pl.pallas_call(
        _topk_kernel,
        grid=(t // tm,),
        in_specs=[pl.BlockSpec((tm, n), lambda i: (i, 0))],
        out_specs=[pl.BlockSpec((tm, n), lambda i: (i, 0)), pl.BlockSpec((tm, n), lambda i: (i, 0))],
        out_shape=[jax.ShapeDtypeStruct((t, n), jnp.int32), jax.ShapeDtypeStruct((t, n), F32)],
        compiler_params=_cp(("arbitrary",)),
        name="router_topk",
    )(logits)


def _gather_kernel(idx0_ref, idxn_ref, src_ref, out_ref, buf, gsem, osem, *, chunk):
    i = pl.program_id(0)
    n = pl.num_programs(0)
    slot = i % 2
    other = 1 - slot

    def out_copy(step, s):
        return pltpu.make_async_copy(buf.at[s], out_ref.at[pl.ds(step * chunk, chunk)], osem.at[s])

    def issue_rows(idx_ref, s):
        def body(p, carry):
            r = 2 * p
            pltpu.make_async_copy(src_ref.at[idx_ref[0, r]], buf.at[s, r], gsem.at[s]).start(priority=0)
            pltpu.make_async_copy(src_ref.at[idx_ref[0, r + 1]], buf.at[s, r + 1], gsem.at[s]).start(priority=1)
            return carry

        lax.fori_loop(0, chunk // 2, body, 0)

    @pl.when(i == 0)
    def _first():
        issue_rows(idx0_ref, 0)

    @pl.when(i + 1 < n)
    def _next():
        @pl.when(i >= 1)
        def _free_slot():
            out_copy(i - 1, other).wait()

        issue_rows(idxn_ref, other)

    pltpu.make_async_copy(src_ref.at[pl.ds(0, chunk)], buf.at[slot], gsem.at[slot]).wait()
    out_copy(i, slot).start()

    @pl.when(i == n - 1)
    def _flush():
        out_copy(i, slot).wait()

        @pl.when(i >= 1)
        def _prev():
            out_copy(i - 1, other).wait()


def _gather_rows(src, idx, chunk=GATHER_CHUNK):
    r = idx.shape[0]
    d = src.shape[1:]
    nchunks = r // chunk
    assert src.shape[0] >= chunk
    idx3 = idx.reshape(nchunks, 1, chunk)
    return pl.pallas_call(
        functools.partial(_gather_kernel, chunk=chunk),
        grid=(nchunks,),
        in_specs=[
            pl.BlockSpec((None, 1, chunk), lambda i: (0, 0, 0), memory_space=pltpu.SMEM),
            pl.BlockSpec((None, 1, chunk), lambda i: (jnp.minimum(i + 1, nchunks - 1), 0, 0),
                         memory_space=pltpu.SMEM),
            pl.BlockSpec(memory_space=pl.ANY),
        ],
        out_specs=pl.BlockSpec(memory_space=pl.ANY),
        out_shape=jax.ShapeDtypeStruct((r,) + d, src.dtype),
        scratch_shapes=[pltpu.VMEM((2, chunk) + d, src.dtype), pltpu.SemaphoreType.DMA((2,)),
                        pltpu.SemaphoreType.DMA((2,))],
        compiler_params=_cp(("arbitrary",)),
        name="row_gather",
    )(idx3, idx3, src)


def _gmm1_kernel(be_ref, first_ref, nused_ref, x_ref, w_ref, b_ref, o_ref, wsc, hsc):
    i = pl.program_id(1)
    nu = nused_ref[0]
    tn = w_ref.shape[1]

    def matmul(slot):
        hsc[slot] = _dot(x_ref[...], wsc[...]) + b_ref[...]

    def activate(slot):
        w2 = 2 * LANES
        rsel = lax.broadcasted_iota(jnp.int32, (w2, LANES), 0)
        csel = lax.broadcasted_iota(jnp.int32, (w2, LANES), 1)
        sel = (rsel == 2 * csel).astype(BF16)
        for s in range(tn // w2):
            parts = []
            for u in range(2):
                lo = s * w2 + u * LANES
                hid = hsc[slot, :, lo:lo + LANES]
                glu = jnp.minimum(hid, SWIGLU_LIMIT)
                glu = glu * _sigmoid(SWIGLU_ALPHA * glu)
                lin = jnp.clip(hid, -SWIGLU_LIMIT, SWIGLU_LIMIT) + 1.0
                parts.append((glu * pltpu.roll(lin, LANES - 1, 1)).astype(BF16))
            prod = jnp.concatenate(parts, axis=1)
            o_ref[:, s * LANES:(s + 1) * LANES] = _dot(prod, sel).astype(o_ref.dtype)

    @pl.when((i < nu) & (first_ref[jnp.minimum(i, nu - 1)] == 1))
    def _cast():
        wsc[...] = w_ref[...].astype(BF16)

    @pl.when((i == 0) & (nu > 0))
    def _head():
        matmul(0)

    for par in range(2):
        @pl.when((i >= 1) & (i < nu) & (i % 2 == par))
        def _steady():
            matmul(par)
            activate(1 - par)

        @pl.when((i >= 1) & (i == nu) & (i % 2 == par))
        def _tail():
            activate(1 - par)

    @pl.when(i > nu)
    def _unused():
        o_ref[...] = jnp.zeros(o_ref.shape, o_ref.dtype)


def _gmm2_kernel(be_ref, first_ref, nused_ref, x_ref, w_ref, b_ref, o_ref, wsc):
    i = pl.program_id(1)

    @pl.when(i < nused_ref[0])
    def _():
        @pl.when(first_ref[i] == 1)
        def _cast():
            wsc[...] = w_ref[...].astype(BF16)

        o_ref[...] = (_dot(x_ref[...], wsc[...]) + b_ref[...]).astype(o_ref.dtype)

    @pl.when(i >= nused_ref[0])
    def _unused():
        o_ref[...] = jnp.zeros(o_ref.shape, o_ref.dtype)


def _gmm(kernel, x, w, b, layer, be, first, nused, bm, tn, out_cols, out_dtype, name, skew=False):
    r, k = x.shape
    n = w.shape[3]
    tn = _tile(n, tn)
    nb = r // bm
    n_tiles = n // tn
    oc = out_cols // n_tiles
    lag = 1 if skew else 0

    def blk(i, nu):
        return jnp.minimum(i, nu[0] - 1)

    scratch = [pltpu.VMEM((k, tn), BF16)]
    if skew:
        scratch.append(pltpu.VMEM((2, bm, tn), F32))
    grid_spec = pltpu.PrefetchScalarGridSpec(
        num_scalar_prefetch=3,
        grid=(n_tiles, nb + lag),
        in_specs=[
            pl.BlockSpec((bm, k), lambda j, i, be, fi, nu: (blk(i, nu), 0)),
            pl.BlockSpec((None, None, k, tn), lambda j, i, be, fi, nu: (layer, be[blk(i, nu)], 0, j)),
            pl.BlockSpec((None, None, 1, tn), lambda j, i, be, fi, nu: (layer, be[blk(i, nu)], 0, j)),
        ],
        out_specs=pl.BlockSpec((bm, oc), lambda j, i, be, fi, nu: (jnp.maximum(i - lag, 0), j)),
        scratch_shapes=scratch,
    )
    return pl.pallas_call(
        kernel,
        grid_spec=grid_spec,
        out_shape=jax.ShapeDtypeStruct((r, out_cols), out_dtype),
        compiler_params=_cp(("arbitrary", "arbitrary")),
        name=name,
    )(be, first, nused, x, w, b.reshape(b.shape[0], b.shape[1], 1, n))


def _combine_kernel(z_ref, y0_ref, y1_ref, y2_ref, y3_ref, gt_ref, gate_ref, gain_ref, sh_ref, sc_ref,
                    *outs, final):
    gt = gt_ref[...]
    ffn = gt[:, 0:1] * y0_ref[...].astype(F32)
    for k, y_ref in enumerate((y1_ref, y2_ref, y3_ref), start=1):
        ffn = ffn + gt[:, k:k + 1] * y_ref[...].astype(F32)
    z2 = z_ref[...] + gate_ref[...] * ffn
    h = _rms_rows(z2) * gain_ref[...]
    if final:
        outs[0][...] = h
    else:
        outs[0][...] = z2
        outs[1][...] = (h * (1.0 + sc_ref[...]) + sh_ref[...]).astype(outs[1].dtype)


def _combine(z1, yt, gates, mods_cur, mods_next, gain, bsz, nblk, rb, final):
    t, d = z1.shape
    has_ctx = not final

    def midx(i):
        if has_ctx:
            return jnp.where(i % nblk == 0, bsz, i // nblk)
        return i // nblk

    row = lambda i: (i, 0)
    nrow = t // rb
    out_specs = [pl.BlockSpec((rb, d), row)]
    out_shape = [jax.ShapeDtypeStruct((t, d), F32)]
    if not final:
        out_specs.append(pl.BlockSpec((rb, d), row))
        out_shape.append(jax.ShapeDtypeStruct((t, d), BF16))
    y_specs = [pl.BlockSpec((rb, d), functools.partial(lambda i, k: (k * nrow + i, 0), k=k))
               for k in range(TOP_K)]
    return pl.pallas_call(
        functools.partial(_combine_kernel, final=final),
        grid=(nrow,),
        in_specs=[
            pl.BlockSpec((rb, d), row),
            *y_specs,
            pl.BlockSpec((rb, gates.shape[1]), row),
            pl.BlockSpec((None, None, 1, d), lambda i: (midx(i), 5, 0, 0)),
            pl.BlockSpec((1, d), lambda i: (0, 0)),
            pl.BlockSpec((None, None, 1, d), lambda i: (midx(i), 0, 0, 0)),
            pl.BlockSpec((None, None, 1, d), lambda i: (midx(i), 1, 0, 0)),
        ],
        out_specs=out_specs,
        out_shape=out_shape,
        compiler_params=_cp(("arbitrary",)),
        name="moe_combine",
    )(z1, yt, yt, yt, yt, gates, mods_cur, gain, mods_next, mods_next)


def _routing(idx, n_exp, bm, gchunk):
    tm = idx.shape[0]
    onehot = (idx[:, :, None] == jnp.arange(n_exp, dtype=jnp.int32)).astype(jnp.int32).sum(axis=1)
    csum = jnp.cumsum(onehot, axis=0)
    counts = csum[-1]
    rank = jnp.take_along_axis(csum - onehot, idx, axis=1)
    padded = (counts + bm - 1) // bm * bm
    pad_end = jnp.cumsum(padded)
    pad_start = pad_end - padded
    dest = pad_start[idx] + rank
    nb = (tm * TOP_K + n_exp * (bm - 1)) // bm
    nb = -(-(nb * bm) // gchunk) * gchunk // bm
    tok = jnp.repeat(jnp.arange(tm, dtype=jnp.int32), TOP_K)
    row_tok = (jnp.arange(nb * bm, dtype=jnp.int32) % tm).at[dest.reshape(-1)].set(tok)
    nused = (pad_end[-1] // bm).astype(jnp.int32).reshape(1)
    block_start = jnp.arange(nb, dtype=jnp.int32) * bm
    be = jnp.minimum((pad_end[None, :] <= block_start[:, None]).astype(jnp.int32).sum(axis=1), n_exp - 1)
    first = jnp.concatenate([jnp.ones((1,), jnp.int32), (be[1:] != be[:-1]).astype(jnp.int32)])
    dest_kmajor = dest.T.reshape(-1).astype(jnp.int32)
    return dest_kmajor, row_tok, be, first, nused


def _moe(h2, logits, w1, b1, w2, b2, layer):
    tm, d = h2.shape
    n_exp = w1.shape[1]
    sub = d // LANES
    idx_p, gate_p = _topk(logits)
    idx = idx_p[:, :TOP_K]
    dest, row_tok, be, first, nused = _routing(idx, n_exp, MOE_BM, GATHER_CHUNK)
    xs = _gather_rows(h2.reshape(tm, sub, LANES), row_tok).reshape(-1, d)
    act = _gmm(_gmm1_kernel, xs, w1, b1, layer, be, first, nused, MOE_BM, 1024, w1.shape[3] // 2, BF16, "moe_up",
               skew=True)
    y = _gmm(_gmm2_kernel, act, w2, b2, layer, be, first, nused, MOE_BM, 1024, w2.shape[3], BF16, "moe_down")
    yt = _gather_rows(y.reshape(-1, sub, LANES), dest).reshape(TOP_K * tm, d)
    return yt, gate_p


def _rope_tables(n_lat, lc):
    n_rows = n_lat // GRID_W
    row = jnp.repeat(jnp.arange(n_rows, dtype=F32), GRID_W)
    col = jnp.tile(jnp.arange(GRID_W, dtype=F32), n_rows)
    n_freq = HEAD_DIM // 4
    inv_freq = ROPE_THETA ** (-jnp.arange(n_freq, dtype=F32) / n_freq)
    ang = jnp.concatenate([row[:, None] * inv_freq, col[:, None] * inv_freq], axis=-1)
    cos, sin = jnp.cos(ang), jnp.sin(ang)
    cos2 = jnp.concatenate([cos, cos], axis=-1)
    sin2 = jnp.concatenate([-sin, sin], axis=-1)
    cos2 = jnp.concatenate([jnp.ones((lc, HEAD_DIM), F32), cos2], axis=0)
    sin2 = jnp.concatenate([jnp.zeros((lc, HEAD_DIM), F32), sin2], axis=0)
    return cos2, sin2


def kernel(x, c, ctx, c_ctx, ada_w, ada_b, norm_mix, norm_ffn, ab_w_in, ab_w_out, a_q_norm, a_k_norm, b_decay_exp, b_gn, c_w_in, c_w_out, c_lb, c_gn, router_w, router_b, exp_w1, exp_b1, exp_w2, exp_b2, norm_final):
    bsz, n_lat, d = x.shape
    lc = ctx.shape[1]
    depth = ada_w.shape[0]
    assert depth == 2 and n_lat % lc == 0 and bsz < MOD_ROWS and lc % GLA_CHUNK == 0
    seg = lc + n_lat
    nseg = seg // lc
    n_exp = router_w.shape[2]

    z = jnp.concatenate([ctx, x], axis=1).reshape(bsz * seg, d)

    cond = jnp.zeros((MOD_ROWS, d), F32).at[:bsz].set(c).at[bsz].set(c_ctx)
    mods = _ada(cond, ada_w, ada_b).reshape(depth, MOD_ROWS, N_MOD, 1, d)
    cos2, sin2 = _rope_tables(n_lat, lc)

    lb_soft = jax.nn.softmax(c_lb.astype(F32), axis=0)
    lower_bounds = jnp.cumsum(lb_soft, axis=0) - lb_soft[0]
    log_g = jnp.log1p(-jnp.exp2(-b_decay_exp[0].astype(F32)))

    wr = jnp.zeros((depth, d, LANES), BF16).at[:, :, :n_exp].set(router_w.astype(BF16))
    br = jnp.full((depth, 1, LANES), -1e30, F32).at[:, 0, :n_exp].set(router_b)

    hz = _norm_mod(z, norm_mix[0:1], mods[0], nseg, bsz, lc)
    proj = _matmul(hz, ab_w_in[0].astype(BF16), 1024, 512, name="in_proj0")
    att = _attention(proj, cos2, sin2, a_q_norm[0:1], a_k_norm[0:1], bsz, seg, lc)
    ret = _retention(proj, log_g, cos2, sin2, b_gn[0:1], bsz, seg, lc)
    w_out = ab_w_out[0].astype(BF16)
    aw = A_Q_HEADS * HEAD_DIM
    z1, h2, logits = _out_proj([att, ret], [w_out[:aw], w_out[aw:]], z, mods[0], norm_ffn[0:1],
                               wr[0], br[0], bsz, nseg, 0, lc)
    yt, gates = _moe(h2, logits, exp_w1, exp_b1, exp_w2, exp_b2, 0)
    z2, hz = _combine(z1, yt, gates, mods[0], mods[1], norm_mix[1:2], bsz, nseg, lc, final=False)

    proj = _matmul(hz, c_w_in[0].astype(BF16), 1024, 512, name="in_proj1")
    hg = _hgrn(proj, lower_bounds[1:2], c_gn[0:1], bsz, seg, lc)
    z1, h2, logits = _out_proj([hg], [c_w_out[0].astype(BF16)], z2, mods[1], norm_ffn[1:2],
                               wr[1], br[1], bsz, nseg, 1, lc)
    yt, gates = _moe(h2, logits, exp_w1, exp_b1, exp_w2, exp_b2, 1)
    (out,) = _combine(z1, yt, gates, mods[1], mods[1], norm_final.reshape(1, d), bsz, nseg - 1, lc, final=True)
    return out.reshape(bsz, n_lat, d)
```

```python
import functools

import jax
import jax.numpy as jnp
from jax import lax
from jax.experimental import pallas as pl
from jax.experimental.pallas import tpu as pltpu

HEAD_DIM = 128
GRID_W = 64
ROPE_THETA = 10000.0
NORM_EPS = 1e-6
N_MOD = 6
A_Q_HEADS = 8
A_KV_HEADS = 2
B_HEADS = 8
C_HEADS = 16
TOP_K = 4
SWIGLU_ALPHA = 1.702
SWIGLU_LIMIT = 7.0

LANES = 128
MOD_ROWS = 16
GLA_CHUNK = 64
MOE_BM = 512
GATHER_CHUNK = 512
VMEM_LIMIT = 56 * 1024 * 1024

LOG2E = 1.4426950408889634

F32 = jnp.float32
BF16 = jnp.bfloat16


def _cp(sem, vmem=VMEM_LIMIT):
    return pltpu.CompilerParams(dimension_semantics=sem, vmem_limit_bytes=vmem)


def _tile(n, pref):
    t = pref
    while n % t:
        t //= 2
    return t


def _dot(a, b):
    return jnp.dot(a, b, preferred_element_type=F32)


def _dot_nt(a, b):
    return lax.dot_general(a, b, (((1,), (1,)), ((), ())), preferred_element_type=F32)


def _dot_tn(a, b):
    return lax.dot_general(a, b, (((0,), (0,)), ((), ())), preferred_element_type=F32)


def _sigmoid(x):
    return 1.0 / (1.0 + jnp.exp(-x))


def _rope(t, cos2, sin2):
    return t * cos2 + pltpu.roll(t, HEAD_DIM // 2, 1) * sin2


def _rms_rows(x):
    return x * lax.rsqrt(jnp.mean(x * x, axis=-1, keepdims=True) + NORM_EPS)


def _ada_kernel(c_ref, w_ref, b_ref, o_ref):
    c = c_ref[...]
    s = (c * _sigmoid(c)).astype(BF16)
    o_ref[...] = _dot(s, w_ref[...].astype(BF16)) + b_ref[...]


def _ada(cond, ada_w, ada_b, tn=1024):
    depth, d, n = ada_w.shape
    return pl.pallas_call(
        _ada_kernel,
        grid=(depth, n // tn),
        in_specs=[
            pl.BlockSpec((MOD_ROWS, d), lambda l, j: (0, 0)),
            pl.BlockSpec((None, d, tn), lambda l, j: (l, 0, j)),
            pl.BlockSpec((None, 1, tn), lambda l, j: (l, 0, j)),
        ],
        out_specs=pl.BlockSpec((None, MOD_ROWS, tn), lambda l, j: (l, 0, j)),
        out_shape=jax.ShapeDtypeStruct((depth, MOD_ROWS, n), F32),
        compiler_params=_cp(("arbitrary", "arbitrary")),
        name="ada_mod",
    )(cond, ada_w, ada_b.reshape(depth, 1, n))


def _norm_mod_kernel(z_ref, g_ref, sh_ref, sc_ref, o_ref):
    y = _rms_rows(z_ref[...]) * g_ref[...]
    o_ref[...] = (y * (1.0 + sc_ref[...]) + sh_ref[...]).astype(o_ref.dtype)


def _norm_mod(z, gain, mods, nseg, bsz, rb):
    t, d = z.shape

    def midx(i):
        return jnp.where(i % nseg == 0, bsz, i // nseg)

    return pl.pallas_call(
        _norm_mod_kernel,
        grid=(t // rb,),
        in_specs=[
            pl.BlockSpec((rb, d), lambda i: (i, 0)),
            pl.BlockSpec((1, d), lambda i: (0, 0)),
            pl.BlockSpec((None, None, 1, d), lambda i: (midx(i), 0, 0, 0)),
            pl.BlockSpec((None, None, 1, d), lambda i: (midx(i), 1, 0, 0)),
        ],
        out_specs=pl.BlockSpec((rb, d), lambda i: (i, 0)),
        out_shape=jax.ShapeDtypeStruct((t, d), BF16),
        compiler_params=_cp(("arbitrary",)),
        name="norm_mod",
    )(z, gain, mods, mods)


def _mm_kernel(x_ref, w_ref, o_ref):
    o_ref[...] = _dot(x_ref[...], w_ref[...]).astype(o_ref.dtype)


def _matmul(x, w, tm, tn, out_dtype=F32, name="matmul"):
    m, k = x.shape
    n = w.shape[1]
    tm, tn = _tile(m, tm), _tile(n, tn)
    return pl.pallas_call(
        _mm_kernel,
        grid=(m // tm, n // tn),
        in_specs=[
            pl.BlockSpec((tm, k), lambda i, j: (i, 0)),
            pl.BlockSpec((k, tn), lambda i, j: (0, j)),
        ],
        out_specs=pl.BlockSpec((tm, tn), lambda i, j: (i, j)),
        out_shape=jax.ShapeDtypeStruct((m, n), out_dtype),
        compiler_params=_cp(("arbitrary", "arbitrary")),
        name=name,
    )(x, w)


def _attn_kernel(q_ref, k_ref, v_ref, cosq_ref, sinq_ref, cosk_ref, sink_ref,
                 qg_ref, kg_ref, o_ref, ks, vs, *, lc, group):
    qi = pl.program_id(2)
    rb = q_ref.shape[0]
    scale = HEAD_DIM ** -0.5

    @pl.when(qi == 0)
    def _prep():
        kn = _rms_rows(k_ref[...]) * kg_ref[...]
        ks[...] = _rope(kn, cosk_ref[...], sink_ref[...]).astype(BF16)
        vs[...] = v_ref[...].astype(BF16)

    def attend(k, v):
        for r in range(group):
            cols = slice(r * HEAD_DIM, (r + 1) * HEAD_DIM)
            qh = _rms_rows(q_ref[:, cols]) * qg_ref[...]
            qh = _rope(qh, cosq_ref[...], sinq_ref[...]).astype(BF16)
            s = _dot_nt(qh, k)
            m = jnp.max(s, axis=-1, keepdims=True)
            p = jnp.exp2((s - m) * (scale * LOG2E))
            l = jnp.sum(p, axis=-1, keepdims=True)
            o_ref[:, cols] = (_dot(p.astype(BF16), v) / l).astype(o_ref.dtype)

    @pl.when(qi == 0)
    def _ctx():
        attend(ks[0:lc], vs[0:lc])

    @pl.when(qi > 0)
    def _lat():
        attend(ks[...], vs[...])


def _attention(proj, cos2, sin2, q_gain, k_gain, bsz, seg, lc):
    t = proj.shape[0]
    nseg = seg // lc
    group = A_Q_HEADS // A_KV_HEADS
    kv_w = A_KV_HEADS
    q_col0 = (2 * kv_w + 2 * B_HEADS) // group
    kern = functools.partial(_attn_kernel, lc=lc, group=group)
    return pl.pallas_call(
        kern,
        grid=(bsz, A_KV_HEADS, nseg),
        in_specs=[
            pl.BlockSpec((lc, group * HEAD_DIM), lambda b, g, i: (b * nseg + i, q_col0 + g)),
            pl.BlockSpec((seg, HEAD_DIM), lambda b, g, i: (b, g)),
            pl.BlockSpec((seg, HEAD_DIM), lambda b, g, i: (b, kv_w + g)),
            pl.BlockSpec((lc, HEAD_DIM), lambda b, g, i: (i, 0)),
            pl.BlockSpec((lc, HEAD_DIM), lambda b, g, i: (i, 0)),
            pl.BlockSpec((seg, HEAD_DIM), lambda b, g, i: (0, 0)),
            pl.BlockSpec((seg, HEAD_DIM), lambda b, g, i: (0, 0)),
            pl.BlockSpec((1, HEAD_DIM), lambda b, g, i: (0, 0)),
            pl.BlockSpec((1, HEAD_DIM), lambda b, g, i: (0, 0)),
        ],
        out_specs=pl.BlockSpec((lc, group * HEAD_DIM), lambda b, g, i: (b * nseg + i, g)),
        out_shape=jax.ShapeDtypeStruct((t, A_Q_HEADS * HEAD_DIM), BF16),
        scratch_shapes=[pltpu.VMEM((seg, HEAD_DIM), BF16), pltpu.VMEM((seg, HEAD_DIM), BF16)],
        compiler_params=_cp(("arbitrary", "arbitrary", "arbitrary")),
        name="gqa_attention",
    )(proj, proj, proj, cos2, sin2, cos2, sin2, q_gain, k_gain)


def _ret_kernel(lg_ref, q_ref, k_ref, v_ref, g_ref, cos_ref, sin_ref, gn_ref, o_ref,
                oacc, qf_s, qb_s, uf_s, ub_s, sf_s, sb_s, *, c, nchunk, nctx):
    h = pl.program_id(1)
    lgf = lg_ref[0, h]
    lgb = lg_ref[1, h]
    cf = float(c)
    n_i = lax.broadcasted_iota(jnp.int32, (c, 1), 0).astype(F32)
    m_i = lax.broadcasted_iota(jnp.int32, (1, c), 1).astype(F32)
    diff = n_i - m_i
    dmat = (jnp.where(diff >= 0, jnp.exp(lgf * jnp.maximum(diff, 0.0)), 0.0)
            + jnp.where(diff <= 0, jnp.exp(lgb * jnp.maximum(-diff, 0.0)), 0.0))
    qdf = jnp.exp(lgf * (n_i + 1.0))
    qdb = jnp.exp(lgb * (cf - n_i))
    kdf = jnp.exp(lgf * (cf - 1.0 - n_i))
    kdb = jnp.exp(lgb * n_i)
    cdf = jnp.exp(lgf * cf)
    cdb = jnp.exp(lgb * cf)
    kscale = HEAD_DIM ** -0.5

    def phase_a(ci, carry):
        r0 = pl.multiple_of(ci * c, c)
        rows = pl.ds(r0, c)
        cos2 = cos_ref[rows, :]
        sin2 = sin_ref[rows, :]
        q = _rope(q_ref[rows, :], cos2, sin2)
        k = _rope(k_ref[rows, :], cos2, sin2) * kscale
        vb = v_ref[rows, :].astype(BF16)
        s = _dot_nt(q.astype(BF16), k.astype(BF16)) * dmat
        oacc[rows, :] = _dot(s.astype(BF16), vb)
        qf_s[rows, :] = (q * qdf).astype(BF16)
        qb_s[rows, :] = (q * qdb).astype(BF16)
        uf_s[ci] = _dot_tn((k * kdf).astype(BF16), vb)
        ub_s[ci] = _dot_tn((k * kdb).astype(BF16), vb)
        return carry

    unroll = 3 if nchunk % 3 == 0 else 1
    lax.fori_loop(0, nchunk, phase_a, 0, unroll=unroll)

    def scan_f(ci, s):
        sf_s[ci] = s.astype(BF16)
        return s * cdf + uf_s[ci]

    lax.fori_loop(0, nchunk, scan_f, jnp.zeros((HEAD_DIM, HEAD_DIM), F32))

    def scan_b(i, s):
        ci = jnp.where(i < nctx, nctx - 1 - i, nchunk - 1 - (i - nctx))
        sb_s[ci] = s.astype(BF16)
        return s * cdb + ub_s[ci]

    lax.fori_loop(0, nchunk, scan_b, jnp.zeros((HEAD_DIM, HEAD_DIM), F32))

    def phase_c(ci, carry):
        r0 = pl.multiple_of(ci * c, c)
        rows = pl.ds(r0, c)
        o = oacc[rows, :] + _dot(qf_s[rows, :], sf_s[ci]) + _dot(qb_s[rows, :], sb_s[ci])
        mu = jnp.mean(o, axis=-1, keepdims=True)
        d = o - mu
        on = d * lax.rsqrt(jnp.mean(d * d, axis=-1, keepdims=True) + NORM_EPS)
        g = g_ref[rows, :]
        o_ref[rows, :] = (on * gn_ref[...] * (g * _sigmoid(g))).astype(o_ref.dtype)
        return carry

    lax.fori_loop(0, nchunk, phase_c, 0, unroll=unroll)


def _retention(proj, log_g, cos2, sin2, gn_gain, bsz, seg, lc):
    t = proj.shape[0]
    nchunk = seg // lc
    kvw = 2 * A_KV_HEADS
    k0, v0 = kvw, kvw + B_HEADS
    q0 = kvw + 2 * B_HEADS + A_Q_HEADS
    g0 = q0 + B_HEADS
    kern = functools.partial(_ret_kernel, c=lc, nchunk=nchunk, nctx=1)
    col = lambda c0: pl.BlockSpec((seg, HEAD_DIM), lambda b, h: (b, c0 + h))
    return pl.pallas_call(
        kern,
        grid=(bsz, B_HEADS),
        in_specs=[
            pl.BlockSpec(memory_space=pltpu.SMEM),
            col(q0), col(k0), col(v0), col(g0),
            pl.BlockSpec((seg, HEAD_DIM), lambda b, h: (0, 0)),
            pl.BlockSpec((seg, HEAD_DIM), lambda b, h: (0, 0)),
            pl.BlockSpec((1, HEAD_DIM), lambda b, h: (0, h)),
        ],
        out_specs=pl.BlockSpec((seg, HEAD_DIM), lambda b, h: (b, h)),
        out_shape=jax.ShapeDtypeStruct((t, B_HEADS * HEAD_DIM), BF16),
        scratch_shapes=[
            pltpu.VMEM((seg, HEAD_DIM), F32),
            pltpu.VMEM((seg, HEAD_DIM), BF16),
            pltpu.VMEM((seg, HEAD_DIM), BF16),
            pltpu.VMEM((nchunk, HEAD_DIM, HEAD_DIM), F32),
            pltpu.VMEM((nchunk, HEAD_DIM, HEAD_DIM), F32),
            pltpu.VMEM((nchunk, HEAD_DIM, HEAD_DIM), BF16),
            pltpu.VMEM((nchunk, HEAD_DIM, HEAD_DIM), BF16),
        ],
        compiler_params=_cp(("arbitrary", "arbitrary")),
        name="retention",
    )(log_g, proj, proj, proj, proj, cos2, sin2, gn_gain)


def _split3(x):
    hi = x.astype(BF16)
    r1 = x - hi.astype(F32)
    mid = r1.astype(BF16)
    lo = (r1 - mid.astype(F32)).astype(BF16)
    return hi, mid, lo


def _hgrn_kernel(ff_ref, fb_ref, v_ref, q_ref, g_ref, lb_ref, gn_ref, o_ref,
                 oacc, qf_s, qb_s, uf_s, ub_s, df_s, db_s, sf_s, sb_s, *, c, nchunk, nctx):
    g = 4 if nchunk % 4 == 0 else (2 if nchunk % 2 == 0 else 1)
    tc = g * c
    lb = lb_ref[...]
    one_m_lb = 1.0 - lb
    n_i = lax.broadcasted_iota(jnp.int32, (tc, tc), 0)
    m_i = lax.broadcasted_iota(jnp.int32, (tc, tc), 1)
    shift = c.bit_length() - 1
    same = jnp.right_shift(n_i, shift) == jnp.right_shift(m_i, shift)
    lower = same & (n_i >= m_i)
    upper = same & (m_i >= n_i)
    tri_l = lower.astype(BF16)
    tri_u = upper.astype(BF16)
    mid = c // 2

    def csum(tri, x):
        hi, md, lo = _split3(x)
        return _dot(tri, hi) + _dot(tri, md) + _dot(tri, lo)

    def chunk_row(x, row):
        return jnp.concatenate(
            [jnp.broadcast_to(x[j * c + row:j * c + row + 1, :], (c, HEAD_DIM)) for j in range(g)], axis=0)

    def phase_a(ti, carry):
        r0 = pl.multiple_of(ti * tc, tc)
        rows = pl.ds(r0, tc)
        qr = q_ref[rows, :]
        q = qr * _sigmoid(qr)
        vb = v_ref[rows, :].astype(BF16)
        frf = ff_ref[rows, :]
        frb = fb_ref[rows, :]
        kf = one_m_lb * _sigmoid(-frf)
        kb = one_m_lb * _sigmoid(-frb)
        lff = jnp.log(lb + one_m_lb * _sigmoid(frf))
        lfb = jnp.log(lb + one_m_lb * _sigmoid(frb))
        cum_f = csum(tri_l, lff)
        cum_b = csum(tri_u, lfb)
        an_f = chunk_row(cum_f, mid)
        an_b = chunk_row(cum_b, mid)
        a_f = _dot_nt((q * jnp.exp(cum_f - an_f)).astype(BF16), (kf * jnp.exp(an_f - cum_f)).astype(BF16))
        a_b = _dot_nt((q * jnp.exp(cum_b - an_b)).astype(BF16), (kb * jnp.exp(an_b - cum_b)).astype(BF16))
        a = jnp.where(lower, a_f, 0.0) + jnp.where(upper, a_b, 0.0)
        oacc[rows, :] = _dot(a.astype(BF16), vb)
        last_f = chunk_row(cum_f, c - 1)
        last_b = chunk_row(cum_b, 0)
        qf_s[rows, :] = (q * jnp.exp(cum_f)).astype(BF16)
        qb_s[rows, :] = (q * jnp.exp(cum_b)).astype(BF16)
        khf = (kf * jnp.exp(last_f - cum_f)).astype(BF16)
        khb = (kb * jnp.exp(last_b - cum_b)).astype(BF16)
        for j in range(g):
            sl = slice(j * c, (j + 1) * c)
            ci = ti * g + j
            uf_s[ci] = _dot_tn(vb[sl], khf[sl])
            ub_s[ci] = _dot_tn(vb[sl], khb[sl])
            df_s[ci] = jnp.exp(last_f[j * c:j * c + 1, :])
            db_s[ci] = jnp.exp(last_b[j * c:j * c + 1, :])
        return carry

    lax.fori_loop(0, nchunk // g, phase_a, 0)
    unroll = g

    def scan_f(ci, s):
        sf_s[ci] = s.astype(BF16)
        return s * df_s[ci] + uf_s[ci]

    lax.fori_loop(0, nchunk, scan_f, jnp.zeros((HEAD_DIM, HEAD_DIM), F32))

    def scan_b(i, s):
        ci = jnp.where(i < nctx, nctx - 1 - i, nchunk - 1 - (i - nctx))
        sb_s[ci] = s.astype(BF16)
        return s * db_s[ci] + ub_s[ci]

    lax.fori_loop(0, nchunk, scan_b, jnp.zeros((HEAD_DIM, HEAD_DIM), F32))

    def phase_c(ci, carry):
        r0 = pl.multiple_of(ci * c, c)
        rows = pl.ds(r0, c)
        o = oacc[rows, :] + _dot_nt(qf_s[rows, :], sf_s[ci]) + _dot_nt(qb_s[rows, :], sb_s[ci])
        on = _rms_rows(o)
        g = g_ref[rows, :]
        o_ref[rows, :] = (on * gn_ref[...] * (g * _sigmoid(g))).astype(o_ref.dtype)
        return carry

    lax.fori_loop(0, nchunk, phase_c, 0, unroll=unroll)


def _hgrn(proj, lb, gn_gain, bsz, seg, lc):
    t = proj.shape[0]
    c = GLA_CHUNK
    nchunk = seg // c
    nctx = lc // c
    kern = functools.partial(_hgrn_kernel, c=c, nchunk=nchunk, nctx=nctx)
    col = lambda c0: pl.BlockSpec((seg, HEAD_DIM), lambda b, h: (b, c0 + h))
    vec = pl.BlockSpec((1, HEAD_DIM), lambda b, h: (0, h))
    return pl.pallas_call(
        kern,
        grid=(bsz, C_HEADS),
        in_specs=[col(0), col(C_HEADS), col(2 * C_HEADS), col(3 * C_HEADS), col(4 * C_HEADS), vec, vec],
        out_specs=pl.BlockSpec((seg, HEAD_DIM), lambda b, h: (b, h)),
        out_shape=jax.ShapeDtypeStruct((t, C_HEADS * HEAD_DIM), BF16),
        scratch_shapes=[
            pltpu.VMEM((seg, HEAD_DIM), F32),
            pltpu.VMEM((seg, HEAD_DIM), BF16),
            pltpu.VMEM((seg, HEAD_DIM), BF16),
            pltpu.VMEM((nchunk, HEAD_DIM, HEAD_DIM), F32),
            pltpu.VMEM((nchunk, HEAD_DIM, HEAD_DIM), F32),
            pltpu.VMEM((nchunk, 1, HEAD_DIM), F32),
            pltpu.VMEM((nchunk, 1, HEAD_DIM), F32),
            pltpu.VMEM((nchunk, HEAD_DIM, HEAD_DIM), BF16),
            pltpu.VMEM((nchunk, HEAD_DIM, HEAD_DIM), BF16),
        ],
        compiler_params=_cp(("arbitrary", "arbitrary")),
        name="hgrn2",
    )(proj, proj, proj, proj, proj, lb, gn_gain)


def _out_kernel(*refs, n_in):
    xs = refs[:n_in]
    ws = refs[n_in:2 * n_in]
    z_ref, gate_ref, gain_ref, sh_ref, sc_ref, wr_ref, br_ref = refs[2 * n_in:2 * n_in + 7]
    z_out, h_out, lg_out = refs[2 * n_in + 7:]
    o = _dot(xs[0][...], ws[0][...])
    for x_ref, w_ref in zip(xs[1:], ws[1:]):
        o = o + _dot(x_ref[...], w_ref[...])
    z1 = z_ref[...] + gate_ref[...] * o
    z_out[...] = z1
    h = _rms_rows(z1) * gain_ref[...]
    h = h * (1.0 + sc_ref[...]) + sh_ref[...]
    hb = h.astype(BF16)
    h_out[...] = hb
    lg_out[...] = _dot(hb, wr_ref[...]) + br_ref[...]


def _out_proj(xs, ws, z, mods, gain, wr, br, bsz, nseg, off, rb):
    d = z.shape[1]
    nout = nseg - off
    n_in = len(xs)
    npad = wr.shape[1]

    def rin(b, j):
        return (b * nseg + off + j, 0)

    def rout(b, j):
        return (b * nout + j, 0)

    def mod(which):
        return pl.BlockSpec((None, None, 1, d),
                            lambda b, j: (jnp.where(j + off == 0, bsz, b), which, 0, 0))

    in_specs = [pl.BlockSpec((rb, x.shape[1]), rin) for x in xs]
    in_specs += [pl.BlockSpec(w.shape, lambda b, j: (0, 0)) for w in ws]
    in_specs += [
        pl.BlockSpec((rb, d), rin),
        mod(2),
        pl.BlockSpec((1, d), lambda b, j: (0, 0)),
        mod(3), mod(4),
        pl.BlockSpec(wr.shape, lambda b, j: (0, 0)),
        pl.BlockSpec((1, npad), lambda b, j: (0, 0)),
    ]
    tm = bsz * nout * rb
    return pl.pallas_call(
        functools.partial(_out_kernel, n_in=n_in),
        grid=(bsz, nout),
        in_specs=in_specs,
        out_specs=[pl.BlockSpec((rb, d), rout), pl.BlockSpec((rb, d), rout), pl.BlockSpec((rb, npad), rout)],
        out_shape=[jax.ShapeDtypeStruct((tm, d), F32), jax.ShapeDtypeStruct((tm, d), BF16),
                   jax.ShapeDtypeStruct((tm, npad), F32)],
        compiler_params=_cp(("arbitrary", "arbitrary")),
        name="out_proj",
    )(*xs, *ws, z, mods, gain, mods, mods, wr, br)


def _topk_kernel(lg_ref, idx_ref, gate_ref):
    l = lg_ref[...]
    lane = lax.broadcasted_iota(jnp.int32, l.shape, 1)
    vals, idxs = [], []
    for _ in range(TOP_K):
        m = jnp.max(l, axis=-1, keepdims=True)
        i = jnp.min(jnp.where(l == m, lane, LANES), axis=-1, keepdims=True)
        vals.append(m)
        idxs.append(i)
        l = jnp.where(lane == i, -jnp.inf, l)
    es = [jnp.exp(v - vals[0]) for v in vals]
    den = es[0] + es[1] + es[2] + es[3]
    io = jnp.zeros(l.shape, jnp.int32)
    go = jnp.zeros(l.shape, F32)
    for k in range(TOP_K):
        io = jnp.where(lane == k, idxs[k], io)
        go = jnp.where(lane == k, es[k] / den, go)
    idx_ref[...] = io
    gate_ref[...] = go


def _topk(logits, tm=512):
    t, n = logits.shape
    tm = _tile(t, tm)
    return pl.pallas_call(
        _topk_kernel,
        grid=(t // tm,),
        in_specs=[pl.BlockSpec((tm, n), lambda i: (i, 0))],
        out_specs=[pl.BlockSpec((tm, n), lambda i: (i, 0)), pl.BlockSpec((tm, n), lambda i: (i, 0))],
        out_shape=[jax.ShapeDtypeStruct((t, n), jnp.int32), jax.ShapeDtypeStruct((t, n), F32)],
        compiler_params=_cp(("arbitrary",)),
        name="router_topk",
    )(logits)


def _gather_kernel(idx0_ref, idxn_ref, src_ref, out_ref, buf, gsem, osem, *, chunk):
    i = pl.program_id(0)
    n = pl.num_programs(0)
    slot = i % 2
    other = 1 - slot

    def out_copy(step, s):
        return pltpu.make_async_copy(buf.at[s], out_ref.at[pl.ds(step * chunk, chunk)], osem.at[s])

    def issue_rows(idx_ref, s):
        def body(p, carry):
            r = 2 * p
            pltpu.make_async_copy(src_ref.at[idx_ref[0, r]], buf.at[s, r], gsem.at[s]).start(priority=0)
            pltpu.make_async_copy(src_ref.at[idx_ref[0, r + 1]], buf.at[s, r + 1], gsem.at[s]).start(priority=1)
            return carry

        lax.fori_loop(0, chunk // 2, body, 0, unroll=4)

    @pl.when(i == 0)
    def _first():
        issue_rows(idx0_ref, 0)

    @pl.when(i + 1 < n)
    def _next():
        @pl.when(i >= 1)
        def _free_slot():
            out_copy(i - 1, other).wait()

        issue_rows(idxn_ref, other)

    pltpu.make_async_copy(src_ref.at[pl.ds(0, chunk)], buf.at[slot], gsem.at[slot]).wait()
    out_copy(i, slot).start()

    @pl.when(i == n - 1)
    def _flush():
        out_copy(i, slot).wait()

        @pl.when(i >= 1)
        def _prev():
            out_copy(i - 1, other).wait()


def _gather_rows(src, idx, chunk=GATHER_CHUNK):
    r = idx.shape[0]
    d = src.shape[1:]
    nchunks = r // chunk
    assert src.shape[0] >= chunk
    idx3 = idx.reshape(nchunks, 1, chunk)
    return pl.pallas_call(
        functools.partial(_gather_kernel, chunk=chunk),
        grid=(nchunks,),
        in_specs=[
            pl.BlockSpec((None, 1, chunk), lambda i: (0, 0, 0), memory_space=pltpu.SMEM),
            pl.BlockSpec((None, 1, chunk), lambda i: (jnp.minimum(i + 1, nchunks - 1), 0, 0),
                         memory_space=pltpu.SMEM),
            pl.BlockSpec(memory_space=pl.ANY),
        ],
        out_specs=pl.BlockSpec(memory_space=pl.ANY),
        out_shape=jax.ShapeDtypeStruct((r,) + d, src.dtype),
        scratch_shapes=[pltpu.VMEM((2, chunk) + d, src.dtype), pltpu.SemaphoreType.DMA((2,)),
                        pltpu.SemaphoreType.DMA((2,))],
        compiler_params=_cp(("arbitrary",)),
        name="row_gather",
    )(idx3, idx3, src)


def _scatter_kernel(dest_ref, src_ref, init_ref, out_ref, sem, *, chunk, fan):
    del init_ref

    def body(t, carry):
        for k in range(fan):
            pltpu.make_async_copy(src_ref.at[t], out_ref.at[dest_ref[0, t * fan + k]], sem).start(priority=k % 2)
        return carry

    lax.fori_loop(0, chunk, body, 0, unroll=2)
    for _ in range(fan):
        pltpu.make_async_copy(src_ref, out_ref.at[pl.ds(0, chunk)], sem).wait()


def _scatter_rows(src, dest, n_out, fan, chunk=GATHER_CHUNK):
    s = src.shape[0]
    d = src.shape[1:]
    chunk = _tile(s, chunk)
    nchunks = s // chunk
    assert n_out >= chunk
    return pl.pallas_call(
        functools.partial(_scatter_kernel, chunk=chunk, fan=fan),
        grid=(nchunks,),
        in_specs=[
            pl.BlockSpec((None, 1, chunk * fan), lambda i: (i, 0, 0), memory_space=pltpu.SMEM),
            pl.BlockSpec((chunk,) + d, lambda i: (i, 0, 0)),
            pl.BlockSpec(memory_space=pl.ANY),
        ],
        out_specs=pl.BlockSpec(memory_space=pl.ANY),
        out_shape=jax.ShapeDtypeStruct((n_out,) + d, src.dtype),
        scratch_shapes=[pltpu.SemaphoreType.DMA],
        input_output_aliases={2: 0},
        compiler_params=_cp(("arbitrary",)),
        name="row_scatter",
    )(dest.reshape(nchunks, 1, chunk * fan), src, jnp.zeros((n_out,) + d, src.dtype))


def _gmm1_kernel(be_ref, first_ref, nused_ref, x_ref, w_ref, b_ref, o_ref, wsc, hsc):
    i = pl.program_id(1)
    nu = nused_ref[0]
    tn = w_ref.shape[1]

    def matmul(slot):
        hsc[slot] = _dot(x_ref[...], wsc[...]) + b_ref[...]

    def activate(slot):
        w2 = 2 * LANES
        rsel = lax.broadcasted_iota(jnp.int32, (w2, LANES), 0)
        csel = lax.broadcasted_iota(jnp.int32, (w2, LANES), 1)
        sel = (rsel == 2 * csel).astype(BF16)
        for s in range(tn // w2):
            parts = []
            for u in range(2):
                lo = s * w2 + u * LANES
                hid = hsc[slot, :, lo:lo + LANES]
                glu = jnp.minimum(hid, SWIGLU_LIMIT)
                glu = glu * _sigmoid(SWIGLU_ALPHA * glu)
                lin = jnp.clip(hid, -SWIGLU_LIMIT, SWIGLU_LIMIT) + 1.0
                parts.append((glu * pltpu.roll(lin, LANES - 1, 1)).astype(BF16))
            prod = jnp.concatenate(parts, axis=1)
            o_ref[:, s * LANES:(s + 1) * LANES] = _dot(prod, sel).astype(o_ref.dtype)

    @pl.when((i < nu) & (first_ref[jnp.minimum(i, nu - 1)] == 1))
    def _cast():
        wsc[...] = w_ref[...].astype(BF16)

    @pl.when((i == 0) & (nu > 0))
    def _head():
        matmul(0)

    for par in range(2):
        @pl.when((i >= 1) & (i < nu) & (i % 2 == par))
        def _steady():
            matmul(par)
            activate(1 - par)

        @pl.when((i >= 1) & (i == nu) & (i % 2 == par))
        def _tail():
            activate(1 - par)

    @pl.when(i > nu)
    def _unused():
        o_ref[...] = jnp.zeros(o_ref.shape, o_ref.dtype)


def _gmm2_kernel(be_ref, first_ref, nused_ref, x_ref, w_ref, b_ref, o_ref, wsc):
    i = pl.program_id(1)

    @pl.when(i < nused_ref[0])
    def _():
        @pl.when(first_ref[i] == 1)
        def _cast():
            wsc[...] = w_ref[...].astype(BF16)

        o_ref[...] = (_dot(x_ref[...], wsc[...]) + b_ref[...]).astype(o_ref.dtype)

    @pl.when(i >= nused_ref[0])
    def _unused():
        o_ref[...] = jnp.zeros(o_ref.shape, o_ref.dtype)


def _gmm(kernel, x, w, b, layer, be, first, nused, bm, tn, out_cols, out_dtype, name, skew=False):
    r, k = x.shape
    n = w.shape[3]
    tn = _tile(n, tn)
    nb = r // bm
    n_tiles = n // tn
    oc = out_cols // n_tiles
    lag = 1 if skew else 0

    def blk(i, nu):
        return jnp.minimum(i, nu[0] - 1)

    scratch = [pltpu.VMEM((k, tn), BF16)]
    if skew:
        scratch.append(pltpu.VMEM((2, bm, tn), F32))
    grid_spec = pltpu.PrefetchScalarGridSpec(
        num_scalar_prefetch=3,
        grid=(n_tiles, nb + lag),
        in_specs=[
            pl.BlockSpec((bm, k), lambda j, i, be, fi, nu: (blk(i, nu), 0)),
            pl.BlockSpec((None, None, k, tn), lambda j, i, be, fi, nu: (layer, be[blk(i, nu)], 0, j)),
            pl.BlockSpec((None, None, 1, tn), lambda j, i, be, fi, nu: (layer, be[blk(i, nu)], 0, j)),
        ],
        out_specs=pl.BlockSpec((bm, oc), lambda j, i, be, fi, nu: (jnp.maximum(i - lag, 0), j)),
        scratch_shapes=scratch,
    )
    return pl.pallas_call(
        kernel,
        grid_spec=grid_spec,
        out_shape=jax.ShapeDtypeStruct((r, out_cols), out_dtype),
        compiler_params=_cp(("arbitrary", "arbitrary")),
        name=name,
    )(be, first, nused, x, w, b.reshape(b.shape[0], b.shape[1], 1, n))


def _combine_kernel(z_ref, y0_ref, y1_ref, y2_ref, y3_ref, gt_ref, gate_ref, gain_ref, sh_ref, sc_ref,
                    *outs, final):
    gt = gt_ref[...]
    ffn = gt[:, 0:1] * y0_ref[...].astype(F32)
    for k, y_ref in enumerate((y1_ref, y2_ref, y3_ref), start=1):
        ffn = ffn + gt[:, k:k + 1] * y_ref[...].astype(F32)
    z2 = z_ref[...] + gate_ref[...] * ffn
    h = _rms_rows(z2) * gain_ref[...]
    if final:
        outs[0][...] = h
    else:
        outs[0][...] = z2
        outs[1][...] = (h * (1.0 + sc_ref[...]) + sh_ref[...]).astype(outs[1].dtype)


def _combine(z1, yt, gates, mods_cur, mods_next, gain, bsz, nblk, rb, final):
    t, d = z1.shape
    has_ctx = not final

    def midx(i):
        if has_ctx:
            return jnp.where(i % nblk == 0, bsz, i // nblk)
        return i // nblk

    row = lambda i: (i, 0)
    nrow = t // rb
    out_specs = [pl.BlockSpec((rb, d), row)]
    out_shape = [jax.ShapeDtypeStruct((t, d), F32)]
    if not final:
        out_specs.append(pl.BlockSpec((rb, d), row))
        out_shape.append(jax.ShapeDtypeStruct((t, d), BF16))
    y_specs = [pl.BlockSpec((rb, d), functools.partial(lambda i, k: (k * nrow + i, 0), k=k))
               for k in range(TOP_K)]
    return pl.pallas_call(
        functools.partial(_combine_kernel, final=final),
        grid=(nrow,),
        in_specs=[
            pl.BlockSpec((rb, d), row),
            *y_specs,
            pl.BlockSpec((rb, gates.shape[1]), row),
            pl.BlockSpec((None, None, 1, d), lambda i: (midx(i), 5, 0, 0)),
            pl.BlockSpec((1, d), lambda i: (0, 0)),
            pl.BlockSpec((None, None, 1, d), lambda i: (midx(i), 0, 0, 0)),
            pl.BlockSpec((None, None, 1, d), lambda i: (midx(i), 1, 0, 0)),
        ],
        out_specs=out_specs,
        out_shape=out_shape,
        compiler_params=_cp(("arbitrary",)),
        name="moe_combine",
    )(z1, yt, yt, yt, yt, gates, mods_cur, gain, mods_next, mods_next)


def _routing(idx, n_exp, bm, gchunk):
    tm = idx.shape[0]
    onehot = (idx[:, :, None] == jnp.arange(n_exp, dtype=jnp.int32)).astype(jnp.int32).sum(axis=1)
    csum = jnp.cumsum(onehot, axis=0)
    counts = csum[-1]
    rank = jnp.take_along_axis(csum - onehot, idx, axis=1)
    padded = (counts + bm - 1) // bm * bm
    pad_end = jnp.cumsum(padded)
    pad_start = pad_end - padded
    dest = pad_start[idx] + rank
    nb = (tm * TOP_K + n_exp * (bm - 1)) // bm
    nb = -(-(nb * bm) // gchunk) * gchunk // bm
    nused = (pad_end[-1] // bm).astype(jnp.int32).reshape(1)
    block_start = jnp.arange(nb, dtype=jnp.int32) * bm
    be = jnp.minimum((pad_end[None, :] <= block_start[:, None]).astype(jnp.int32).sum(axis=1), n_exp - 1)
    first = jnp.concatenate([jnp.ones((1,), jnp.int32), (be[1:] != be[:-1]).astype(jnp.int32)])
    dest = dest.astype(jnp.int32)
    dest_kmajor = dest.T.reshape(-1)
    return dest.reshape(-1), dest_kmajor, nb * bm, be, first, nused


def _moe(h2, logits, w1, b1, w2, b2, layer):
    tm, d = h2.shape
    n_exp = w1.shape[1]
    sub = d // LANES
    idx_p, gate_p = _topk(logits)
    idx = idx_p[:, :TOP_K]
    dest_tmajor, dest, n_rows, be, first, nused = _routing(idx, n_exp, MOE_BM, GATHER_CHUNK)
    xs = _scatter_rows(h2.reshape(tm, sub, LANES), dest_tmajor, n_rows, TOP_K).reshape(-1, d)
    act = _gmm(_gmm1_kernel, xs, w1, b1, layer, be, first, nused, MOE_BM, 1024, w1.shape[3] // 2, BF16, "moe_up",
               skew=True)
    y = _gmm(_gmm2_kernel, act, w2, b2, layer, be, first, nused, MOE_BM, 1024, w2.shape[3], BF16, "moe_down")
    yt = _gather_rows(y.reshape(-1, sub, LANES), dest).reshape(TOP_K * tm, d)
    return yt, gate_p


def _rope_tables(n_lat, lc):
    n_rows = n_lat // GRID_W
    row = jnp.repeat(jnp.arange(n_rows, dtype=F32), GRID_W)
    col = jnp.tile(jnp.arange(GRID_W, dtype=F32), n_rows)
    n_freq = HEAD_DIM // 4
    inv_freq = ROPE_THETA ** (-jnp.arange(n_freq, dtype=F32) / n_freq)
    ang = jnp.concatenate([row[:, None] * inv_freq, col[:, None] * inv_freq], axis=-1)
    cos, sin = jnp.cos(ang), jnp.sin(ang)
    cos2 = jnp.concatenate([cos, cos], axis=-1)
    sin2 = jnp.concatenate([-sin, sin], axis=-1)
    cos2 = jnp.concatenate([jnp.ones((lc, HEAD_DIM), F32), cos2], axis=0)
    sin2 = jnp.concatenate([jnp.zeros((lc, HEAD_DIM), F32), sin2], axis=0)
    return cos2, sin2


def kernel(x, c, ctx, c_ctx, ada_w, ada_b, norm_mix, norm_ffn, ab_w_in, ab_w_out, a_q_norm, a_k_norm, b_decay_exp, b_gn, c_w_in, c_w_out, c_lb, c_gn, router_w, router_b, exp_w1, exp_b1, exp_w2, exp_b2, norm_final):
    bsz, n_lat, d = x.shape
    lc = ctx.shape[1]
    depth = ada_w.shape[0]
    assert depth == 2 and n_lat % lc == 0 and bsz < MOD_ROWS and lc % GLA_CHUNK == 0
    seg = lc + n_lat
    nseg = seg // lc
    n_exp = router_w.shape[2]

    z = jnp.concatenate([ctx, x], axis=1).reshape(bsz * seg, d)

    cond = jnp.zeros((MOD_ROWS, d), F32).at[:bsz].set(c).at[bsz].set(c_ctx)
    mods = _ada(cond, ada_w, ada_b).reshape(depth, MOD_ROWS, N_MOD, 1, d)
    cos2, sin2 = _rope_tables(n_lat, lc)

    lb_soft = jax.nn.softmax(c_lb.astype(F32), axis=0)
    lower_bounds = jnp.cumsum(lb_soft, axis=0) - lb_soft[0]
    log_g = jnp.log1p(-jnp.exp2(-b_decay_exp[0].astype(F32)))

    wr = jnp.zeros((depth, d, LANES), BF16).at[:, :, :n_exp].set(router_w.astype(BF16))
    br = jnp.full((depth, 1, LANES), -1e30, F32).at[:, 0, :n_exp].set(router_b)

    hz = _norm_mod(z, norm_mix[0:1], mods[0], nseg, bsz, lc)
    proj = _matmul(hz, ab_w_in[0].astype(BF16), 1024, 512, name="in_proj0")
    att = _attention(proj, cos2, sin2, a_q_norm[0:1], a_k_norm[0:1], bsz, seg, lc)
    ret = _retention(proj, log_g, cos2, sin2, b_gn[0:1], bsz, seg, lc)
    w_out = ab_w_out[0].astype(BF16)
    aw = A_Q_HEADS * HEAD_DIM
    z1, h2, logits = _out_proj([att, ret], [w_out[:aw], w_out[aw:]], z, mods[0], norm_ffn[0:1],
                               wr[0], br[0], bsz, nseg, 0, lc)
    yt, gates = _moe(h2, logits, exp_w1, exp_b1, exp_w2, exp_b2, 0)
    z2, hz = _combine(z1, yt, gates, mods[0], mods[1], norm_mix[1:2], bsz, nseg, lc, final=False)

    proj = _matmul(hz, c_w_in[0].astype(BF16), 1024, 512, name="in_proj1")
    hg = _hgrn(proj, lower_bounds[1:2], c_gn[0:1], bsz, seg, lc)
    z1, h2, logits = _out_proj([hg], [c_w_out[0].astype(BF16)], z2, mods[1], norm_ffn[1:2],
                               wr[1], br[1], bsz, nseg, 1, lc)
    yt, gates = _moe(h2, logits, exp_w1, exp_b1, exp_w2, exp_b2, 1)
    (out,) = _combine(z1, yt, gates, mods[1], mods[1], norm_final.reshape(1, d), bsz, nseg - 1, lc, final=True)
    return out.reshape(bsz, n_lat, d)
```

```python
import functools

import jax
import jax.numpy as jnp
from jax import lax
from jax.experimental import pallas as pl
from jax.experimental.pallas import tpu as pltpu

HEAD_DIM = 128
GRID_W = 64
ROPE_THETA = 10000.0
NORM_EPS = 1e-6
N_MOD = 6
A_Q_HEADS = 8
A_KV_HEADS = 2
B_HEADS = 8
C_HEADS = 16
TOP_K = 4
SWIGLU_ALPHA = 1.702
SWIGLU_LIMIT = 7.0

LANES = 128
MOD_ROWS = 16
GLA_CHUNK = 64
MOE_BM = 512
GATHER_CHUNK = 512
VMEM_LIMIT = 56 * 1024 * 1024

LOG2E = 1.4426950408889634

F32 = jnp.float32
BF16 = jnp.bfloat16


def _cp(sem, vmem=VMEM_LIMIT):
    return pltpu.CompilerParams(dimension_semantics=sem, vmem_limit_bytes=vmem)


def _tile(n, pref):
    t = pref
    while n % t:
        t //= 2
    return t


def _dot(a, b):
    return jnp.dot(a, b, preferred_element_type=F32)


def _dot_nt(a, b):
    return lax.dot_general(a, b, (((1,), (1,)), ((), ())), preferred_element_type=F32)


def _dot_tn(a, b):
    return lax.dot_general(a, b, (((0,), (0,)), ((), ())), preferred_element_type=F32)


def _sigmoid(x):
    return 1.0 / (1.0 + jnp.exp(-x))


def _rope(t, cos2, sin2):
    return t * cos2 + pltpu.roll(t, HEAD_DIM // 2, 1) * sin2


def _rms_rows(x):
    return x * lax.rsqrt(jnp.mean(x * x, axis=-1, keepdims=True) + NORM_EPS)


def _ada_kernel(c_ref, w_ref, b_ref, o_ref):
    c = c_ref[...]
    s = (c * _sigmoid(c)).astype(BF16)
    o_ref[...] = _dot(s, w_ref[...].astype(BF16)) + b_ref[...]


def _ada(cond, ada_w, ada_b, tn=1024):
    depth, d, n = ada_w.shape
    return pl.pallas_call(
        _ada_kernel,
        grid=(depth, n // tn),
        in_specs=[
            pl.BlockSpec((MOD_ROWS, d), lambda l, j: (0, 0)),
            pl.BlockSpec((None, d, tn), lambda l, j: (l, 0, j)),
            pl.BlockSpec((None, 1, tn), lambda l, j: (l, 0, j)),
        ],
        out_specs=pl.BlockSpec((None, MOD_ROWS, tn), lambda l, j: (l, 0, j)),
        out_shape=jax.ShapeDtypeStruct((depth, MOD_ROWS, n), F32),
        compiler_params=_cp(("arbitrary", "arbitrary")),
        name="ada_mod",
    )(cond, ada_w, ada_b.reshape(depth, 1, n))


def _norm_mod_kernel(z_ref, g_ref, sh_ref, sc_ref, o_ref):
    y = _rms_rows(z_ref[...]) * g_ref[...]
    o_ref[...] = (y * (1.0 + sc_ref[...]) + sh_ref[...]).astype(o_ref.dtype)


def _norm_mod(z, gain, mods, nseg, bsz, rb):
    t, d = z.shape

    def midx(i):
        return jnp.where(i % nseg == 0, bsz, i // nseg)

    return pl.pallas_call(
        _norm_mod_kernel,
        grid=(t // rb,),
        in_specs=[
            pl.BlockSpec((rb, d), lambda i: (i, 0)),
            pl.BlockSpec((1, d), lambda i: (0, 0)),
            pl.BlockSpec((None, None, 1, d), lambda i: (midx(i), 0, 0, 0)),
            pl.BlockSpec((None, None, 1, d), lambda i: (midx(i), 1, 0, 0)),
        ],
        out_specs=pl.BlockSpec((rb, d), lambda i: (i, 0)),
        out_shape=jax.ShapeDtypeStruct((t, d), BF16),
        compiler_params=_cp(("arbitrary",)),
        name="norm_mod",
    )(z, gain, mods, mods)


def _mm_kernel(x_ref, w_ref, o_ref):
    o_ref[...] = _dot(x_ref[...], w_ref[...]).astype(o_ref.dtype)


def _matmul(x, w, tm, tn, out_dtype=F32, name="matmul"):
    m, k = x.shape
    n = w.shape[1]
    tm, tn = _tile(m, tm), _tile(n, tn)
    return pl.pallas_call(
        _mm_kernel,
        grid=(m // tm, n // tn),
        in_specs=[
            pl.BlockSpec((tm, k), lambda i, j: (i, 0)),
            pl.BlockSpec((k, tn), lambda i, j: (0, j)),
        ],
        out_specs=pl.BlockSpec((tm, tn), lambda i, j: (i, j)),
        out_shape=jax.ShapeDtypeStruct((m, n), out_dtype),
        compiler_params=_cp(("arbitrary", "arbitrary")),
        name=name,
    )(x, w)


def _attn_kernel(q_ref, k_ref, v_ref, cosq_ref, sinq_ref, cosk_ref, sink_ref,
                 qg_ref, kg_ref, o_ref, ks, vs, *, lc, group):
    qi = pl.program_id(2)
    rb = q_ref.shape[0]
    scale = HEAD_DIM ** -0.5

    @pl.when(qi == 0)
    def _prep():
        kn = _rms_rows(k_ref[...]) * kg_ref[...]
        ks[...] = _rope(kn, cosk_ref[...], sink_ref[...]).astype(BF16)
        vs[...] = v_ref[...].astype(BF16)

    def attend(k, v):
        for r in range(group):
            cols = slice(r * HEAD_DIM, (r + 1) * HEAD_DIM)
            qh = _rms_rows(q_ref[:, cols]) * qg_ref[...]
            qh = _rope(qh, cosq_ref[...], sinq_ref[...]).astype(BF16)
            s = _dot_nt(qh, k)
            m = jnp.max(s, axis=-1, keepdims=True)
            p = jnp.exp2((s - m) * (scale * LOG2E))
            l = jnp.sum(p, axis=-1, keepdims=True)
            o_ref[:, cols] = (_dot(p.astype(BF16), v) / l).astype(o_ref.dtype)

    @pl.when(qi == 0)
    def _ctx():
        attend(ks[0:lc], vs[0:lc])

    @pl.when(qi > 0)
    def _lat():
        attend(ks[...], vs[...])


def _attention(proj, cos2, sin2, q_gain, k_gain, bsz, seg, lc):
    t = proj.shape[0]
    nseg = seg // lc
    group = A_Q_HEADS // A_KV_HEADS
    kv_w = A_KV_HEADS
    q_col0 = (2 * kv_w + 2 * B_HEADS) // group
    kern = functools.partial(_attn_kernel, lc=lc, group=group)
    return pl.pallas_call(
        kern,
        grid=(bsz, A_KV_HEADS, nseg),
        in_specs=[
            pl.BlockSpec((lc, group * HEAD_DIM), lambda b, g, i: (b * nseg + i, q_col0 + g)),
            pl.BlockSpec((seg, HEAD_DIM), lambda b, g, i: (b, g)),
            pl.BlockSpec((seg, HEAD_DIM), lambda b, g, i: (b, kv_w + g)),
            pl.BlockSpec((lc, HEAD_DIM), lambda b, g, i: (i, 0)),
            pl.BlockSpec((lc, HEAD_DIM), lambda b, g, i: (i, 0)),
            pl.BlockSpec((seg, HEAD_DIM), lambda b, g, i: (0, 0)),
            pl.BlockSpec((seg, HEAD_DIM), lambda b, g, i: (0, 0)),
            pl.BlockSpec((1, HEAD_DIM), lambda b, g, i: (0, 0)),
            pl.BlockSpec((1, HEAD_DIM), lambda b, g, i: (0, 0)),
        ],
        out_specs=pl.BlockSpec((lc, group * HEAD_DIM), lambda b, g, i: (b * nseg + i, g)),
        out_shape=jax.ShapeDtypeStruct((t, A_Q_HEADS * HEAD_DIM), BF16),
        scratch_shapes=[pltpu.VMEM((seg, HEAD_DIM), BF16), pltpu.VMEM((seg, HEAD_DIM), BF16)],
        compiler_params=_cp(("arbitrary", "arbitrary", "arbitrary")),
        name="gqa_attention",
    )(proj, proj, proj, cos2, sin2, cos2, sin2, q_gain, k_gain)


def _ret_kernel(lg_ref, q_ref, k_ref, v_ref, g_ref, cos_ref, sin_ref, gn_ref, o_ref,
                oacc, qf_s, qb_s, uf_s, ub_s, sf_s, sb_s, *, c, nchunk, nctx):
    h = pl.program_id(1)
    lgf = lg_ref[0, h]
    lgb = lg_ref[1, h]
    cf = float(c)
    n_i = lax.broadcasted_iota(jnp.int32, (c, 1), 0).astype(F32)
    m_i = lax.broadcasted_iota(jnp.int32, (1, c), 1).astype(F32)
    diff = n_i - m_i
    dmat = (jnp.where(diff >= 0, jnp.exp(lgf * jnp.maximum(diff, 0.0)), 0.0)
            + jnp.where(diff <= 0, jnp.exp(lgb * jnp.maximum(-diff, 0.0)), 0.0))
    qdf = jnp.exp(lgf * (n_i + 1.0))
    qdb = jnp.exp(lgb * (cf - n_i))
    kdf = jnp.exp(lgf * (cf - 1.0 - n_i))
    kdb = jnp.exp(lgb * n_i)
    cdf = jnp.exp(lgf * cf)
    cdb = jnp.exp(lgb * cf)
    kscale = HEAD_DIM ** -0.5

    def phase_a(ci, carry):
        r0 = pl.multiple_of(ci * c, c)
        rows = pl.ds(r0, c)
        cos2 = cos_ref[rows, :]
        sin2 = sin_ref[rows, :]
        q = _rope(q_ref[rows, :], cos2, sin2)
        k = _rope(k_ref[rows, :], cos2, sin2) * kscale
        vb = v_ref[rows, :].astype(BF16)
        s = _dot_nt(q.astype(BF16), k.astype(BF16)) * dmat
        oacc[rows, :] = _dot(s.astype(BF16), vb)
        qf_s[rows, :] = (q * qdf).astype(BF16)
        qb_s[rows, :] = (q * qdb).astype(BF16)
        uf_s[ci] = _dot_tn((k * kdf).astype(BF16), vb)
        ub_s[ci] = _dot_tn((k * kdb).astype(BF16), vb)
        return carry

    unroll = 3 if nchunk % 3 == 0 else 1
    lax.fori_loop(0, nchunk, phase_a, 0, unroll=unroll)

    def scan_f(ci, s):
        sf_s[ci] = s.astype(BF16)
        return s * cdf + uf_s[ci]

    lax.fori_loop(0, nchunk, scan_f, jnp.zeros((HEAD_DIM, HEAD_DIM), F32))

    def scan_b(i, s):
        ci = jnp.where(i < nctx, nctx - 1 - i, nchunk - 1 - (i - nctx))
        sb_s[ci] = s.astype(BF16)
        return s * cdb + ub_s[ci]

    lax.fori_loop(0, nchunk, scan_b, jnp.zeros((HEAD_DIM, HEAD_DIM), F32))

    def phase_c(ci, carry):
        r0 = pl.multiple_of(ci * c, c)
        rows = pl.ds(r0, c)
        o = oacc[rows, :] + _dot(qf_s[rows, :], sf_s[ci]) + _dot(qb_s[rows, :], sb_s[ci])
        mu = jnp.mean(o, axis=-1, keepdims=True)
        d = o - mu
        on = d * lax.rsqrt(jnp.mean(d * d, axis=-1, keepdims=True) + NORM_EPS)
        g = g_ref[rows, :]
        o_ref[rows, :] = (on * gn_ref[...] * (g * _sigmoid(g))).astype(o_ref.dtype)
        return carry

    lax.fori_loop(0, nchunk, phase_c, 0, unroll=unroll)


def _retention(proj, log_g, cos2, sin2, gn_gain, bsz, seg, lc):
    t = proj.shape[0]
    nchunk = seg // lc
    kvw = 2 * A_KV_HEADS
    k0, v0 = kvw, kvw + B_HEADS
    q0 = kvw + 2 * B_HEADS + A_Q_HEADS
    g0 = q0 + B_HEADS
    kern = functools.partial(_ret_kernel, c=lc, nchunk=nchunk, nctx=1)
    col = lambda c0: pl.BlockSpec((seg, HEAD_DIM), lambda b, h: (b, c0 + h))
    return pl.pallas_call(
        kern,
        grid=(bsz, B_HEADS),
        in_specs=[
            pl.BlockSpec(memory_space=pltpu.SMEM),
            col(q0), col(k0), col(v0), col(g0),
            pl.BlockSpec((seg, HEAD_DIM), lambda b, h: (0, 0)),
            pl.BlockSpec((seg, HEAD_DIM), lambda b, h: (0, 0)),
            pl.BlockSpec((1, HEAD_DIM), lambda b, h: (0, h)),
        ],
        out_specs=pl.BlockSpec((seg, HEAD_DIM), lambda b, h: (b, h)),
        out_shape=jax.ShapeDtypeStruct((t, B_HEADS * HEAD_DIM), BF16),
        scratch_shapes=[
            pltpu.VMEM((seg, HEAD_DIM), F32),
            pltpu.VMEM((seg, HEAD_DIM), BF16),
            pltpu.VMEM((seg, HEAD_DIM), BF16),
            pltpu.VMEM((nchunk, HEAD_DIM, HEAD_DIM), F32),
            pltpu.VMEM((nchunk, HEAD_DIM, HEAD_DIM), F32),
            pltpu.VMEM((nchunk, HEAD_DIM, HEAD_DIM), BF16),
            pltpu.VMEM((nchunk, HEAD_DIM, HEAD_DIM), BF16),
        ],
        compiler_params=_cp(("arbitrary", "arbitrary")),
        name="retention",
    )(log_g, proj, proj, proj, proj, cos2, sin2, gn_gain)


def _split3(x):
    hi = x.astype(BF16)
    r1 = x - hi.astype(F32)
    mid = r1.astype(BF16)
    lo = (r1 - mid.astype(F32)).astype(BF16)
    return hi, mid, lo


def _hgrn_kernel(ff_ref, fb_ref, v_ref, q_ref, g_ref, lb_ref, gn_ref, o_ref,
                 oacc, qf_s, qb_s, uf_s, ub_s, df_s, db_s, sf_s, sb_s, *, c, nchunk, nctx):
    g = 4 if nchunk % 4 == 0 else (2 if nchunk % 2 == 0 else 1)
    tc = g * c
    lb = lb_ref[...]
    one_m_lb = 1.0 - lb
    n_i = lax.broadcasted_iota(jnp.int32, (tc, tc), 0)
    m_i = lax.broadcasted_iota(jnp.int32, (tc, tc), 1)
    shift = c.bit_length() - 1
    same = jnp.right_shift(n_i, shift) == jnp.right_shift(m_i, shift)
    lower = same & (n_i >= m_i)
    upper = same & (m_i >= n_i)
    tri_l = lower.astype(BF16)
    tri_u = upper.astype(BF16)
    mid = c // 2

    def csum(tri, x):
        hi, md, lo = _split3(x)
        return _dot(tri, hi) + _dot(tri, md) + _dot(tri, lo)

    def chunk_row(x, row):
        return jnp.concatenate(
            [jnp.broadcast_to(x[j * c + row:j * c + row + 1, :], (c, HEAD_DIM)) for j in range(g)], axis=0)

    def phase_a(ti, carry):
        r0 = pl.multiple_of(ti * tc, tc)
        rows = pl.ds(r0, tc)
        qr = q_ref[rows, :]
        q = qr * _sigmoid(qr)
        vb = v_ref[rows, :].astype(BF16)
        frf = ff_ref[rows, :]
        frb = fb_ref[rows, :]
        kf = one_m_lb * _sigmoid(-frf)
        kb = one_m_lb * _sigmoid(-frb)
        lff = jnp.log(lb + one_m_lb * _sigmoid(frf))
        lfb = jnp.log(lb + one_m_lb * _sigmoid(frb))
        cum_f = csum(tri_l, lff)
        cum_b = csum(tri_u, lfb)
        an_f = chunk_row(cum_f, mid)
        an_b = chunk_row(cum_b, mid)
        a_f = _dot_nt((q * jnp.exp(cum_f - an_f)).astype(BF16), (kf * jnp.exp(an_f - cum_f)).astype(BF16))
        a_b = _dot_nt((q * jnp.exp(cum_b - an_b)).astype(BF16), (kb * jnp.exp(an_b - cum_b)).astype(BF16))
        a = jnp.where(lower, a_f, 0.0) + jnp.where(upper, a_b, 0.0)
        oacc[rows, :] = _dot(a.astype(BF16), vb)
        last_f = chunk_row(cum_f, c - 1)
        last_b = chunk_row(cum_b, 0)
        qf_s[rows, :] = (q * jnp.exp(cum_f)).astype(BF16)
        qb_s[rows, :] = (q * jnp.exp(cum_b)).astype(BF16)
        khf = (kf * jnp.exp(last_f - cum_f)).astype(BF16)
        khb = (kb * jnp.exp(last_b - cum_b)).astype(BF16)
        for j in range(g):
            sl = slice(j * c, (j + 1) * c)
            ci = ti * g + j
            uf_s[ci] = _dot_tn(vb[sl], khf[sl])
            ub_s[ci] = _dot_tn(vb[sl], khb[sl])
            df_s[ci] = jnp.exp(last_f[j * c:j * c + 1, :])
            db_s[ci] = jnp.exp(last_b[j * c:j * c + 1, :])
        return carry

    lax.fori_loop(0, nchunk // g, phase_a, 0)
    unroll = g

    def scan_f(ci, s):
        sf_s[ci] = s.astype(BF16)
        return s * df_s[ci] + uf_s[ci]

    lax.fori_loop(0, nchunk, scan_f, jnp.zeros((HEAD_DIM, HEAD_DIM), F32))

    def scan_b(i, s):
        ci = jnp.where(i < nctx, nctx - 1 - i, nchunk - 1 - (i - nctx))
        sb_s[ci] = s.astype(BF16)
        return s * db_s[ci] + ub_s[ci]

    lax.fori_loop(0, nchunk, scan_b, jnp.zeros((HEAD_DIM, HEAD_DIM), F32))

    def phase_c(ci, carry):
        r0 = pl.multiple_of(ci * c, c)
        rows = pl.ds(r0, c)
        o = oacc[rows, :] + _dot_nt(qf_s[rows, :], sf_s[ci]) + _dot_nt(qb_s[rows, :], sb_s[ci])
        on = _rms_rows(o)
        g = g_ref[rows, :]
        o_ref[rows, :] = (on * gn_ref[...] * (g * _sigmoid(g))).astype(o_ref.dtype)
        return carry

    lax.fori_loop(0, nchunk, phase_c, 0, unroll=unroll)


def _hgrn(proj, lb, gn_gain, bsz, seg, lc):
    t = proj.shape[0]
    c = GLA_CHUNK
    nchunk = seg // c
    nctx = lc // c
    kern = functools.partial(_hgrn_kernel, c=c, nchunk=nchunk, nctx=nctx)
    col = lambda c0: pl.BlockSpec((seg, HEAD_DIM), lambda b, h: (b, c0 + h))
    vec = pl.BlockSpec((1, HEAD_DIM), lambda b, h: (0, h))
    return pl.pallas_call(
        kern,
        grid=(bsz, C_HEADS),
        in_specs=[col(0), col(C_HEADS), col(2 * C_HEADS), col(3 * C_HEADS), col(4 * C_HEADS), vec, vec],
        out_specs=pl.BlockSpec((seg, HEAD_DIM), lambda b, h: (b, h)),
        out_shape=jax.ShapeDtypeStruct((t, C_HEADS * HEAD_DIM), BF16),
        scratch_shapes=[
            pltpu.VMEM((seg, HEAD_DIM), F32),
            pltpu.VMEM((seg, HEAD_DIM), BF16),
            pltpu.VMEM((seg, HEAD_DIM), BF16),
            pltpu.VMEM((nchunk, HEAD_DIM, HEAD_DIM), F32),
            pltpu.VMEM((nchunk, HEAD_DIM, HEAD_DIM), F32),
            pltpu.VMEM((nchunk, 1, HEAD_DIM), F32),
            pltpu.VMEM((nchunk, 1, HEAD_DIM), F32),
            pltpu.VMEM((nchunk, HEAD_DIM, HEAD_DIM), BF16),
            pltpu.VMEM((nchunk, HEAD_DIM, HEAD_DIM), BF16),
        ],
        compiler_params=_cp(("arbitrary", "arbitrary")),
        name="hgrn2",
    )(proj, proj, proj, proj, proj, lb, gn_gain)


def _out_kernel(*refs, n_in):
    xs = refs[:n_in]
    ws = refs[n_in:2 * n_in]
    z_ref, gate_ref, gain_ref, sh_ref, sc_ref, wr_ref, br_ref = refs[2 * n_in:2 * n_in + 7]
    z_out, h_out, lg_out = refs[2 * n_in + 7:]
    o = _dot(xs[0][...], ws[0][...])
    for x_ref, w_ref in zip(xs[1:], ws[1:]):
        o = o + _dot(x_ref[...], w_ref[...])
    z1 = z_ref[...] + gate_ref[...] * o
    z_out[...] = z1
    h = _rms_rows(z1) * gain_ref[...]
    h = h * (1.0 + sc_ref[...]) + sh_ref[...]
    hb = h.astype(BF16)
    h_out[...] = hb
    lg_out[...] = _dot(hb, wr_ref[...]) + br_ref[...]


def _out_proj(xs, ws, z, mods, gain, wr, br, bsz, nseg, off, rb):
    d = z.shape[1]
    nout = nseg - off
    n_in = len(xs)
    npad = wr.shape[1]

    def rin(b, j):
        return (b * nseg + off + j, 0)

    def rout(b, j):
        return (b * nout + j, 0)

    def mod(which):
        return pl.BlockSpec((None, None, 1, d),
                            lambda b, j: (jnp.where(j + off == 0, bsz, b), which, 0, 0))

    in_specs = [pl.BlockSpec((rb, x.shape[1]), rin) for x in xs]
    in_specs += [pl.BlockSpec(w.shape, lambda b, j: (0, 0)) for w in ws]
    in_specs += [
        pl.BlockSpec((rb, d), rin),
        mod(2),
        pl.BlockSpec((1, d), lambda b, j: (0, 0)),
        mod(3), mod(4),
        pl.BlockSpec(wr.shape, lambda b, j: (0, 0)),
        pl.BlockSpec((1, npad), lambda b, j: (0, 0)),
    ]
    tm = bsz * nout * rb
    return pl.pallas_call(
        functools.partial(_out_kernel, n_in=n_in),
        grid=(bsz, nout),
        in_specs=in_specs,
        out_specs=[pl.BlockSpec((rb, d), rout), pl.BlockSpec((rb, d), rout), pl.BlockSpec((rb, npad), rout)],
        out_shape=[jax.ShapeDtypeStruct((tm, d), F32), jax.ShapeDtypeStruct((tm, d), BF16),
                   jax.ShapeDtypeStruct((tm, npad), F32)],
        compiler_params=_cp(("arbitrary", "arbitrary")),
        name="out_proj",
    )(*xs, *ws, z, mods, gain, mods, mods, wr, br)


def _topk_kernel(lg_ref, idx_ref, gate_ref):
    l = lg_ref[...]
    lane = lax.broadcasted_iota(jnp.int32, l.shape, 1)
    vals, idxs = [], []
    for _ in range(TOP_K):
        m = jnp.max(l, axis=-1, keepdims=True)
        i = jnp.min(jnp.where(l == m, lane, LANES), axis=-1, keepdims=True)
        vals.append(m)
        idxs.append(i)
        l = jnp.where(lane == i, -jnp.inf, l)
    es = [jnp.exp(v - vals[0]) for v in vals]
    den = es[0] + es[1] + es[2] + es[3]
    io = jnp.zeros(l.shape, jnp.int32)
    go = jnp.zeros(l.shape, F32)
    for k in range(TOP_K):
        io = jnp.where(lane == k, idxs[k], io)
        go = jnp.where(lane == k, es[k] / den, go)
    idx_ref[...] = io
    gate_ref[...] = go


def _topk(logits, tm=512):
    t, n = logits.shape
    tm = _tile(t, tm)
    return pl.pallas_call(
        _topk_kernel,
        grid=(t // tm,),
        in_specs=[pl.BlockSpec((tm, n), lambda i: (i, 0))],
        out_specs=[pl.BlockSpec((tm, n), lambda i: (i, 0)), pl.BlockSpec((tm, n), lambda i: (i, 0))],
        out_shape=[jax.ShapeDtypeStruct((t, n), jnp.int32), jax.ShapeDtypeStruct((t, n), F32)],
        compiler_params=_cp(("arbitrary",)),
        name="router_topk",
    )(logits)


def _gather_kernel(idx0_ref, idxn_ref, src_ref, out_ref, buf, gsem, osem, *, chunk):
    i = pl.program_id(0)
    n = pl.num_programs(0)
    slot = i % 2
    other = 1 - slot

    def out_copy(step, s):
        return pltpu.make_async_copy(buf.at[s], out_ref.at[pl.ds(step * chunk, chunk)], osem.at[s])

    def issue_rows(idx_ref, s):
        def body(p, carry):
            r = 2 * p
            pltpu.make_async_copy(src_ref.at[idx_ref[0, r]], buf.at[s, r], gsem.at[s]).start(priority=0)
            pltpu.make_async_copy(src_ref.at[idx_ref[0, r + 1]], buf.at[s, r + 1], gsem.at[s]).start(priority=1)
            return carry

        lax.fori_loop(0, chunk // 2, body, 0, unroll=4)

    @pl.when(i == 0)
    def _first():
        issue_rows(idx0_ref, 0)

    @pl.when(i + 1 < n)
    def _next():
        @pl.when(i >= 1)
        def _free_slot():
            out_copy(i - 1, other).wait()

        issue_rows(idxn_ref, other)

    pltpu.make_async_copy(src_ref.at[pl.ds(0, chunk)], buf.at[slot], gsem.at[slot]).wait()
    out_copy(i, slot).start()

    @pl.when(i == n - 1)
    def _flush():
        out_copy(i, slot).wait()

        @pl.when(i >= 1)
        def _prev():
            out_copy(i - 1, other).wait()


def _gather_rows(src, idx, chunk=GATHER_CHUNK):
    r = idx.shape[0]
    d = src.shape[1:]
    nchunks = r // chunk
    assert src.shape[0] >= chunk
    idx3 = idx.reshape(nchunks, 1, chunk)
    return pl.pallas_call(
        functools.partial(_gather_kernel, chunk=chunk),
        grid=(nchunks,),
        in_specs=[
            pl.BlockSpec((None, 1, chunk), lambda i: (0, 0, 0), memory_space=pltpu.SMEM),
            pl.BlockSpec((None, 1, chunk), lambda i: (jnp.minimum(i + 1, nchunks - 1), 0, 0),
                         memory_space=pltpu.SMEM),
            pl.BlockSpec(memory_space=pl.ANY),
        ],
        out_specs=pl.BlockSpec(memory_space=pl.ANY),
        out_shape=jax.ShapeDtypeStruct((r,) + d, src.dtype),
        scratch_shapes=[pltpu.VMEM((2, chunk) + d, src.dtype), pltpu.SemaphoreType.DMA((2,)),
                        pltpu.SemaphoreType.DMA((2,))],
        compiler_params=_cp(("arbitrary",)),
        name="row_gather",
    )(idx3, idx3, src)


def _scatter_kernel(dest_ref, src_ref, init_ref, out_ref, sem, *, chunk, fan):
    del init_ref

    def body(t, carry):
        for k in range(fan):
            pltpu.make_async_copy(src_ref.at[t], out_ref.at[dest_ref[0, t * fan + k]], sem).start(priority=k % 2)
        return carry

    lax.fori_loop(0, chunk, body, 0, unroll=2)
    for _ in range(fan):
        pltpu.make_async_copy(src_ref, out_ref.at[pl.ds(0, chunk)], sem).wait()


def _scatter_rows(src, dest, n_out, fan, chunk=GATHER_CHUNK):
    s = src.shape[0]
    d = src.shape[1:]
    chunk = _tile(s, chunk)
    nchunks = s // chunk
    assert n_out >= chunk
    return pl.pallas_call(
        functools.partial(_scatter_kernel, chunk=chunk, fan=fan),
        grid=(nchunks,),
        in_specs=[
            pl.BlockSpec((None, 1, chunk * fan), lambda i: (i, 0, 0), memory_space=pltpu.SMEM),
            pl.BlockSpec((chunk,) + d, lambda i: (i, 0, 0)),
            pl.BlockSpec(memory_space=pl.ANY),
        ],
        out_specs=pl.BlockSpec(memory_space=pl.ANY),
        out_shape=jax.ShapeDtypeStruct((n_out,) + d, src.dtype),
        scratch_shapes=[pltpu.SemaphoreType.DMA],
        input_output_aliases={2: 0},
        compiler_params=_cp(("arbitrary",)),
        name="row_scatter",
    )(dest.reshape(nchunks, 1, chunk * fan), src, jnp.zeros((n_out,) + d, src.dtype))


def _gmm1_kernel(be_ref, first_ref, nused_ref, x_ref, w_ref, b_ref, o_ref, wsc, hsc):
    i = pl.program_id(1)
    nu = nused_ref[0]
    tn = w_ref.shape[1]

    def matmul(slot):
        hsc[slot] = _dot(x_ref[...], wsc[...]) + b_ref[...]

    def activate(slot):
        w2 = 2 * LANES
        bm = o_ref.shape[0]
        lane = lax.broadcasted_iota(jnp.int32, (bm, LANES), 1)
        even = (2 * lane) % LANES
        for s in range(tn // w2):
            parts = []
            for u in range(2):
                lo = s * w2 + u * LANES
                hid = hsc[slot, :, lo:lo + LANES]
                glu = jnp.minimum(hid, SWIGLU_LIMIT)
                glu = glu * _sigmoid(SWIGLU_ALPHA * glu)
                lin = jnp.clip(hid, -SWIGLU_LIMIT, SWIGLU_LIMIT) + 1.0
                prod = glu * pltpu.roll(lin, LANES - 1, 1)
                parts.append(jnp.take_along_axis(prod, even, axis=1))
            o_ref[:, s * LANES:(s + 1) * LANES] = jnp.where(
                lane < LANES // 2, parts[0], parts[1]).astype(o_ref.dtype)

    @pl.when((i < nu) & (first_ref[jnp.minimum(i, nu - 1)] == 1))
    def _cast():
        wsc[...] = w_ref[...].astype(BF16)

    @pl.when((i == 0) & (nu > 0))
    def _head():
        matmul(0)

    for par in range(2):
        @pl.when((i >= 1) & (i < nu) & (i % 2 == par))
        def _steady():
            matmul(par)
            activate(1 - par)

        @pl.when((i >= 1) & (i == nu) & (i % 2 == par))
        def _tail():
            activate(1 - par)

    @pl.when(i > nu)
    def _unused():
        o_ref[...] = jnp.zeros(o_ref.shape, o_ref.dtype)


def _gmm2_kernel(be_ref, first_ref, nused_ref, x_ref, w_ref, b_ref, o_ref, wsc):
    i = pl.program_id(1)

    @pl.when(i < nused_ref[0])
    def _():
        @pl.when(first_ref[i] == 1)
        def _cast():
            wsc[...] = w_ref[...].astype(BF16)

        o_ref[...] = (_dot(x_ref[...], wsc[...]) + b_ref[...]).astype(o_ref.dtype)

    @pl.when(i >= nused_ref[0])
    def _unused():
        o_ref[...] = jnp.zeros(o_ref.shape, o_ref.dtype)


def _gmm(kernel, x, w, b, layer, be, first, nused, bm, tn, out_cols, out_dtype, name, skew=False):
    r, k = x.shape
    n = w.shape[3]
    tn = _tile(n, tn)
    nb = r // bm
    n_tiles = n // tn
    oc = out_cols // n_tiles
    lag = 1 if skew else 0

    def blk(i, nu):
        return jnp.minimum(i, nu[0] - 1)

    scratch = [pltpu.VMEM((k, tn), BF16)]
    if skew:
        scratch.append(pltpu.VMEM((2, bm, tn), F32))
    grid_spec = pltpu.PrefetchScalarGridSpec(
        num_scalar_prefetch=3,
        grid=(n_tiles, nb + lag),
        in_specs=[
            pl.BlockSpec((bm, k), lambda j, i, be, fi, nu: (blk(i, nu), 0)),
            pl.BlockSpec((None, None, k, tn), lambda j, i, be, fi, nu: (layer, be[blk(i, nu)], 0, j)),
            pl.BlockSpec((None, None, 1, tn), lambda j, i, be, fi, nu: (layer, be[blk(i, nu)], 0, j)),
        ],
        out_specs=pl.BlockSpec((bm, oc), lambda j, i, be, fi, nu: (jnp.maximum(i - lag, 0), j)),
        scratch_shapes=scratch,
    )
    return pl.pallas_call(
        kernel,
        grid_spec=grid_spec,
        out_shape=jax.ShapeDtypeStruct((r, out_cols), out_dtype),
        compiler_params=_cp(("arbitrary", "arbitrary")),
        name=name,
    )(be, first, nused, x, w, b.reshape(b.shape[0], b.shape[1], 1, n))


def _combine_kernel(z_ref, y0_ref, y1_ref, y2_ref, y3_ref, gt_ref, gate_ref, gain_ref, sh_ref, sc_ref,
                    *outs, final):
    gt = gt_ref[...]
    ffn = gt[:, 0:1] * y0_ref[...].astype(F32)
    for k, y_ref in enumerate((y1_ref, y2_ref, y3_ref), start=1):
        ffn = ffn + gt[:, k:k + 1] * y_ref[...].astype(F32)
    z2 = z_ref[...] + gate_ref[...] * ffn
    h = _rms_rows(z2) * gain_ref[...]
    if final:
        outs[0][...] = h
    else:
        outs[0][...] = z2
        outs[1][...] = (h * (1.0 + sc_ref[...]) + sh_ref[...]).astype(outs[1].dtype)


def _combine(z1, yt, gates, mods_cur, mods_next, gain, bsz, nblk, rb, final):
    t, d = z1.shape
    has_ctx = not final

    def midx(i):
        if has_ctx:
            return jnp.where(i % nblk == 0, bsz, i // nblk)
        return i // nblk

    row = lambda i: (i, 0)
    nrow = t // rb
    out_specs = [pl.BlockSpec((rb, d), row)]
    out_shape = [jax.ShapeDtypeStruct((t, d), F32)]
    if not final:
        out_specs.append(pl.BlockSpec((rb, d), row))
        out_shape.append(jax.ShapeDtypeStruct((t, d), BF16))
    y_specs = [pl.BlockSpec((rb, d), functools.partial(lambda i, k: (k * nrow + i, 0), k=k))
               for k in range(TOP_K)]
    return pl.pallas_call(
        functools.partial(_combine_kernel, final=final),
        grid=(nrow,),
        in_specs=[
            pl.BlockSpec((rb, d), row),
            *y_specs,
            pl.BlockSpec((rb, gates.shape[1]), row),
            pl.BlockSpec((None, None, 1, d), lambda i: (midx(i), 5, 0, 0)),
            pl.BlockSpec((1, d), lambda i: (0, 0)),
            pl.BlockSpec((None, None, 1, d), lambda i: (midx(i), 0, 0, 0)),
            pl.BlockSpec((None, None, 1, d), lambda i: (midx(i), 1, 0, 0)),
        ],
        out_specs=out_specs,
        out_shape=out_shape,
        compiler_params=_cp(("arbitrary",)),
        name="moe_combine",
    )(z1, yt, yt, yt, yt, gates, mods_cur, gain, mods_next, mods_next)


def _routing(idx, n_exp, bm, gchunk):
    tm = idx.shape[0]
    onehot = (idx[:, :, None] == jnp.arange(n_exp, dtype=jnp.int32)).astype(jnp.int32).sum(axis=1)
    csum = jnp.cumsum(onehot, axis=0)
    counts = csum[-1]
    rank = jnp.take_along_axis(csum - onehot, idx, axis=1)
    padded = (counts + bm - 1) // bm * bm
    pad_end = jnp.cumsum(padded)
    pad_start = pad_end - padded
    dest = pad_start[idx] + rank
    nb = (tm * TOP_K + n_exp * (bm - 1)) // bm
    nb = -(-(nb * bm) // gchunk) * gchunk // bm
    nused = (pad_end[-1] // bm).astype(jnp.int32).reshape(1)
    block_start = jnp.arange(nb, dtype=jnp.int32) * bm
    be = jnp.minimum((pad_end[None, :] <= block_start[:, None]).astype(jnp.int32).sum(axis=1), n_exp - 1)
    first = jnp.concatenate([jnp.ones((1,), jnp.int32), (be[1:] != be[:-1]).astype(jnp.int32)])
    dest = dest.astype(jnp.int32)
    dest_kmajor = dest.T.reshape(-1)
    return dest.reshape(-1), dest_kmajor, nb * bm, be, first, nused


def _moe(h2, logits, w1, b1, w2, b2, layer):
    tm, d = h2.shape
    n_exp = w1.shape[1]
    sub = d // LANES
    idx_p, gate_p = _topk(logits)
    idx = idx_p[:, :TOP_K]
    dest_tmajor, dest, n_rows, be, first, nused = _routing(idx, n_exp, MOE_BM, GATHER_CHUNK)
    xs = _scatter_rows(h2.reshape(tm, sub, LANES), dest_tmajor, n_rows, TOP_K).reshape(-1, d)
    act = _gmm(_gmm1_kernel, xs, w1, b1, layer, be, first, nused, MOE_BM, 1024, w1.shape[3] // 2, BF16, "moe_up",
               skew=True)
    y = _gmm(_gmm2_kernel, act, w2, b2, layer, be, first, nused, MOE_BM, 1024, w2.shape[3], BF16, "moe_down")
    yt = _gather_rows(y.reshape(-1, sub, LANES), dest).reshape(TOP_K * tm, d)
    return yt, gate_p


def _rope_tables(n_lat, lc):
    n_rows = n_lat // GRID_W
    row = jnp.repeat(jnp.arange(n_rows, dtype=F32), GRID_W)
    col = jnp.tile(jnp.arange(GRID_W, dtype=F32), n_rows)
    n_freq = HEAD_DIM // 4
    inv_freq = ROPE_THETA ** (-jnp.arange(n_freq, dtype=F32) / n_freq)
    ang = jnp.concatenate([row[:, None] * inv_freq, col[:, None] * inv_freq], axis=-1)
    cos, sin = jnp.cos(ang), jnp.sin(ang)
    cos2 = jnp.concatenate([cos, cos], axis=-1)
    sin2 = jnp.concatenate([-sin, sin], axis=-1)
    cos2 = jnp.concatenate([jnp.ones((lc, HEAD_DIM), F32), cos2], axis=0)
    sin2 = jnp.concatenate([jnp.zeros((lc, HEAD_DIM), F32), sin2], axis=0)
    return cos2, sin2


def kernel(x, c, ctx, c_ctx, ada_w, ada_b, norm_mix, norm_ffn, ab_w_in, ab_w_out, a_q_norm, a_k_norm, b_decay_exp, b_gn, c_w_in, c_w_out, c_lb, c_gn, router_w, router_b, exp_w1, exp_b1, exp_w2, exp_b2, norm_final):
    bsz, n_lat, d = x.shape
    lc = ctx.shape[1]
    depth = ada_w.shape[0]
    assert depth == 2 and n_lat % lc == 0 and bsz < MOD_ROWS and lc % GLA_CHUNK == 0
    seg = lc + n_lat
    nseg = seg // lc
    n_exp = router_w.shape[2]

    z = jnp.concatenate([ctx, x], axis=1).reshape(bsz * seg, d)

    cond = jnp.zeros((MOD_ROWS, d), F32).at[:bsz].set(c).at[bsz].set(c_ctx)
    mods = _ada(cond, ada_w, ada_b).reshape(depth, MOD_ROWS, N_MOD, 1, d)
    cos2, sin2 = _rope_tables(n_lat, lc)

    lb_soft = jax.nn.softmax(c_lb.astype(F32), axis=0)
    lower_bounds = jnp.cumsum(lb_soft, axis=0) - lb_soft[0]
    log_g = jnp.log1p(-jnp.exp2(-b_decay_exp[0].astype(F32)))

    wr = jnp.zeros((depth, d, LANES), BF16).at[:, :, :n_exp].set(router_w.astype(BF16))
    br = jnp.full((depth, 1, LANES), -1e30, F32).at[:, 0, :n_exp].set(router_b)

    hz = _norm_mod(z, norm_mix[0:1], mods[0], nseg, bsz, lc)
    proj = _matmul(hz, ab_w_in[0].astype(BF16), 1024, 512, name="in_proj0")
    att = _attention(proj, cos2, sin2, a_q_norm[0:1], a_k_norm[0:1], bsz, seg, lc)
    ret = _retention(proj, log_g, cos2, sin2, b_gn[0:1], bsz, seg, lc)
    w_out = ab_w_out[0].astype(BF16)
    aw = A_Q_HEADS * HEAD_DIM
    z1, h2, logits = _out_proj([att, ret], [w_out[:aw], w_out[aw:]], z, mods[0], norm_ffn[0:1],
                               wr[0], br[0], bsz, nseg, 0, lc)
    yt, gates = _moe(h2, logits, exp_w1, exp_b1, exp_w2, exp_b2, 0)
    z2, hz = _combine(z1, yt, gates, mods[0], mods[1], norm_mix[1:2], bsz, nseg, lc, final=False)

    proj = _matmul(hz, c_w_in[0].astype(BF16), 1024, 512, name="in_proj1")
    hg = _hgrn(proj, lower_bounds[1:2], c_gn[0:1], bsz, seg, lc)
    z1, h2, logits = _out_proj([hg], [c_w_out[0].astype(BF16)], z2, mods[1], norm_ffn[1:2],
                               wr[1], br[1], bsz, nseg, 1, lc)
    yt, gates = _moe(h2, logits, exp_w1, exp_b1, exp_w2, exp_b2, 1)
    (out,) = _combine(z1, yt, gates, mods[1], mods[1], norm_final.reshape(1, d), bsz, nseg - 1, lc, final=True)
    return out.reshape(bsz, n_lat, d)
```

```python
import functools

import jax
import jax.numpy as jnp
from jax import lax
from jax.experimental import pallas as pl
from jax.experimental.pallas import tpu as pltpu

HEAD_DIM = 128
GRID_W = 64
ROPE_THETA = 10000.0
NORM_EPS = 1e-6
N_MOD = 6
A_Q_HEADS = 8
A_KV_HEADS = 2
B_HEADS = 8
C_HEADS = 16
TOP_K = 4
SWIGLU_ALPHA = 1.702
SWIGLU_LIMIT = 7.0

LANES = 128
MOD_ROWS = 16
GLA_CHUNK = 64
MOE_BM = 512
GATHER_CHUNK = 512
VMEM_LIMIT = 56 * 1024 * 1024

LOG2E = 1.4426950408889634

F32 = jnp.float32
BF16 = jnp.bfloat16


def _cp(sem, vmem=VMEM_LIMIT):
    return pltpu.CompilerParams(dimension_semantics=sem, vmem_limit_bytes=vmem)


def _tile(n, pref):
    t = pref
    while n % t:
        t //= 2
    return t


def _dot(a, b):
    return jnp.dot(a, b, preferred_element_type=F32)


def _dot_nt(a, b):
    return lax.dot_general(a, b, (((1,), (1,)), ((), ())), preferred_element_type=F32)


def _dot_tn(a, b):
    return lax.dot_general(a, b, (((0,), (0,)), ((), ())), preferred_element_type=F32)


def _sigmoid(x):
    return 1.0 / (1.0 + jnp.exp(-x))


def _rope(t, cos2, sin2):
    return t * cos2 + pltpu.roll(t, HEAD_DIM // 2, 1) * sin2


def _rms_rows(x):
    return x * lax.rsqrt(jnp.mean(x * x, axis=-1, keepdims=True) + NORM_EPS)


def _ada_kernel(c_ref, w_ref, b_ref, o_ref):
    c = c_ref[...]
    s = (c * _sigmoid(c)).astype(BF16)
    o_ref[...] = _dot(s, w_ref[...].astype(BF16)) + b_ref[...]


def _ada(cond, ada_w, ada_b, tn=1024):
    depth, d, n = ada_w.shape
    return pl.pallas_call(
        _ada_kernel,
        grid=(depth, n // tn),
        in_specs=[
            pl.BlockSpec((MOD_ROWS, d), lambda l, j: (0, 0)),
            pl.BlockSpec((None, d, tn), lambda l, j: (l, 0, j)),
            pl.BlockSpec((None, 1, tn), lambda l, j: (l, 0, j)),
        ],
        out_specs=pl.BlockSpec((None, MOD_ROWS, tn), lambda l, j: (l, 0, j)),
        out_shape=jax.ShapeDtypeStruct((depth, MOD_ROWS, n), F32),
        compiler_params=_cp(("arbitrary", "arbitrary")),
        name="ada_mod",
    )(cond, ada_w, ada_b.reshape(depth, 1, n))


def _norm_mod_kernel(z_ref, g_ref, sh_ref, sc_ref, o_ref):
    y = _rms_rows(z_ref[...]) * g_ref[...]
    o_ref[...] = (y * (1.0 + sc_ref[...]) + sh_ref[...]).astype(o_ref.dtype)


def _norm_mod(z, gain, mods, nseg, bsz, rb):
    t, d = z.shape

    def midx(i):
        return jnp.where(i % nseg == 0, bsz, i // nseg)

    return pl.pallas_call(
        _norm_mod_kernel,
        grid=(t // rb,),
        in_specs=[
            pl.BlockSpec((rb, d), lambda i: (i, 0)),
            pl.BlockSpec((1, d), lambda i: (0, 0)),
            pl.BlockSpec((None, None, 1, d), lambda i: (midx(i), 0, 0, 0)),
            pl.BlockSpec((None, None, 1, d), lambda i: (midx(i), 1, 0, 0)),
        ],
        out_specs=pl.BlockSpec((rb, d), lambda i: (i, 0)),
        out_shape=jax.ShapeDtypeStruct((t, d), BF16),
        compiler_params=_cp(("arbitrary",)),
        name="norm_mod",
    )(z, gain, mods, mods)


def _mm_kernel(x_ref, w_ref, o_ref):
    o_ref[...] = _dot(x_ref[...], w_ref[...]).astype(o_ref.dtype)


def _matmul(x, w, tm, tn, out_dtype=F32, name="matmul"):
    m, k = x.shape
    n = w.shape[1]
    tm, tn = _tile(m, tm), _tile(n, tn)
    return pl.pallas_call(
        _mm_kernel,
        grid=(m // tm, n // tn),
        in_specs=[
            pl.BlockSpec((tm, k), lambda i, j: (i, 0)),
            pl.BlockSpec((k, tn), lambda i, j: (0, j)),
        ],
        out_specs=pl.BlockSpec((tm, tn), lambda i, j: (i, j)),
        out_shape=jax.ShapeDtypeStruct((m, n), out_dtype),
        compiler_params=_cp(("arbitrary", "arbitrary")),
        name=name,
    )(x, w)


def _attn_kernel(q_ref, k_ref, v_ref, cosq_ref, sinq_ref, cosk_ref, sink_ref,
                 qg_ref, kg_ref, o_ref, ks, vs, *, lc, group):
    qi = pl.program_id(2)
    rb = q_ref.shape[0]
    scale = HEAD_DIM ** -0.5

    @pl.when(qi == 0)
    def _prep():
        kn = _rms_rows(k_ref[...]) * kg_ref[...]
        ks[...] = _rope(kn, cosk_ref[...], sink_ref[...]).astype(BF16)
        vs[...] = v_ref[...].astype(BF16)

    def attend(k, v):
        for r in range(group):
            cols = slice(r * HEAD_DIM, (r + 1) * HEAD_DIM)
            qh = _rms_rows(q_ref[:, cols]) * qg_ref[...]
            qh = _rope(qh, cosq_ref[...], sinq_ref[...]).astype(BF16)
            s = _dot_nt(qh, k)
            m = jnp.max(s, axis=-1, keepdims=True)
            p = jnp.exp2((s - m) * (scale * LOG2E))
            l = jnp.sum(p, axis=-1, keepdims=True)
            o_ref[:, cols] = (_dot(p.astype(BF16), v) / l).astype(o_ref.dtype)

    @pl.when(qi == 0)
    def _ctx():
        attend(ks[0:lc], vs[0:lc])

    @pl.when(qi > 0)
    def _lat():
        attend(ks[...], vs[...])


def _attention(proj, cos2, sin2, q_gain, k_gain, bsz, seg, lc):
    t = proj.shape[0]
    nseg = seg // lc
    group = A_Q_HEADS // A_KV_HEADS
    kv_w = A_KV_HEADS
    q_col0 = (2 * kv_w + 2 * B_HEADS) // group
    kern = functools.partial(_attn_kernel, lc=lc, group=group)
    return pl.pallas_call(
        kern,
        grid=(bsz, A_KV_HEADS, nseg),
        in_specs=[
            pl.BlockSpec((lc, group * HEAD_DIM), lambda b, g, i: (b * nseg + i, q_col0 + g)),
            pl.BlockSpec((seg, HEAD_DIM), lambda b, g, i: (b, g)),
            pl.BlockSpec((seg, HEAD_DIM), lambda b, g, i: (b, kv_w + g)),
            pl.BlockSpec((lc, HEAD_DIM), lambda b, g, i: (i, 0)),
            pl.BlockSpec((lc, HEAD_DIM), lambda b, g, i: (i, 0)),
            pl.BlockSpec((seg, HEAD_DIM), lambda b, g, i: (0, 0)),
            pl.BlockSpec((seg, HEAD_DIM), lambda b, g, i: (0, 0)),
            pl.BlockSpec((1, HEAD_DIM), lambda b, g, i: (0, 0)),
            pl.BlockSpec((1, HEAD_DIM), lambda b, g, i: (0, 0)),
        ],
        out_specs=pl.BlockSpec((lc, group * HEAD_DIM), lambda b, g, i: (b * nseg + i, g)),
        out_shape=jax.ShapeDtypeStruct((t, A_Q_HEADS * HEAD_DIM), BF16),
        scratch_shapes=[pltpu.VMEM((seg, HEAD_DIM), BF16), pltpu.VMEM((seg, HEAD_DIM), BF16)],
        compiler_params=_cp(("arbitrary", "arbitrary", "arbitrary")),
        name="gqa_attention",
    )(proj, proj, proj, cos2, sin2, cos2, sin2, q_gain, k_gain)


def _ret_kernel(lg_ref, q_ref, k_ref, v_ref, g_ref, cos_ref, sin_ref, gn_ref, o_ref,
                oacc, qf_s, qb_s, uf_s, ub_s, sf_s, sb_s, *, c, nchunk, nctx):
    h = pl.program_id(1)
    lgf = lg_ref[0, h]
    lgb = lg_ref[1, h]
    cf = float(c)
    n_i = lax.broadcasted_iota(jnp.int32, (c, 1), 0).astype(F32)
    m_i = lax.broadcasted_iota(jnp.int32, (1, c), 1).astype(F32)
    diff = n_i - m_i
    dmat = (jnp.where(diff >= 0, jnp.exp(lgf * jnp.maximum(diff, 0.0)), 0.0)
            + jnp.where(diff <= 0, jnp.exp(lgb * jnp.maximum(-diff, 0.0)), 0.0))
    qdf = jnp.exp(lgf * (n_i + 1.0))
    qdb = jnp.exp(lgb * (cf - n_i))
    kdf = jnp.exp(lgf * (cf - 1.0 - n_i))
    kdb = jnp.exp(lgb * n_i)
    cdf = jnp.exp(lgf * cf)
    cdb = jnp.exp(lgb * cf)
    kscale = HEAD_DIM ** -0.5

    def phase_a(ci, carry):
        r0 = pl.multiple_of(ci * c, c)
        rows = pl.ds(r0, c)
        cos2 = cos_ref[rows, :]
        sin2 = sin_ref[rows, :]
        q = _rope(q_ref[rows, :], cos2, sin2)
        k = _rope(k_ref[rows, :], cos2, sin2) * kscale
        vb = v_ref[rows, :].astype(BF16)
        s = _dot_nt(q.astype(BF16), k.astype(BF16)) * dmat
        oacc[rows, :] = _dot(s.astype(BF16), vb)
        qf_s[rows, :] = (q * qdf).astype(BF16)
        qb_s[rows, :] = (q * qdb).astype(BF16)
        uf_s[ci] = _dot_tn((k * kdf).astype(BF16), vb)
        ub_s[ci] = _dot_tn((k * kdb).astype(BF16), vb)
        return carry

    unroll = 3 if nchunk % 3 == 0 else 1
    lax.fori_loop(0, nchunk, phase_a, 0, unroll=unroll)

    def scan_f(ci, s):
        sf_s[ci] = s.astype(BF16)
        return s * cdf + uf_s[ci]

    lax.fori_loop(0, nchunk, scan_f, jnp.zeros((HEAD_DIM, HEAD_DIM), F32))

    def scan_b(i, s):
        ci = jnp.where(i < nctx, nctx - 1 - i, nchunk - 1 - (i - nctx))
        sb_s[ci] = s.astype(BF16)
        return s * cdb + ub_s[ci]

    lax.fori_loop(0, nchunk, scan_b, jnp.zeros((HEAD_DIM, HEAD_DIM), F32))

    def phase_c(ci, carry):
        r0 = pl.multiple_of(ci * c, c)
        rows = pl.ds(r0, c)
        o = oacc[rows, :] + _dot(qf_s[rows, :], sf_s[ci]) + _dot(qb_s[rows, :], sb_s[ci])
        mu = jnp.mean(o, axis=-1, keepdims=True)
        d = o - mu
        on = d * lax.rsqrt(jnp.mean(d * d, axis=-1, keepdims=True) + NORM_EPS)
        g = g_ref[rows, :]
        o_ref[rows, :] = (on * gn_ref[...] * (g * _sigmoid(g))).astype(o_ref.dtype)
        return carry

    lax.fori_loop(0, nchunk, phase_c, 0, unroll=unroll)


def _retention(proj, log_g, cos2, sin2, gn_gain, bsz, seg, lc):
    t = proj.shape[0]
    nchunk = seg // lc
    kvw = 2 * A_KV_HEADS
    k0, v0 = kvw, kvw + B_HEADS
    q0 = kvw + 2 * B_HEADS + A_Q_HEADS
    g0 = q0 + B_HEADS
    kern = functools.partial(_ret_kernel, c=lc, nchunk=nchunk, nctx=1)
    col = lambda c0: pl.BlockSpec((seg, HEAD_DIM), lambda b, h: (b, c0 + h))
    return pl.pallas_call(
        kern,
        grid=(bsz, B_HEADS),
        in_specs=[
            pl.BlockSpec(memory_space=pltpu.SMEM),
            col(q0), col(k0), col(v0), col(g0),
            pl.BlockSpec((seg, HEAD_DIM), lambda b, h: (0, 0)),
            pl.BlockSpec((seg, HEAD_DIM), lambda b, h: (0, 0)),
            pl.BlockSpec((1, HEAD_DIM), lambda b, h: (0, h)),
        ],
        out_specs=pl.BlockSpec((seg, HEAD_DIM), lambda b, h: (b, h)),
        out_shape=jax.ShapeDtypeStruct((t, B_HEADS * HEAD_DIM), BF16),
        scratch_shapes=[
            pltpu.VMEM((seg, HEAD_DIM), F32),
            pltpu.VMEM((seg, HEAD_DIM), BF16),
            pltpu.VMEM((seg, HEAD_DIM), BF16),
            pltpu.VMEM((nchunk, HEAD_DIM, HEAD_DIM), F32),
            pltpu.VMEM((nchunk, HEAD_DIM, HEAD_DIM), F32),
            pltpu.VMEM((nchunk, HEAD_DIM, HEAD_DIM), BF16),
            pltpu.VMEM((nchunk, HEAD_DIM, HEAD_DIM), BF16),
        ],
        compiler_params=_cp(("arbitrary", "arbitrary")),
        name="retention",
    )(log_g, proj, proj, proj, proj, cos2, sin2, gn_gain)


def _split2(x):
    hi = x.astype(BF16)
    lo = (x - hi.astype(F32)).astype(BF16)
    return hi, lo


def _hgrn_kernel(ff_ref, fb_ref, v_ref, q_ref, g_ref, lb_ref, gn_ref, o_ref,
                 oacc, q_s, u_s, d_s, st_s, *, c, nchunk, nctx):
    g = 4 if nchunk % 4 == 0 else (2 if nchunk % 2 == 0 else 1)
    tc = g * c
    lb = lb_ref[...]
    one_m_lb = 1.0 - lb
    n_i = lax.broadcasted_iota(jnp.int32, (tc, tc), 0)
    m_i = lax.broadcasted_iota(jnp.int32, (tc, tc), 1)
    shift = c.bit_length() - 1
    same = jnp.right_shift(n_i, shift) == jnp.right_shift(m_i, shift)
    lower = same & (n_i >= m_i)
    upper = same & (m_i >= n_i)
    tri_l = lower.astype(BF16)
    tri_u = upper.astype(BF16)
    mid = c // 2

    def csum(tri, x):
        hi, lo = _split2(x)
        return _dot(tri, hi) + _dot(tri, lo)

    def chunk_row(x, row):
        return jnp.concatenate(
            [jnp.broadcast_to(x[j * c + row:j * c + row + 1, :], (c, HEAD_DIM)) for j in range(g)], axis=0)

    def phase_a(ti, carry):
        r0 = pl.multiple_of(ti * tc, tc)
        rows = pl.ds(r0, tc)
        qr = q_ref[rows, :]
        q = qr * _sigmoid(qr)
        vb = v_ref[rows, :].astype(BF16)
        frf = ff_ref[rows, :]
        frb = fb_ref[rows, :]
        sgf = _sigmoid(frf)
        sgb = _sigmoid(frb)
        kf = one_m_lb * (1.0 - sgf)
        kb = one_m_lb * (1.0 - sgb)
        lff = jnp.log(lb + one_m_lb * sgf)
        lfb = jnp.log(lb + one_m_lb * sgb)
        cum_f = csum(tri_l, lff)
        cum_b = csum(tri_u, lfb)
        an_f = chunk_row(cum_f, mid)
        an_b = chunk_row(cum_b, mid)
        a_f = _dot_nt((q * jnp.exp(cum_f - an_f)).astype(BF16), (kf * jnp.exp(an_f - cum_f)).astype(BF16))
        a_b = _dot_nt((q * jnp.exp(cum_b - an_b)).astype(BF16), (kb * jnp.exp(an_b - cum_b)).astype(BF16))
        a = jnp.where(lower, a_f, 0.0) + jnp.where(upper, a_b, 0.0)
        oacc[rows, :] = _dot(a.astype(BF16), vb)
        last_f = chunk_row(cum_f, c - 1)
        last_b = chunk_row(cum_b, 0)
        q_s[rows, 0:HEAD_DIM] = (q * jnp.exp(cum_f)).astype(BF16)
        q_s[rows, HEAD_DIM:] = (q * jnp.exp(cum_b)).astype(BF16)
        kh = jnp.concatenate([(kf * jnp.exp(last_f - cum_f)).astype(BF16),
                              (kb * jnp.exp(last_b - cum_b)).astype(BF16)], axis=1)
        for j in range(g):
            sl = slice(j * c, (j + 1) * c)
            ci = ti * g + j
            u_s[ci] = _dot_tn(vb[sl], kh[sl])
            d_s[ci] = jnp.concatenate([jnp.exp(last_f[j * c:j * c + 1, :]),
                                       jnp.exp(last_b[j * c:j * c + 1, :])], axis=1)
        return carry

    lax.fori_loop(0, nchunk // g, phase_a, 0)
    unroll = g

    def scan(i, carry):
        s_f, s_b = carry
        cb = jnp.where(i < nctx, nctx - 1 - i, nchunk - 1 - (i - nctx))
        st_s[i, :, 0:HEAD_DIM] = s_f.astype(BF16)
        st_s[cb, :, HEAD_DIM:] = s_b.astype(BF16)
        s_f = s_f * d_s[i][:, 0:HEAD_DIM] + u_s[i][:, 0:HEAD_DIM]
        s_b = s_b * d_s[cb][:, HEAD_DIM:] + u_s[cb][:, HEAD_DIM:]
        return s_f, s_b

    zero = jnp.zeros((HEAD_DIM, HEAD_DIM), F32)
    lax.fori_loop(0, nchunk, scan, (zero, zero))

    def phase_c(ci, carry):
        r0 = pl.multiple_of(ci * c, c)
        rows = pl.ds(r0, c)
        o = oacc[rows, :] + _dot_nt(q_s[rows, :], st_s[ci])
        on = _rms_rows(o)
        g = g_ref[rows, :]
        o_ref[rows, :] = (on * gn_ref[...] * (g * _sigmoid(g))).astype(o_ref.dtype)
        return carry

    lax.fori_loop(0, nchunk, phase_c, 0, unroll=unroll)


def _hgrn(proj, lb, gn_gain, bsz, seg, lc):
    t = proj.shape[0]
    c = GLA_CHUNK
    nchunk = seg // c
    nctx = lc // c
    kern = functools.partial(_hgrn_kernel, c=c, nchunk=nchunk, nctx=nctx)
    col = lambda c0: pl.BlockSpec((seg, HEAD_DIM), lambda b, h: (b, c0 + h))
    vec = pl.BlockSpec((1, HEAD_DIM), lambda b, h: (0, h))
    return pl.pallas_call(
        kern,
        grid=(bsz, C_HEADS),
        in_specs=[col(0), col(C_HEADS), col(2 * C_HEADS), col(3 * C_HEADS), col(4 * C_HEADS), vec, vec],
        out_specs=pl.BlockSpec((seg, HEAD_DIM), lambda b, h: (b, h)),
        out_shape=jax.ShapeDtypeStruct((t, C_HEADS * HEAD_DIM), BF16),
        scratch_shapes=[
            pltpu.VMEM((seg, HEAD_DIM), F32),
            pltpu.VMEM((seg, 2 * HEAD_DIM), BF16),
            pltpu.VMEM((nchunk, HEAD_DIM, 2 * HEAD_DIM), F32),
            pltpu.VMEM((nchunk, 1, 2 * HEAD_DIM), F32),
            pltpu.VMEM((nchunk, HEAD_DIM, 2 * HEAD_DIM), BF16),
        ],
        compiler_params=_cp(("arbitrary", "arbitrary")),
        name="hgrn2",
    )(proj, proj, proj, proj, proj, lb, gn_gain)


def _out_kernel(*refs, n_in):
    xs = refs[:n_in]
    ws = refs[n_in:2 * n_in]
    z_ref, gate_ref, gain_ref, sh_ref, sc_ref, wr_ref, br_ref = refs[2 * n_in:2 * n_in + 7]
    z_out, h_out, lg_out = refs[2 * n_in + 7:]
    o = _dot(xs[0][...], ws[0][...])
    for x_ref, w_ref in zip(xs[1:], ws[1:]):
        o = o + _dot(x_ref[...], w_ref[...])
    z1 = z_ref[...] + gate_ref[...] * o
    z_out[...] = z1
    h = _rms_rows(z1) * gain_ref[...]
    h = h * (1.0 + sc_ref[...]) + sh_ref[...]
    hb = h.astype(BF16)
    h_out[...] = hb
    lg_out[...] = _dot(hb, wr_ref[...]) + br_ref[...]


def _out_proj(xs, ws, z, mods, gain, wr, br, bsz, nseg, off, rb):
    d = z.shape[1]
    nout = nseg - off
    n_in = len(xs)
    npad = wr.shape[1]

    def rin(b, j):
        return (b * nseg + off + j, 0)

    def rout(b, j):
        return (b * nout + j, 0)

    def mod(which):
        return pl.BlockSpec((None, None, 1, d),
                            lambda b, j: (jnp.where(j + off == 0, bsz, b), which, 0, 0))

    in_specs = [pl.BlockSpec((rb, x.shape[1]), rin) for x in xs]
    in_specs += [pl.BlockSpec(w.shape, lambda b, j: (0, 0)) for w in ws]
    in_specs += [
        pl.BlockSpec((rb, d), rin),
        mod(2),
        pl.BlockSpec((1, d), lambda b, j: (0, 0)),
        mod(3), mod(4),
        pl.BlockSpec(wr.shape, lambda b, j: (0, 0)),
        pl.BlockSpec((1, npad), lambda b, j: (0, 0)),
    ]
    tm = bsz * nout * rb
    return pl.pallas_call(
        functools.partial(_out_kernel, n_in=n_in),
        grid=(bsz, nout),
        in_specs=in_specs,
        out_specs=[pl.BlockSpec((rb, d), rout), pl.BlockSpec((rb, d), rout), pl.BlockSpec((rb, npad), rout)],
        out_shape=[jax.ShapeDtypeStruct((tm, d), F32), jax.ShapeDtypeStruct((tm, d), BF16),
                   jax.ShapeDtypeStruct((tm, npad), F32)],
        compiler_params=_cp(("arbitrary", "arbitrary")),
        name="out_proj",
    )(*xs, *ws, z, mods, gain, mods, mods, wr, br)


def _topk_kernel(lg_ref, idx_ref, gate_ref):
    l = lg_ref[...]
    lane = lax.broadcasted_iota(jnp.int32, l.shape, 1)
    vals, idxs = [], []
    for _ in range(TOP_K):
        m = jnp.max(l, axis=-1, keepdims=True)
        i = jnp.min(jnp.where(l == m, lane, LANES), axis=-1, keepdims=True)
        vals.append(m)
        idxs.append(i)
        l = jnp.where(lane == i, -jnp.inf, l)
    es = [jnp.exp(v - vals[0]) for v in vals]
    den = es[0] + es[1] + es[2] + es[3]
    io = jnp.zeros(l.shape, jnp.int32)
    go = jnp.zeros(l.shape, F32)
    for k in range(TOP_K):
        io = jnp.where(lane == k, idxs[k], io)
        go = jnp.where(lane == k, es[k] / den, go)
    idx_ref[...] = io
    gate_ref[...] = go


def _topk(logits, tm=512):
    t, n = logits.shape
    tm = _tile(t, tm)
    return pl.pallas_call(
        _topk_kernel,
        grid=(t // tm,),
        in_specs=[pl.BlockSpec((tm, n), lambda i: (i, 0))],
        out_specs=[pl.BlockSpec((tm, n), lambda i: (i, 0)), pl.BlockSpec((tm, n), lambda i: (i, 0))],
        out_shape=[jax.ShapeDtypeStruct((t, n), jnp.int32), jax.ShapeDtypeStruct((t, n), F32)],
        compiler_params=_cp(("arbitrary",)),
        name="router_topk",
    )(logits)


def _gather_kernel(idx0_ref, idxn_ref, src_ref, out_ref, buf, gsem, osem, *, chunk):
    i = pl.program_id(0)
    n = pl.num_programs(0)
    slot = i % 2
    other = 1 - slot

    def out_copy(step, s):
        return pltpu.make_async_copy(buf.at[s], out_ref.at[pl.ds(step * chunk, chunk)], osem.at[s])

    def issue_rows(idx_ref, s):
        def body(p, carry):
            r = 2 * p
            pltpu.make_async_copy(src_ref.at[idx_ref[0, r]], buf.at[s, r], gsem.at[s]).start(priority=0)
            pltpu.make_async_copy(src_ref.at[idx_ref[0, r + 1]], buf.at[s, r + 1], gsem.at[s]).start(priority=1)
            return carry

        lax.fori_loop(0, chunk // 2, body, 0, unroll=4)

    @pl.when(i == 0)
    def _first():
        issue_rows(idx0_ref, 0)

    @pl.when(i + 1 < n)
    def _next():
        @pl.when(i >= 1)
        def _free_slot():
            out_copy(i - 1, other).wait()

        issue_rows(idxn_ref, other)

    pltpu.make_async_copy(src_ref.at[pl.ds(0, chunk)], buf.at[slot], gsem.at[slot]).wait()
    out_copy(i, slot).start()

    @pl.when(i == n - 1)
    def _flush():
        out_copy(i, slot).wait()

        @pl.when(i >= 1)
        def _prev():
            out_copy(i - 1, other).wait()


def _gather_rows(src, idx, chunk=GATHER_CHUNK):
    r = idx.shape[0]
    d = src.shape[1:]
    nchunks = r // chunk
    assert src.shape[0] >= chunk
    idx3 = idx.reshape(nchunks, 1, chunk)
    return pl.pallas_call(
        functools.partial(_gather_kernel, chunk=chunk),
        grid=(nchunks,),
        in_specs=[
            pl.BlockSpec((None, 1, chunk), lambda i: (0, 0, 0), memory_space=pltpu.SMEM),
            pl.BlockSpec((None, 1, chunk), lambda i: (jnp.minimum(i + 1, nchunks - 1), 0, 0),
                         memory_space=pltpu.SMEM),
            pl.BlockSpec(memory_space=pl.ANY),
        ],
        out_specs=pl.BlockSpec(memory_space=pl.ANY),
        out_shape=jax.ShapeDtypeStruct((r,) + d, src.dtype),
        scratch_shapes=[pltpu.VMEM((2, chunk) + d, src.dtype), pltpu.SemaphoreType.DMA((2,)),
                        pltpu.SemaphoreType.DMA((2,))],
        compiler_params=_cp(("arbitrary",)),
        name="row_gather",
    )(idx3, idx3, src)


def _scatter_kernel(dest_ref, src_ref, init_ref, out_ref, sem, *, chunk, fan):
    del init_ref

    def body(t, carry):
        for k in range(fan):
            pltpu.make_async_copy(src_ref.at[t], out_ref.at[dest_ref[0, t * fan + k]], sem).start(priority=k % 2)
        return carry

    lax.fori_loop(0, chunk, body, 0, unroll=2)
    for _ in range(fan):
        pltpu.make_async_copy(src_ref, out_ref.at[pl.ds(0, chunk)], sem).wait()


def _scatter_rows(src, dest, n_out, fan, chunk=GATHER_CHUNK):
    s = src.shape[0]
    d = src.shape[1:]
    chunk = _tile(s, chunk)
    nchunks = s // chunk
    assert n_out >= chunk
    return pl.pallas_call(
        functools.partial(_scatter_kernel, chunk=chunk, fan=fan),
        grid=(nchunks,),
        in_specs=[
            pl.BlockSpec((None, 1, chunk * fan), lambda i: (i, 0, 0), memory_space=pltpu.SMEM),
            pl.BlockSpec((chunk,) + d, lambda i: (i, 0, 0)),
            pl.BlockSpec(memory_space=pl.ANY),
        ],
        out_specs=pl.BlockSpec(memory_space=pl.ANY),
        out_shape=jax.ShapeDtypeStruct((n_out,) + d, src.dtype),
        scratch_shapes=[pltpu.SemaphoreType.DMA],
        input_output_aliases={2: 0},
        compiler_params=_cp(("arbitrary",)),
        name="row_scatter",
    )(dest.reshape(nchunks, 1, chunk * fan), src, jnp.zeros((n_out,) + d, src.dtype))


def _gmm1_kernel(be_ref, first_ref, nused_ref, x_ref, w_ref, b_ref, o_ref, wsc, hsc):
    i = pl.program_id(1)
    nu = nused_ref[0]
    tn = w_ref.shape[1]

    def matmul(slot):
        hsc[slot] = _dot(x_ref[...], wsc[...]) + b_ref[...]

    def activate(slot):
        w2 = 2 * LANES
        bm = o_ref.shape[0]
        lane = lax.broadcasted_iota(jnp.int32, (bm, LANES), 1)
        even = (2 * lane) % LANES
        for s in range(tn // w2):
            parts = []
            for u in range(2):
                lo = s * w2 + u * LANES
                hid = hsc[slot, :, lo:lo + LANES]
                glu = jnp.minimum(hid, SWIGLU_LIMIT)
                glu = glu * _sigmoid(SWIGLU_ALPHA * glu)
                lin = jnp.clip(hid, -SWIGLU_LIMIT, SWIGLU_LIMIT) + 1.0
                prod = glu * pltpu.roll(lin, LANES - 1, 1)
                parts.append(jnp.take_along_axis(prod, even, axis=1))
            o_ref[:, s * LANES:(s + 1) * LANES] = jnp.where(
                lane < LANES // 2, parts[0], parts[1]).astype(o_ref.dtype)

    @pl.when((i < nu) & (first_ref[jnp.minimum(i, nu - 1)] == 1))
    def _cast():
        wsc[...] = w_ref[...].astype(BF16)

    @pl.when((i == 0) & (nu > 0))
    def _head():
        matmul(0)

    for par in range(2):
        @pl.when((i >= 1) & (i < nu) & (i % 2 == par))
        def _steady():
            matmul(par)
            activate(1 - par)

        @pl.when((i >= 1) & (i == nu) & (i % 2 == par))
        def _tail():
            activate(1 - par)

    @pl.when(i > nu)
    def _unused():
        o_ref[...] = jnp.zeros(o_ref.shape, o_ref.dtype)


def _gmm2_kernel(be_ref, first_ref, nused_ref, x_ref, w_ref, b_ref, o_ref, wsc):
    i = pl.program_id(1)

    @pl.when(i < nused_ref[0])
    def _():
        @pl.when(first_ref[i] == 1)
        def _cast():
            wsc[...] = w_ref[...].astype(BF16)

        o_ref[...] = (_dot(x_ref[...], wsc[...]) + b_ref[...]).astype(o_ref.dtype)

    @pl.when(i >= nused_ref[0])
    def _unused():
        o_ref[...] = jnp.zeros(o_ref.shape, o_ref.dtype)


def _gmm(kernel, x, w, b, layer, be, first, nused, bm, tn, out_cols, out_dtype, name, skew=False):
    r, k = x.shape
    n = w.shape[3]
    tn = _tile(n, tn)
    nb = r // bm
    n_tiles = n // tn
    oc = out_cols // n_tiles
    lag = 1 if skew else 0

    def blk(i, nu):
        return jnp.minimum(i, nu[0] - 1)

    scratch = [pltpu.VMEM((k, tn), BF16)]
    if skew:
        scratch.append(pltpu.VMEM((2, bm, tn), F32))
    grid_spec = pltpu.PrefetchScalarGridSpec(
        num_scalar_prefetch=3,
        grid=(n_tiles, nb + lag),
        in_specs=[
            pl.BlockSpec((bm, k), lambda j, i, be, fi, nu: (blk(i, nu), 0)),
            pl.BlockSpec((None, None, k, tn), lambda j, i, be, fi, nu: (layer, be[blk(i, nu)], 0, j)),
            pl.BlockSpec((None, None, 1, tn), lambda j, i, be, fi, nu: (layer, be[blk(i, nu)], 0, j)),
        ],
        out_specs=pl.BlockSpec((bm, oc), lambda j, i, be, fi, nu: (jnp.maximum(i - lag, 0), j)),
        scratch_shapes=scratch,
    )
    return pl.pallas_call(
        kernel,
        grid_spec=grid_spec,
        out_shape=jax.ShapeDtypeStruct((r, out_cols), out_dtype),
        compiler_params=_cp(("arbitrary", "arbitrary")),
        name=name,
    )(be, first, nused, x, w, b.reshape(b.shape[0], b.shape[1], 1, n))


def _combine_kernel(z_ref, y0_ref, y1_ref, y2_ref, y3_ref, gt_ref, gate_ref, gain_ref, sh_ref, sc_ref,
                    *outs, final):
    gt = gt_ref[...]
    ffn = gt[:, 0:1] * y0_ref[...].astype(F32)
    for k, y_ref in enumerate((y1_ref, y2_ref, y3_ref), start=1):
        ffn = ffn + gt[:, k:k + 1] * y_ref[...].astype(F32)
    z2 = z_ref[...] + gate_ref[...] * ffn
    h = _rms_rows(z2) * gain_ref[...]
    if final:
        outs[0][...] = h
    else:
        outs[0][...] = z2
        outs[1][...] = (h * (1.0 + sc_ref[...]) + sh_ref[...]).astype(outs[1].dtype)


def _combine(z1, yt, gates, mods_cur, mods_next, gain, bsz, nblk, rb, final):
    t, d = z1.shape
    has_ctx = not final

    def midx(i):
        if has_ctx:
            return jnp.where(i % nblk == 0, bsz, i // nblk)
        return i // nblk

    row = lambda i: (i, 0)
    nrow = t // rb
    out_specs = [pl.BlockSpec((rb, d), row)]
    out_shape = [jax.ShapeDtypeStruct((t, d), F32)]
    if not final:
        out_specs.append(pl.BlockSpec((rb, d), row))
        out_shape.append(jax.ShapeDtypeStruct((t, d), BF16))
    y_specs = [pl.BlockSpec((rb, d), functools.partial(lambda i, k: (k * nrow + i, 0), k=k))
               for k in range(TOP_K)]
    return pl.pallas_call(
        functools.partial(_combine_kernel, final=final),
        grid=(nrow,),
        in_specs=[
            pl.BlockSpec((rb, d), row),
            *y_specs,
            pl.BlockSpec((rb, gates.shape[1]), row),
            pl.BlockSpec((None, None, 1, d), lambda i: (midx(i), 5, 0, 0)),
            pl.BlockSpec((1, d), lambda i: (0, 0)),
            pl.BlockSpec((None, None, 1, d), lambda i: (midx(i), 0, 0, 0)),
            pl.BlockSpec((None, None, 1, d), lambda i: (midx(i), 1, 0, 0)),
        ],
        out_specs=out_specs,
        out_shape=out_shape,
        compiler_params=_cp(("arbitrary",)),
        name="moe_combine",
    )(z1, yt, yt, yt, yt, gates, mods_cur, gain, mods_next, mods_next)


def _routing(idx, n_exp, bm, gchunk):
    tm = idx.shape[0]
    onehot = (idx[:, :, None] == jnp.arange(n_exp, dtype=jnp.int32)).astype(jnp.int32).sum(axis=1)
    csum = jnp.cumsum(onehot, axis=0)
    counts = csum[-1]
    rank = jnp.take_along_axis(csum - onehot, idx, axis=1)
    padded = (counts + bm - 1) // bm * bm
    pad_end = jnp.cumsum(padded)
    pad_start = pad_end - padded
    dest = pad_start[idx] + rank
    nb = (tm * TOP_K + n_exp * (bm - 1)) // bm
    nb = -(-(nb * bm) // gchunk) * gchunk // bm
    nused = (pad_end[-1] // bm).astype(jnp.int32).reshape(1)
    block_start = jnp.arange(nb, dtype=jnp.int32) * bm
    be = jnp.minimum((pad_end[None, :] <= block_start[:, None]).astype(jnp.int32).sum(axis=1), n_exp - 1)
    first = jnp.concatenate([jnp.ones((1,), jnp.int32), (be[1:] != be[:-1]).astype(jnp.int32)])
    dest = dest.astype(jnp.int32)
    dest_kmajor = dest.T.reshape(-1)
    return dest.reshape(-1), dest_kmajor, nb * bm, be, first, nused


def _moe(h2, logits, w1, b1, w2, b2, layer):
    tm, d = h2.shape
    n_exp = w1.shape[1]
    sub = d // LANES
    idx_p, gate_p = _topk(logits)
    idx = idx_p[:, :TOP_K]
    dest_tmajor, dest, n_rows, be, first, nused = _routing(idx, n_exp, MOE_BM, GATHER_CHUNK)
    xs = _scatter_rows(h2.reshape(tm, sub, LANES), dest_tmajor, n_rows, TOP_K).reshape(-1, d)
    act = _gmm(_gmm1_kernel, xs, w1, b1, layer, be, first, nused, MOE_BM, 1024, w1.shape[3] // 2, BF16, "moe_up",
               skew=True)
    y = _gmm(_gmm2_kernel, act, w2, b2, layer, be, first, nused, MOE_BM, 2048, w2.shape[3], BF16, "moe_down")
    yt = _gather_rows(y.reshape(-1, sub, LANES), dest).reshape(TOP_K * tm, d)
    return yt, gate_p


def _rope_tables(n_lat, lc):
    n_rows = n_lat // GRID_W
    row = jnp.repeat(jnp.arange(n_rows, dtype=F32), GRID_W)
    col = jnp.tile(jnp.arange(GRID_W, dtype=F32), n_rows)
    n_freq = HEAD_DIM // 4
    inv_freq = ROPE_THETA ** (-jnp.arange(n_freq, dtype=F32) / n_freq)
    ang = jnp.concatenate([row[:, None] * inv_freq, col[:, None] * inv_freq], axis=-1)
    cos, sin = jnp.cos(ang), jnp.sin(ang)
    cos2 = jnp.concatenate([cos, cos], axis=-1)
    sin2 = jnp.concatenate([-sin, sin], axis=-1)
    cos2 = jnp.concatenate([jnp.ones((lc, HEAD_DIM), F32), cos2], axis=0)
    sin2 = jnp.concatenate([jnp.zeros((lc, HEAD_DIM), F32), sin2], axis=0)
    return cos2, sin2


def kernel(x, c, ctx, c_ctx, ada_w, ada_b, norm_mix, norm_ffn, ab_w_in, ab_w_out, a_q_norm, a_k_norm, b_decay_exp, b_gn, c_w_in, c_w_out, c_lb, c_gn, router_w, router_b, exp_w1, exp_b1, exp_w2, exp_b2, norm_final):
    bsz, n_lat, d = x.shape
    lc = ctx.shape[1]
    depth = ada_w.shape[0]
    assert depth == 2 and n_lat % lc == 0 and bsz < MOD_ROWS and lc % GLA_CHUNK == 0
    seg = lc + n_lat
    nseg = seg // lc
    n_exp = router_w.shape[2]

    z = jnp.concatenate([ctx, x], axis=1).reshape(bsz * seg, d)

    cond = jnp.zeros((MOD_ROWS, d), F32).at[:bsz].set(c).at[bsz].set(c_ctx)
    mods = _ada(cond, ada_w, ada_b).reshape(depth, MOD_ROWS, N_MOD, 1, d)
    cos2, sin2 = _rope_tables(n_lat, lc)

    lb_soft = jax.nn.softmax(c_lb.astype(F32), axis=0)
    lower_bounds = jnp.cumsum(lb_soft, axis=0) - lb_soft[0]
    log_g = jnp.log1p(-jnp.exp2(-b_decay_exp[0].astype(F32)))

    wr = jnp.zeros((depth, d, LANES), BF16).at[:, :, :n_exp].set(router_w.astype(BF16))
    br = jnp.full((depth, 1, LANES), -1e30, F32).at[:, 0, :n_exp].set(router_b)

    hz = _norm_mod(z, norm_mix[0:1], mods[0], nseg, bsz, lc)
    proj = _matmul(hz, ab_w_in[0].astype(BF16), 1024, 512, name="in_proj0")
    att = _attention(proj, cos2, sin2, a_q_norm[0:1], a_k_norm[0:1], bsz, seg, lc)
    ret = _retention(proj, log_g, cos2, sin2, b_gn[0:1], bsz, seg, lc)
    w_out = ab_w_out[0].astype(BF16)
    aw = A_Q_HEADS * HEAD_DIM
    z1, h2, logits = _out_proj([att, ret], [w_out[:aw], w_out[aw:]], z, mods[0], norm_ffn[0:1],
                               wr[0], br[0], bsz, nseg, 0, lc)
    yt, gates = _moe(h2, logits, exp_w1, exp_b1, exp_w2, exp_b2, 0)
    z2, hz = _combine(z1, yt, gates, mods[0], mods[1], norm_mix[1:2], bsz, nseg, lc, final=False)

    proj = _matmul(hz, c_w_in[0].astype(BF16), 1024, 512, name="in_proj1")
    hg = _hgrn(proj, lower_bounds[1:2], c_gn[0:1], bsz, seg, lc)
    z1, h2, logits = _out_proj([hg], [c_w_out[0].astype(BF16)], z2, mods[1], norm_ffn[1:2],
                               wr[1], br[1], bsz, nseg, 1, lc)
    yt, gates = _moe(h2, logits, exp_w1, exp_b1, exp_w2, exp_b2, 1)
    (out,) = _combine(z1, yt, gates, mods[1], mods[1], norm_final.reshape(1, d), bsz, nseg - 1, lc, final=True)
    return out.reshape(bsz, n_lat, d)
```

```python
import functools

import jax
import jax.numpy as jnp
from jax import lax
from jax.experimental import pallas as pl
from jax.experimental.pallas import tpu as pltpu

HEAD_DIM = 128
GRID_W = 64
ROPE_THETA = 10000.0
NORM_EPS = 1e-6
N_MOD = 6
A_Q_HEADS = 8
A_KV_HEADS = 2
B_HEADS = 8
C_HEADS = 16
TOP_K = 4
SWIGLU_ALPHA = 1.702
SWIGLU_LIMIT = 7.0

LANES = 128
MOD_ROWS = 16
GLA_CHUNK = 64
MOE_BM = 512
GATHER_CHUNK = 512
VMEM_LIMIT = 56 * 1024 * 1024

LOG2E = 1.4426950408889634

F32 = jnp.float32
BF16 = jnp.bfloat16


def _cp(sem, vmem=VMEM_LIMIT):
    return pltpu.CompilerParams(dimension_semantics=sem, vmem_limit_bytes=vmem)


def _tile(n, pref):
    t = pref
    while n % t:
        t //= 2
    return t


def _dot(a, b):
    return jnp.dot(a, b, preferred_element_type=F32)


def _dot_nt(a, b):
    return lax.dot_general(a, b, (((1,), (1,)), ((), ())), preferred_element_type=F32)


def _dot_tn(a, b):
    return lax.dot_general(a, b, (((0,), (0,)), ((), ())), preferred_element_type=F32)


def _sigmoid(x):
    return 1.0 / (1.0 + jnp.exp(-x))


def _rope(t, cos2, sin2):
    return t * cos2 + pltpu.roll(t, HEAD_DIM // 2, 1) * sin2


def _rms_rows(x):
    return x * lax.rsqrt(jnp.mean(x * x, axis=-1, keepdims=True) + NORM_EPS)


def _ada_kernel(c_ref, w_ref, b_ref, o_ref):
    c = c_ref[...]
    s = (c * _sigmoid(c)).astype(BF16)
    o_ref[...] = _dot(s, w_ref[...].astype(BF16)) + b_ref[...]


def _ada(cond, ada_w, ada_b, tn=1024):
    depth, d, n = ada_w.shape
    return pl.pallas_call(
        _ada_kernel,
        grid=(depth, n // tn),
        in_specs=[
            pl.BlockSpec((MOD_ROWS, d), lambda l, j: (0, 0)),
            pl.BlockSpec((None, d, tn), lambda l, j: (l, 0, j)),
            pl.BlockSpec((None, 1, tn), lambda l, j: (l, 0, j)),
        ],
        out_specs=pl.BlockSpec((None, MOD_ROWS, tn), lambda l, j: (l, 0, j)),
        out_shape=jax.ShapeDtypeStruct((depth, MOD_ROWS, n), F32),
        compiler_params=_cp(("arbitrary", "arbitrary")),
        name="ada_mod",
    )(cond, ada_w, ada_b.reshape(depth, 1, n))


def _norm_mod_kernel(z_ref, g_ref, sh_ref, sc_ref, o_ref):
    y = _rms_rows(z_ref[...]) * g_ref[...]
    o_ref[...] = (y * (1.0 + sc_ref[...]) + sh_ref[...]).astype(o_ref.dtype)


def _norm_mod(z, gain, mods, nseg, bsz, rb):
    t, d = z.shape

    def midx(i):
        return jnp.where(i % nseg == 0, bsz, i // nseg)

    return pl.pallas_call(
        _norm_mod_kernel,
        grid=(t // rb,),
        in_specs=[
            pl.BlockSpec((rb, d), lambda i: (i, 0)),
            pl.BlockSpec((1, d), lambda i: (0, 0)),
            pl.BlockSpec((None, None, 1, d), lambda i: (midx(i), 0, 0, 0)),
            pl.BlockSpec((None, None, 1, d), lambda i: (midx(i), 1, 0, 0)),
        ],
        out_specs=pl.BlockSpec((rb, d), lambda i: (i, 0)),
        out_shape=jax.ShapeDtypeStruct((t, d), BF16),
        compiler_params=_cp(("arbitrary",)),
        name="norm_mod",
    )(z, gain, mods, mods)


def _mm_kernel(x_ref, w_ref, o_ref):
    o_ref[...] = _dot(x_ref[...], w_ref[...]).astype(o_ref.dtype)


def _matmul(x, w, tm, tn, out_dtype=F32, name="matmul"):
    m, k = x.shape
    n = w.shape[1]
    tm, tn = _tile(m, tm), _tile(n, tn)
    return pl.pallas_call(
        _mm_kernel,
        grid=(m // tm, n // tn),
        in_specs=[
            pl.BlockSpec((tm, k), lambda i, j: (i, 0)),
            pl.BlockSpec((k, tn), lambda i, j: (0, j)),
        ],
        out_specs=pl.BlockSpec((tm, tn), lambda i, j: (i, j)),
        out_shape=jax.ShapeDtypeStruct((m, n), out_dtype),
        compiler_params=_cp(("arbitrary", "arbitrary")),
        name=name,
    )(x, w)


def _attn_kernel(q_ref, k_ref, v_ref, cosq_ref, sinq_ref, cosk_ref, sink_ref,
                 qg_ref, kg_ref, o_ref, ks, vs, *, lc, group):
    qi = pl.program_id(2)
    rb = q_ref.shape[0]
    scale = HEAD_DIM ** -0.5

    @pl.when(qi == 0)
    def _prep():
        kn = _rms_rows(k_ref[...]) * kg_ref[...]
        ks[...] = _rope(kn, cosk_ref[...], sink_ref[...]).astype(BF16)
        vs[...] = v_ref[...].astype(BF16)

    def attend(k, v):
        for r in range(group):
            cols = slice(r * HEAD_DIM, (r + 1) * HEAD_DIM)
            qh = _rms_rows(q_ref[:, cols]) * qg_ref[...]
            qh = _rope(qh, cosq_ref[...], sinq_ref[...]).astype(BF16)
            s = _dot_nt(qh, k)
            m = jnp.max(s, axis=-1, keepdims=True)
            p = jnp.exp2((s - m) * (scale * LOG2E))
            l = jnp.sum(p, axis=-1, keepdims=True)
            o_ref[:, cols] = (_dot(p.astype(BF16), v) / l).astype(o_ref.dtype)

    @pl.when(qi == 0)
    def _ctx():
        attend(ks[0:lc], vs[0:lc])

    @pl.when(qi > 0)
    def _lat():
        attend(ks[...], vs[...])


def _attention(proj, cos2, sin2, q_gain, k_gain, bsz, seg, lc):
    t = proj.shape[0]
    nseg = seg // lc
    group = A_Q_HEADS // A_KV_HEADS
    kv_w = A_KV_HEADS
    q_col0 = (2 * kv_w + 2 * B_HEADS) // group
    kern = functools.partial(_attn_kernel, lc=lc, group=group)
    return pl.pallas_call(
        kern,
        grid=(bsz, A_KV_HEADS, nseg),
        in_specs=[
            pl.BlockSpec((lc, group * HEAD_DIM), lambda b, g, i: (b * nseg + i, q_col0 + g)),
            pl.BlockSpec((seg, HEAD_DIM), lambda b, g, i: (b, g)),
            pl.BlockSpec((seg, HEAD_DIM), lambda b, g, i: (b, kv_w + g)),
            pl.BlockSpec((lc, HEAD_DIM), lambda b, g, i: (i, 0)),
            pl.BlockSpec((lc, HEAD_DIM), lambda b, g, i: (i, 0)),
            pl.BlockSpec((seg, HEAD_DIM), lambda b, g, i: (0, 0)),
            pl.BlockSpec((seg, HEAD_DIM), lambda b, g, i: (0, 0)),
            pl.BlockSpec((1, HEAD_DIM), lambda b, g, i: (0, 0)),
            pl.BlockSpec((1, HEAD_DIM), lambda b, g, i: (0, 0)),
        ],
        out_specs=pl.BlockSpec((lc, group * HEAD_DIM), lambda b, g, i: (b * nseg + i, g)),
        out_shape=jax.ShapeDtypeStruct((t, A_Q_HEADS * HEAD_DIM), BF16),
        scratch_shapes=[pltpu.VMEM((seg, HEAD_DIM), BF16), pltpu.VMEM((seg, HEAD_DIM), BF16)],
        compiler_params=_cp(("arbitrary", "arbitrary", "arbitrary")),
        name="gqa_attention",
    )(proj, proj, proj, cos2, sin2, cos2, sin2, q_gain, k_gain)


def _ret_kernel(lg_ref, q_ref, k_ref, v_ref, g_ref, cos_ref, sin_ref, gn_ref, o_ref,
                oacc, qf_s, qb_s, uf_s, ub_s, sf_s, sb_s, *, c, nchunk, nctx):
    h = pl.program_id(1)
    lgf = lg_ref[0, h]
    lgb = lg_ref[1, h]
    cf = float(c)
    n_i = lax.broadcasted_iota(jnp.int32, (c, 1), 0).astype(F32)
    m_i = lax.broadcasted_iota(jnp.int32, (1, c), 1).astype(F32)
    diff = n_i - m_i
    dmat = (jnp.where(diff >= 0, jnp.exp(lgf * jnp.maximum(diff, 0.0)), 0.0)
            + jnp.where(diff <= 0, jnp.exp(lgb * jnp.maximum(-diff, 0.0)), 0.0))
    qdf = jnp.exp(lgf * (n_i + 1.0))
    qdb = jnp.exp(lgb * (cf - n_i))
    kdf = jnp.exp(lgf * (cf - 1.0 - n_i))
    kdb = jnp.exp(lgb * n_i)
    cdf = jnp.exp(lgf * cf)
    cdb = jnp.exp(lgb * cf)
    kscale = HEAD_DIM ** -0.5

    def phase_a(ci, carry):
        r0 = pl.multiple_of(ci * c, c)
        rows = pl.ds(r0, c)
        cos2 = cos_ref[rows, :]
        sin2 = sin_ref[rows, :]
        q = _rope(q_ref[rows, :], cos2, sin2)
        k = _rope(k_ref[rows, :], cos2, sin2) * kscale
        vb = v_ref[rows, :].astype(BF16)
        s = _dot_nt(q.astype(BF16), k.astype(BF16)) * dmat
        oacc[rows, :] = _dot(s.astype(BF16), vb)
        qf_s[rows, :] = (q * qdf).astype(BF16)
        qb_s[rows, :] = (q * qdb).astype(BF16)
        uf_s[ci] = _dot_tn((k * kdf).astype(BF16), vb)
        ub_s[ci] = _dot_tn((k * kdb).astype(BF16), vb)
        return carry

    unroll = 3 if nchunk % 3 == 0 else 1
    lax.fori_loop(0, nchunk, phase_a, 0, unroll=unroll)

    def scan_f(ci, s):
        sf_s[ci] = s.astype(BF16)
        return s * cdf + uf_s[ci]

    lax.fori_loop(0, nchunk, scan_f, jnp.zeros((HEAD_DIM, HEAD_DIM), F32))

    def scan_b(i, s):
        ci = jnp.where(i < nctx, nctx - 1 - i, nchunk - 1 - (i - nctx))
        sb_s[ci] = s.astype(BF16)
        return s * cdb + ub_s[ci]

    lax.fori_loop(0, nchunk, scan_b, jnp.zeros((HEAD_DIM, HEAD_DIM), F32))

    def phase_c(ci, carry):
        r0 = pl.multiple_of(ci * c, c)
        rows = pl.ds(r0, c)
        o = oacc[rows, :] + _dot(qf_s[rows, :], sf_s[ci]) + _dot(qb_s[rows, :], sb_s[ci])
        mu = jnp.mean(o, axis=-1, keepdims=True)
        d = o - mu
        on = d * lax.rsqrt(jnp.mean(d * d, axis=-1, keepdims=True) + NORM_EPS)
        g = g_ref[rows, :]
        o_ref[rows, :] = (on * gn_ref[...] * (g * _sigmoid(g))).astype(o_ref.dtype)
        return carry

    lax.fori_loop(0, nchunk, phase_c, 0, unroll=unroll)


def _retention(proj, log_g, cos2, sin2, gn_gain, bsz, seg, lc):
    t = proj.shape[0]
    nchunk = seg // lc
    kvw = 2 * A_KV_HEADS
    k0, v0 = kvw, kvw + B_HEADS
    q0 = kvw + 2 * B_HEADS + A_Q_HEADS
    g0 = q0 + B_HEADS
    kern = functools.partial(_ret_kernel, c=lc, nchunk=nchunk, nctx=1)
    col = lambda c0: pl.BlockSpec((seg, HEAD_DIM), lambda b, h: (b, c0 + h))
    return pl.pallas_call(
        kern,
        grid=(bsz, B_HEADS),
        in_specs=[
            pl.BlockSpec(memory_space=pltpu.SMEM),
            col(q0), col(k0), col(v0), col(g0),
            pl.BlockSpec((seg, HEAD_DIM), lambda b, h: (0, 0)),
            pl.BlockSpec((seg, HEAD_DIM), lambda b, h: (0, 0)),
            pl.BlockSpec((1, HEAD_DIM), lambda b, h: (0, h)),
        ],
        out_specs=pl.BlockSpec((seg, HEAD_DIM), lambda b, h: (b, h)),
        out_shape=jax.ShapeDtypeStruct((t, B_HEADS * HEAD_DIM), BF16),
        scratch_shapes=[
            pltpu.VMEM((seg, HEAD_DIM), F32),
            pltpu.VMEM((seg, HEAD_DIM), BF16),
            pltpu.VMEM((seg, HEAD_DIM), BF16),
            pltpu.VMEM((nchunk, HEAD_DIM, HEAD_DIM), F32),
            pltpu.VMEM((nchunk, HEAD_DIM, HEAD_DIM), F32),
            pltpu.VMEM((nchunk, HEAD_DIM, HEAD_DIM), BF16),
            pltpu.VMEM((nchunk, HEAD_DIM, HEAD_DIM), BF16),
        ],
        compiler_params=_cp(("arbitrary", "arbitrary")),
        name="retention",
    )(log_g, proj, proj, proj, proj, cos2, sin2, gn_gain)


def _split2(x):
    hi = x.astype(BF16)
    lo = (x - hi.astype(F32)).astype(BF16)
    return hi, lo


def _hgrn_kernel(ff_ref, fb_ref, v_ref, q_ref, g_ref, lb_ref, gn_ref, o_ref,
                 oacc, q_s, u_s, d_s, st_s, *, c, nchunk, nctx):
    g = 4 if nchunk % 4 == 0 else (2 if nchunk % 2 == 0 else 1)
    tc = g * c
    lb = lb_ref[...]
    one_m_lb = 1.0 - lb
    n_i = lax.broadcasted_iota(jnp.int32, (tc, tc), 0)
    m_i = lax.broadcasted_iota(jnp.int32, (tc, tc), 1)
    shift = c.bit_length() - 1
    same = jnp.right_shift(n_i, shift) == jnp.right_shift(m_i, shift)
    lower = same & (n_i >= m_i)
    upper = same & (m_i >= n_i)
    tri_l = lower.astype(BF16)
    tri_u = upper.astype(BF16)
    mid = c // 2

    def csum(tri, x):
        hi, lo = _split2(x)
        return _dot(tri, hi) + _dot(tri, lo)

    def chunk_row(x, row):
        return jnp.concatenate(
            [jnp.broadcast_to(x[j * c + row:j * c + row + 1, :], (c, HEAD_DIM)) for j in range(g)], axis=0)

    def phase_a(ti, carry):
        r0 = pl.multiple_of(ti * tc, tc)
        rows = pl.ds(r0, tc)
        qr = q_ref[rows, :]
        q = qr * _sigmoid(qr)
        vb = v_ref[rows, :].astype(BF16)
        frf = ff_ref[rows, :]
        frb = fb_ref[rows, :]
        sgf = _sigmoid(frf)
        sgb = _sigmoid(frb)
        kf = one_m_lb * (1.0 - sgf)
        kb = one_m_lb * (1.0 - sgb)
        lff = jnp.log(lb + one_m_lb * sgf)
        lfb = jnp.log(lb + one_m_lb * sgb)
        cum_f = csum(tri_l, lff)
        cum_b = csum(tri_u, lfb)
        an_f = chunk_row(cum_f, mid)
        an_b = chunk_row(cum_b, mid)
        a_f = _dot_nt((q * jnp.exp(cum_f - an_f)).astype(BF16), (kf * jnp.exp(an_f - cum_f)).astype(BF16))
        a_b = _dot_nt((q * jnp.exp(cum_b - an_b)).astype(BF16), (kb * jnp.exp(an_b - cum_b)).astype(BF16))
        a = jnp.where(lower, a_f, 0.0) + jnp.where(upper, a_b, 0.0)
        oacc[rows, :] = _dot(a.astype(BF16), vb)
        last_f = chunk_row(cum_f, c - 1)
        last_b = chunk_row(cum_b, 0)
        q_s[rows, 0:HEAD_DIM] = (q * jnp.exp(cum_f)).astype(BF16)
        q_s[rows, HEAD_DIM:] = (q * jnp.exp(cum_b)).astype(BF16)
        kh = jnp.concatenate([(kf * jnp.exp(last_f - cum_f)).astype(BF16),
                              (kb * jnp.exp(last_b - cum_b)).astype(BF16)], axis=1)
        for j in range(g):
            sl = slice(j * c, (j + 1) * c)
            ci = ti * g + j
            u_s[ci] = _dot_tn(vb[sl], kh[sl])
            d_s[ci] = jnp.concatenate([jnp.exp(last_f[j * c:j * c + 1, :]),
                                       jnp.exp(last_b[j * c:j * c + 1, :])], axis=1)
        return carry

    ntile = nchunk // g
    lax.fori_loop(0, ntile, phase_a, 0, unroll=3 if ntile % 3 == 0 else 1)
    unroll = 12 if nchunk % 12 == 0 else g

    def scan(i, carry):
        s_f, s_b = carry
        cb = jnp.where(i < nctx, nctx - 1 - i, nchunk - 1 - (i - nctx))
        st_s[i, :, 0:HEAD_DIM] = s_f.astype(BF16)
        st_s[cb, :, HEAD_DIM:] = s_b.astype(BF16)
        s_f = s_f * d_s[i][:, 0:HEAD_DIM] + u_s[i][:, 0:HEAD_DIM]
        s_b = s_b * d_s[cb][:, HEAD_DIM:] + u_s[cb][:, HEAD_DIM:]
        return s_f, s_b

    zero = jnp.zeros((HEAD_DIM, HEAD_DIM), F32)
    lax.fori_loop(0, nchunk, scan, (zero, zero))

    def phase_c(ci, carry):
        r0 = pl.multiple_of(ci * c, c)
        rows = pl.ds(r0, c)
        o = oacc[rows, :] + _dot_nt(q_s[rows, :], st_s[ci])
        on = _rms_rows(o)
        g = g_ref[rows, :]
        o_ref[rows, :] = (on * gn_ref[...] * (g * _sigmoid(g))).astype(o_ref.dtype)
        return carry

    lax.fori_loop(0, nchunk, phase_c, 0, unroll=unroll)


def _hgrn(proj, lb, gn_gain, bsz, seg, lc):
    t = proj.shape[0]
    c = GLA_CHUNK
    nchunk = seg // c
    nctx = lc // c
    kern = functools.partial(_hgrn_kernel, c=c, nchunk=nchunk, nctx=nctx)
    col = lambda c0: pl.BlockSpec((seg, HEAD_DIM), lambda b, h: (b, c0 + h))
    vec = pl.BlockSpec((1, HEAD_DIM), lambda b, h: (0, h))
    return pl.pallas_call(
        kern,
        grid=(bsz, C_HEADS),
        in_specs=[col(0), col(C_HEADS), col(2 * C_HEADS), col(3 * C_HEADS), col(4 * C_HEADS), vec, vec],
        out_specs=pl.BlockSpec((seg, HEAD_DIM), lambda b, h: (b, h)),
        out_shape=jax.ShapeDtypeStruct((t, C_HEADS * HEAD_DIM), BF16),
        scratch_shapes=[
            pltpu.VMEM((seg, HEAD_DIM), F32),
            pltpu.VMEM((seg, 2 * HEAD_DIM), BF16),
            pltpu.VMEM((nchunk, HEAD_DIM, 2 * HEAD_DIM), F32),
            pltpu.VMEM((nchunk, 1, 2 * HEAD_DIM), F32),
            pltpu.VMEM((nchunk, HEAD_DIM, 2 * HEAD_DIM), BF16),
        ],
        compiler_params=_cp(("arbitrary", "arbitrary")),
        name="hgrn2",
    )(proj, proj, proj, proj, proj, lb, gn_gain)


def _out_kernel(*refs, n_in):
    xs = refs[:n_in]
    ws = refs[n_in:2 * n_in]
    z_ref, gate_ref, gain_ref, sh_ref, sc_ref, wr_ref, br_ref = refs[2 * n_in:2 * n_in + 7]
    z_out, h_out, lg_out = refs[2 * n_in + 7:]
    o = _dot(xs[0][...], ws[0][...])
    for x_ref, w_ref in zip(xs[1:], ws[1:]):
        o = o + _dot(x_ref[...], w_ref[...])
    z1 = z_ref[...] + gate_ref[...] * o
    z_out[...] = z1
    h = _rms_rows(z1) * gain_ref[...]
    h = h * (1.0 + sc_ref[...]) + sh_ref[...]
    hb = h.astype(BF16)
    h_out[...] = hb
    lg_out[...] = _dot(hb, wr_ref[...]) + br_ref[...]


def _out_proj(xs, ws, z, mods, gain, wr, br, bsz, nseg, off, rb):
    d = z.shape[1]
    nout = nseg - off
    n_in = len(xs)
    npad = wr.shape[1]

    def rin(b, j):
        return (b * nseg + off + j, 0)

    def rout(b, j):
        return (b * nout + j, 0)

    def mod(which):
        return pl.BlockSpec((None, None, 1, d),
                            lambda b, j: (jnp.where(j + off == 0, bsz, b), which, 0, 0))

    in_specs = [pl.BlockSpec((rb, x.shape[1]), rin) for x in xs]
    in_specs += [pl.BlockSpec(w.shape, lambda b, j: (0, 0)) for w in ws]
    in_specs += [
        pl.BlockSpec((rb, d), rin),
        mod(2),
        pl.BlockSpec((1, d), lambda b, j: (0, 0)),
        mod(3), mod(4),
        pl.BlockSpec(wr.shape, lambda b, j: (0, 0)),
        pl.BlockSpec((1, npad), lambda b, j: (0, 0)),
    ]
    tm = bsz * nout * rb
    return pl.pallas_call(
        functools.partial(_out_kernel, n_in=n_in),
        grid=(bsz, nout),
        in_specs=in_specs,
        out_specs=[pl.BlockSpec((rb, d), rout), pl.BlockSpec((rb, d), rout), pl.BlockSpec((rb, npad), rout)],
        out_shape=[jax.ShapeDtypeStruct((tm, d), F32), jax.ShapeDtypeStruct((tm, d), BF16),
                   jax.ShapeDtypeStruct((tm, npad), F32)],
        compiler_params=_cp(("arbitrary", "arbitrary")),
        name="out_proj",
    )(*xs, *ws, z, mods, gain, mods, mods, wr, br)


def _topk_kernel(lg_ref, idx_ref, gate_ref):
    l = lg_ref[...]
    lane = lax.broadcasted_iota(jnp.int32, l.shape, 1)
    vals, idxs = [], []
    for _ in range(TOP_K):
        m = jnp.max(l, axis=-1, keepdims=True)
        i = jnp.min(jnp.where(l == m, lane, LANES), axis=-1, keepdims=True)
        vals.append(m)
        idxs.append(i)
        l = jnp.where(lane == i, -jnp.inf, l)
    es = [jnp.exp(v - vals[0]) for v in vals]
    den = es[0] + es[1] + es[2] + es[3]
    io = jnp.zeros(l.shape, jnp.int32)
    go = jnp.zeros(l.shape, F32)
    for k in range(TOP_K):
        io = jnp.where(lane == k, idxs[k], io)
        go = jnp.where(lane == k, es[k] / den, go)
    idx_ref[...] = io
    gate_ref[...] = go


def _topk(logits, tm=512):
    t, n = logits.shape
    tm = _tile(t, tm)
    return pl.pallas_call(
        _topk_kernel,
        grid=(t // tm,),
        in_specs=[pl.BlockSpec((tm, n), lambda i: (i, 0))],
        out_specs=[pl.BlockSpec((tm, n), lambda i: (i, 0)), pl.BlockSpec((tm, n), lambda i: (i, 0))],
        out_shape=[jax.ShapeDtypeStruct((t, n), jnp.int32), jax.ShapeDtypeStruct((t, n), F32)],
        compiler_params=_cp(("arbitrary",)),
        name="router_topk",
    )(logits)


def _gather_kernel(idx0_ref, idxn_ref, src_ref, out_ref, buf, gsem, osem, *, chunk):
    i = pl.program_id(0)
    n = pl.num_programs(0)
    slot = i % 2
    other = 1 - slot

    def out_copy(step, s):
        return pltpu.make_async_copy(buf.at[s], out_ref.at[pl.ds(step * chunk, chunk)], osem.at[s])

    def issue_rows(idx_ref, s):
        def body(p, carry):
            r = 2 * p
            pltpu.make_async_copy(src_ref.at[idx_ref[0, r]], buf.at[s, r], gsem.at[s]).start(priority=0)
            pltpu.make_async_copy(src_ref.at[idx_ref[0, r + 1]], buf.at[s, r + 1], gsem.at[s]).start(priority=1)
            return carry

        lax.fori_loop(0, chunk // 2, body, 0, unroll=4)

    @pl.when(i == 0)
    def _first():
        issue_rows(idx0_ref, 0)

    @pl.when(i + 1 < n)
    def _next():
        @pl.when(i >= 1)
        def _free_slot():
            out_copy(i - 1, other).wait()

        issue_rows(idxn_ref, other)

    pltpu.make_async_copy(src_ref.at[pl.ds(0, chunk)], buf.at[slot], gsem.at[slot]).wait()
    out_copy(i, slot).start()

    @pl.when(i == n - 1)
    def _flush():
        out_copy(i, slot).wait()

        @pl.when(i >= 1)
        def _prev():
            out_copy(i - 1, other).wait()


def _gather_rows(src, idx, chunk=GATHER_CHUNK):
    r = idx.shape[0]
    d = src.shape[1:]
    nchunks = r // chunk
    assert src.shape[0] >= chunk
    idx3 = idx.reshape(nchunks, 1, chunk)
    return pl.pallas_call(
        functools.partial(_gather_kernel, chunk=chunk),
        grid=(nchunks,),
        in_specs=[
            pl.BlockSpec((None, 1, chunk), lambda i: (0, 0, 0), memory_space=pltpu.SMEM),
            pl.BlockSpec((None, 1, chunk), lambda i: (jnp.minimum(i + 1, nchunks - 1), 0, 0),
                         memory_space=pltpu.SMEM),
            pl.BlockSpec(memory_space=pl.ANY),
        ],
        out_specs=pl.BlockSpec(memory_space=pl.ANY),
        out_shape=jax.ShapeDtypeStruct((r,) + d, src.dtype),
        scratch_shapes=[pltpu.VMEM((2, chunk) + d, src.dtype), pltpu.SemaphoreType.DMA((2,)),
                        pltpu.SemaphoreType.DMA((2,))],
        compiler_params=_cp(("arbitrary",)),
        name="row_gather",
    )(idx3, idx3, src)


def _scatter_kernel(dest_ref, src_ref, init_ref, out_ref, sem, *, chunk, fan):
    del init_ref

    def body(t, carry):
        for k in range(fan):
            pltpu.make_async_copy(src_ref.at[t], out_ref.at[dest_ref[0, t * fan + k]], sem).start(priority=k % 2)
        return carry

    lax.fori_loop(0, chunk, body, 0, unroll=2)
    for _ in range(fan):
        pltpu.make_async_copy(src_ref, out_ref.at[pl.ds(0, chunk)], sem).wait()


def _scatter_rows(src, dest, n_out, fan, chunk=GATHER_CHUNK):
    s = src.shape[0]
    d = src.shape[1:]
    chunk = _tile(s, chunk)
    nchunks = s // chunk
    assert n_out >= chunk
    return pl.pallas_call(
        functools.partial(_scatter_kernel, chunk=chunk, fan=fan),
        grid=(nchunks,),
        in_specs=[
            pl.BlockSpec((None, 1, chunk * fan), lambda i: (i, 0, 0), memory_space=pltpu.SMEM),
            pl.BlockSpec((chunk,) + d, lambda i: (i, 0, 0)),
            pl.BlockSpec(memory_space=pl.ANY),
        ],
        out_specs=pl.BlockSpec(memory_space=pl.ANY),
        out_shape=jax.ShapeDtypeStruct((n_out,) + d, src.dtype),
        scratch_shapes=[pltpu.SemaphoreType.DMA],
        input_output_aliases={2: 0},
        compiler_params=_cp(("arbitrary",)),
        name="row_scatter",
    )(dest.reshape(nchunks, 1, chunk * fan), src, jnp.zeros((n_out,) + d, src.dtype))


def _gmm1_kernel(be_ref, first_ref, nused_ref, x_ref, w_ref, b_ref, o_ref, wsc, hsc):
    i = pl.program_id(1)
    nu = nused_ref[0]
    tn = w_ref.shape[1]

    def matmul(slot):
        hsc[slot] = _dot(x_ref[...], wsc[...]) + b_ref[...]

    def activate(slot):
        w2 = 2 * LANES
        bm = o_ref.shape[0]
        lane = lax.broadcasted_iota(jnp.int32, (bm, LANES), 1)
        even = (2 * lane) % LANES
        for s in range(tn // w2):
            parts = []
            for u in range(2):
                lo = s * w2 + u * LANES
                hid = hsc[slot, :, lo:lo + LANES]
                glu = jnp.minimum(hid, SWIGLU_LIMIT)
                glu = glu * _sigmoid(SWIGLU_ALPHA * glu)
                lin = jnp.clip(hid, -SWIGLU_LIMIT, SWIGLU_LIMIT) + 1.0
                prod = glu * pltpu.roll(lin, LANES - 1, 1)
                parts.append(jnp.take_along_axis(prod, even, axis=1))
            o_ref[:, s * LANES:(s + 1) * LANES] = jnp.where(
                lane < LANES // 2, parts[0], parts[1]).astype(o_ref.dtype)

    @pl.when((i < nu) & (first_ref[jnp.minimum(i, nu - 1)] == 1))
    def _cast():
        wsc[...] = w_ref[...].astype(BF16)

    @pl.when((i == 0) & (nu > 0))
    def _head():
        matmul(0)

    for par in range(2):
        @pl.when((i >= 1) & (i < nu) & (i % 2 == par))
        def _steady():
            matmul(par)
            activate(1 - par)

        @pl.when((i >= 1) & (i == nu) & (i % 2 == par))
        def _tail():
            activate(1 - par)

    @pl.when(i > nu)
    def _unused():
        o_ref[...] = jnp.zeros(o_ref.shape, o_ref.dtype)


def _gmm2_kernel(be_ref, first_ref, nused_ref, x_ref, w_ref, b_ref, o_ref, wsc):
    i = pl.program_id(1)

    @pl.when(i < nused_ref[0])
    def _():
        @pl.when(first_ref[i] == 1)
        def _cast():
            wsc[...] = w_ref[...].astype(BF16)

        o_ref[...] = (_dot(x_ref[...], wsc[...]) + b_ref[...]).astype(o_ref.dtype)

    @pl.when(i >= nused_ref[0])
    def _unused():
        o_ref[...] = jnp.zeros(o_ref.shape, o_ref.dtype)


def _gmm(kernel, x, w, b, layer, be, first, nused, bm, tn, out_cols, out_dtype, name, skew=False):
    r, k = x.shape
    n = w.shape[3]
    tn = _tile(n, tn)
    nb = r // bm
    n_tiles = n // tn
    oc = out_cols // n_tiles
    lag = 1 if skew else 0

    def blk(i, nu):
        return jnp.minimum(i, nu[0] - 1)

    scratch = [pltpu.VMEM((k, tn), BF16)]
    if skew:
        scratch.append(pltpu.VMEM((2, bm, tn), F32))
    grid_spec = pltpu.PrefetchScalarGridSpec(
        num_scalar_prefetch=3,
        grid=(n_tiles, nb + lag),
        in_specs=[
            pl.BlockSpec((bm, k), lambda j, i, be, fi, nu: (blk(i, nu), 0)),
            pl.BlockSpec((None, None, k, tn), lambda j, i, be, fi, nu: (layer, be[blk(i, nu)], 0, j)),
            pl.BlockSpec((None, None, 1, tn), lambda j, i, be, fi, nu: (layer, be[blk(i, nu)], 0, j)),
        ],
        out_specs=pl.BlockSpec((bm, oc), lambda j, i, be, fi, nu: (jnp.maximum(i - lag, 0), j)),
        scratch_shapes=scratch,
    )
    return pl.pallas_call(
        kernel,
        grid_spec=grid_spec,
        out_shape=jax.ShapeDtypeStruct((r, out_cols), out_dtype),
        compiler_params=_cp(("arbitrary", "arbitrary")),
        name=name,
    )(be, first, nused, x, w, b.reshape(b.shape[0], b.shape[1], 1, n))


def _combine_kernel(z_ref, y0_ref, y1_ref, y2_ref, y3_ref, gt_ref, gate_ref, gain_ref, sh_ref, sc_ref,
                    *outs, final):
    gt = gt_ref[...]
    ffn = gt[:, 0:1] * y0_ref[...].astype(F32)
    for k, y_ref in enumerate((y1_ref, y2_ref, y3_ref), start=1):
        ffn = ffn + gt[:, k:k + 1] * y_ref[...].astype(F32)
    z2 = z_ref[...] + gate_ref[...] * ffn
    h = _rms_rows(z2) * gain_ref[...]
    if final:
        outs[0][...] = h
    else:
        outs[0][...] = z2
        outs[1][...] = (h * (1.0 + sc_ref[...]) + sh_ref[...]).astype(outs[1].dtype)


def _combine(z1, yt, gates, mods_cur, mods_next, gain, bsz, nblk, rb, final):
    t, d = z1.shape
    has_ctx = not final

    def midx(i):
        if has_ctx:
            return jnp.where(i % nblk == 0, bsz, i // nblk)
        return i // nblk

    row = lambda i: (i, 0)
    nrow = t // rb
    out_specs = [pl.BlockSpec((rb, d), row)]
    out_shape = [jax.ShapeDtypeStruct((t, d), F32)]
    if not final:
        out_specs.append(pl.BlockSpec((rb, d), row))
        out_shape.append(jax.ShapeDtypeStruct((t, d), BF16))
    y_specs = [pl.BlockSpec((rb, d), functools.partial(lambda i, k: (k * nrow + i, 0), k=k))
               for k in range(TOP_K)]
    return pl.pallas_call(
        functools.partial(_combine_kernel, final=final),
        grid=(nrow,),
        in_specs=[
            pl.BlockSpec((rb, d), row),
            *y_specs,
            pl.BlockSpec((rb, gates.shape[1]), row),
            pl.BlockSpec((None, None, 1, d), lambda i: (midx(i), 5, 0, 0)),
            pl.BlockSpec((1, d), lambda i: (0, 0)),
            pl.BlockSpec((None, None, 1, d), lambda i: (midx(i), 0, 0, 0)),
            pl.BlockSpec((None, None, 1, d), lambda i: (midx(i), 1, 0, 0)),
        ],
        out_specs=out_specs,
        out_shape=out_shape,
        compiler_params=_cp(("arbitrary",)),
        name="moe_combine",
    )(z1, yt, yt, yt, yt, gates, mods_cur, gain, mods_next, mods_next)


def _routing(idx, n_exp, bm, gchunk):
    tm = idx.shape[0]
    onehot = (idx[:, :, None] == jnp.arange(n_exp, dtype=jnp.int32)).astype(jnp.int32).sum(axis=1)
    csum = jnp.cumsum(onehot, axis=0)
    counts = csum[-1]
    rank = jnp.take_along_axis(csum - onehot, idx, axis=1)
    padded = (counts + bm - 1) // bm * bm
    pad_end = jnp.cumsum(padded)
    pad_start = pad_end - padded
    dest = pad_start[idx] + rank
    nb = (tm * TOP_K + n_exp * (bm - 1)) // bm
    nb = -(-(nb * bm) // gchunk) * gchunk // bm
    nused = (pad_end[-1] // bm).astype(jnp.int32).reshape(1)
    block_start = jnp.arange(nb, dtype=jnp.int32) * bm
    be = jnp.minimum((pad_end[None, :] <= block_start[:, None]).astype(jnp.int32).sum(axis=1), n_exp - 1)
    first = jnp.concatenate([jnp.ones((1,), jnp.int32), (be[1:] != be[:-1]).astype(jnp.int32)])
    dest = dest.astype(jnp.int32)
    dest_kmajor = dest.T.reshape(-1)
    return dest.reshape(-1), dest_kmajor, nb * bm, be, first, nused


def _moe(h2, logits, w1, b1, w2, b2, layer):
    tm, d = h2.shape
    n_exp = w1.shape[1]
    sub = d // LANES
    idx_p, gate_p = _topk(logits)
    idx = idx_p[:, :TOP_K]
    dest_tmajor, dest, n_rows, be, first, nused = _routing(idx, n_exp, MOE_BM, GATHER_CHUNK)
    xs = _scatter_rows(h2.reshape(tm, sub, LANES), dest_tmajor, n_rows, TOP_K).reshape(-1, d)
    act = _gmm(_gmm1_kernel, xs, w1, b1, layer, be, first, nused, MOE_BM, 1024, w1.shape[3] // 2, BF16, "moe_up",
               skew=True)
    y = _gmm(_gmm2_kernel, act, w2, b2, layer, be, first, nused, MOE_BM, 2048, w2.shape[3], BF16, "moe_down")
    yt = _gather_rows(y.reshape(-1, sub, LANES), dest).reshape(TOP_K * tm, d)
    return yt, gate_p


def _rope_tables(n_lat, lc):
    n_rows = n_lat // GRID_W
    row = jnp.repeat(jnp.arange(n_rows, dtype=F32), GRID_W)
    col = jnp.tile(jnp.arange(GRID_W, dtype=F32), n_rows)
    n_freq = HEAD_DIM // 4
    inv_freq = ROPE_THETA ** (-jnp.arange(n_freq, dtype=F32) / n_freq)
    ang = jnp.concatenate([row[:, None] * inv_freq, col[:, None] * inv_freq], axis=-1)
    cos, sin = jnp.cos(ang), jnp.sin(ang)
    cos2 = jnp.concatenate([cos, cos], axis=-1)
    sin2 = jnp.concatenate([-sin, sin], axis=-1)
    cos2 = jnp.concatenate([jnp.ones((lc, HEAD_DIM), F32), cos2], axis=0)
    sin2 = jnp.concatenate([jnp.zeros((lc, HEAD_DIM), F32), sin2], axis=0)
    return cos2, sin2


def kernel(x, c, ctx, c_ctx, ada_w, ada_b, norm_mix, norm_ffn, ab_w_in, ab_w_out, a_q_norm, a_k_norm, b_decay_exp, b_gn, c_w_in, c_w_out, c_lb, c_gn, router_w, router_b, exp_w1, exp_b1, exp_w2, exp_b2, norm_final):
    bsz, n_lat, d = x.shape
    lc = ctx.shape[1]
    depth = ada_w.shape[0]
    assert depth == 2 and n_lat % lc == 0 and bsz < MOD_ROWS and lc % GLA_CHUNK == 0
    seg = lc + n_lat
    nseg = seg // lc
    n_exp = router_w.shape[2]

    z = jnp.concatenate([ctx, x], axis=1).reshape(bsz * seg, d)

    cond = jnp.zeros((MOD_ROWS, d), F32).at[:bsz].set(c).at[bsz].set(c_ctx)
    mods = _ada(cond, ada_w, ada_b).reshape(depth, MOD_ROWS, N_MOD, 1, d)
    cos2, sin2 = _rope_tables(n_lat, lc)

    lb_soft = jax.nn.softmax(c_lb.astype(F32), axis=0)
    lower_bounds = jnp.cumsum(lb_soft, axis=0) - lb_soft[0]
    log_g = jnp.log1p(-jnp.exp2(-b_decay_exp[0].astype(F32)))

    wr = jnp.zeros((depth, d, LANES), BF16).at[:, :, :n_exp].set(router_w.astype(BF16))
    br = jnp.full((depth, 1, LANES), -1e30, F32).at[:, 0, :n_exp].set(router_b)

    hz = _norm_mod(z, norm_mix[0:1], mods[0], nseg, bsz, lc)
    proj = _matmul(hz, ab_w_in[0].astype(BF16), 1024, 512, name="in_proj0")
    att = _attention(proj, cos2, sin2, a_q_norm[0:1], a_k_norm[0:1], bsz, seg, lc)
    ret = _retention(proj, log_g, cos2, sin2, b_gn[0:1], bsz, seg, lc)
    w_out = ab_w_out[0].astype(BF16)
    aw = A_Q_HEADS * HEAD_DIM
    z1, h2, logits = _out_proj([att, ret], [w_out[:aw], w_out[aw:]], z, mods[0], norm_ffn[0:1],
                               wr[0], br[0], bsz, nseg, 0, lc)
    yt, gates = _moe(h2, logits, exp_w1, exp_b1, exp_w2, exp_b2, 0)
    z2, hz = _combine(z1, yt, gates, mods[0], mods[1], norm_mix[1:2], bsz, nseg, lc, final=False)

    proj = _matmul(hz, c_w_in[0].astype(BF16), 1024, 512, name="in_proj1")
    hg = _hgrn(proj, lower_bounds[1:2], c_gn[0:1], bsz, seg, lc)
    z1, h2, logits = _out_proj([hg], [c_w_out[0].astype(BF16)], z2, mods[1], norm_ffn[1:2],
                               wr[1], br[1], bsz, nseg, 1, lc)
    yt, gates = _moe(h2, logits, exp_w1, exp_b1, exp_w2, exp_b2, 1)
    (out,) = _combine(z1, yt, gates, mods[1], mods[1], norm_final.reshape(1, d), bsz, nseg - 1, lc, final=True)
    return out.reshape(bsz, n_lat, d)
```

```python
import functools

import jax
import jax.numpy as jnp
from jax import lax
from jax.experimental import pallas as pl
from jax.experimental.pallas import tpu as pltpu

HEAD_DIM = 128
GRID_W = 64
ROPE_THETA = 10000.0
NORM_EPS = 1e-6
N_MOD = 6
A_Q_HEADS = 8
A_KV_HEADS = 2
B_HEADS = 8
C_HEADS = 16
TOP_K = 4
SWIGLU_ALPHA = 1.702
SWIGLU_LIMIT = 7.0

LANES = 128
MOD_ROWS = 16
GLA_CHUNK = 64
MOE_BM = 512
GATHER_CHUNK = 512
VMEM_LIMIT = 56 * 1024 * 1024

LOG2E = 1.4426950408889634

F32 = jnp.float32
BF16 = jnp.bfloat16


def _cp(sem, vmem=VMEM_LIMIT):
    return pltpu.CompilerParams(dimension_semantics=sem, vmem_limit_bytes=vmem)


def _tile(n, pref):
    t = pref
    while n % t:
        t //= 2
    return t


def _dot(a, b):
    return jnp.dot(a, b, preferred_element_type=F32)


def _dot_nt(a, b):
    return lax.dot_general(a, b, (((1,), (1,)), ((), ())), preferred_element_type=F32)


def _dot_tn(a, b):
    return lax.dot_general(a, b, (((0,), (0,)), ((), ())), preferred_element_type=F32)


def _sigmoid(x):
    return 1.0 / (1.0 + jnp.exp(-x))


def _rope(t, cos2, sin2):
    return t * cos2 + pltpu.roll(t, HEAD_DIM // 2, 1) * sin2


def _rms_rows(x):
    return x * lax.rsqrt(jnp.mean(x * x, axis=-1, keepdims=True) + NORM_EPS)


def _ada_kernel(c_ref, w_ref, b_ref, o_ref):
    c = c_ref[...]
    s = (c * _sigmoid(c)).astype(BF16)
    o_ref[...] = _dot(s, w_ref[...].astype(BF16)) + b_ref[...]


def _ada(cond, ada_w, ada_b, tn=1024):
    depth, d, n = ada_w.shape
    return pl.pallas_call(
        _ada_kernel,
        grid=(depth, n // tn),
        in_specs=[
            pl.BlockSpec((MOD_ROWS, d), lambda l, j: (0, 0)),
            pl.BlockSpec((None, d, tn), lambda l, j: (l, 0, j)),
            pl.BlockSpec((None, 1, tn), lambda l, j: (l, 0, j)),
        ],
        out_specs=pl.BlockSpec((None, MOD_ROWS, tn), lambda l, j: (l, 0, j)),
        out_shape=jax.ShapeDtypeStruct((depth, MOD_ROWS, n), F32),
        compiler_params=_cp(("arbitrary", "arbitrary")),
        name="ada_mod",
    )(cond, ada_w, ada_b.reshape(depth, 1, n))


def _norm_mod_kernel(z_ref, g_ref, sh_ref, sc_ref, o_ref):
    y = _rms_rows(z_ref[...]) * g_ref[...]
    o_ref[...] = (y * (1.0 + sc_ref[...]) + sh_ref[...]).astype(o_ref.dtype)


def _norm_mod(z, gain, mods, nseg, bsz, rb):
    t, d = z.shape

    def midx(i):
        return jnp.where(i % nseg == 0, bsz, i // nseg)

    return pl.pallas_call(
        _norm_mod_kernel,
        grid=(t // rb,),
        in_specs=[
            pl.BlockSpec((rb, d), lambda i: (i, 0)),
            pl.BlockSpec((1, d), lambda i: (0, 0)),
            pl.BlockSpec((None, None, 1, d), lambda i: (midx(i), 0, 0, 0)),
            pl.BlockSpec((None, None, 1, d), lambda i: (midx(i), 1, 0, 0)),
        ],
        out_specs=pl.BlockSpec((rb, d), lambda i: (i, 0)),
        out_shape=jax.ShapeDtypeStruct((t, d), BF16),
        compiler_params=_cp(("arbitrary",)),
        name="norm_mod",
    )(z, gain, mods, mods)


def _mm_kernel(x_ref, w_ref, o_ref):
    o_ref[...] = _dot(x_ref[...], w_ref[...]).astype(o_ref.dtype)


def _matmul(x, w, tm, tn, out_dtype=F32, name="matmul"):
    m, k = x.shape
    n = w.shape[1]
    tm, tn = _tile(m, tm), _tile(n, tn)
    return pl.pallas_call(
        _mm_kernel,
        grid=(m // tm, n // tn),
        in_specs=[
            pl.BlockSpec((tm, k), lambda i, j: (i, 0)),
            pl.BlockSpec((k, tn), lambda i, j: (0, j)),
        ],
        out_specs=pl.BlockSpec((tm, tn), lambda i, j: (i, j)),
        out_shape=jax.ShapeDtypeStruct((m, n), out_dtype),
        compiler_params=_cp(("arbitrary", "arbitrary")),
        name=name,
    )(x, w)


def _attn_kernel(q_ref, k_ref, v_ref, cosq_ref, sinq_ref, cosk_ref, sink_ref,
                 qg_ref, kg_ref, o_ref, ks, vs, *, lc, group):
    qi = pl.program_id(2)
    rb = q_ref.shape[0]
    scale = HEAD_DIM ** -0.5

    @pl.when(qi == 0)
    def _prep():
        kn = _rms_rows(k_ref[...]) * kg_ref[...]
        ks[...] = _rope(kn, cosk_ref[...], sink_ref[...]).astype(BF16)
        vs[...] = v_ref[...].astype(BF16)

    def attend(k, v):
        for r in range(group):
            cols = slice(r * HEAD_DIM, (r + 1) * HEAD_DIM)
            qh = _rms_rows(q_ref[:, cols]) * qg_ref[...]
            qh = _rope(qh, cosq_ref[...], sinq_ref[...]).astype(BF16)
            s = _dot_nt(qh, k)
            m = jnp.max(s, axis=-1, keepdims=True)
            p = jnp.exp2((s - m) * (scale * LOG2E))
            l = jnp.sum(p, axis=-1, keepdims=True)
            o_ref[:, cols] = (_dot(p.astype(BF16), v) / l).astype(o_ref.dtype)

    @pl.when(qi == 0)
    def _ctx():
        attend(ks[0:lc], vs[0:lc])

    @pl.when(qi > 0)
    def _lat():
        attend(ks[...], vs[...])


def _attention(proj, cos2, sin2, q_gain, k_gain, bsz, seg, lc):
    t = proj.shape[0]
    nseg = seg // lc
    group = A_Q_HEADS // A_KV_HEADS
    kv_w = A_KV_HEADS
    q_col0 = (2 * kv_w + 2 * B_HEADS) // group
    kern = functools.partial(_attn_kernel, lc=lc, group=group)
    return pl.pallas_call(
        kern,
        grid=(bsz, A_KV_HEADS, nseg),
        in_specs=[
            pl.BlockSpec((lc, group * HEAD_DIM), lambda b, g, i: (b * nseg + i, q_col0 + g)),
            pl.BlockSpec((seg, HEAD_DIM), lambda b, g, i: (b, g)),
            pl.BlockSpec((seg, HEAD_DIM), lambda b, g, i: (b, kv_w + g)),
            pl.BlockSpec((lc, HEAD_DIM), lambda b, g, i: (i, 0)),
            pl.BlockSpec((lc, HEAD_DIM), lambda b, g, i: (i, 0)),
            pl.BlockSpec((seg, HEAD_DIM), lambda b, g, i: (0, 0)),
            pl.BlockSpec((seg, HEAD_DIM), lambda b, g, i: (0, 0)),
            pl.BlockSpec((1, HEAD_DIM), lambda b, g, i: (0, 0)),
            pl.BlockSpec((1, HEAD_DIM), lambda b, g, i: (0, 0)),
        ],
        out_specs=pl.BlockSpec((lc, group * HEAD_DIM), lambda b, g, i: (b * nseg + i, g)),
        out_shape=jax.ShapeDtypeStruct((t, A_Q_HEADS * HEAD_DIM), BF16),
        scratch_shapes=[pltpu.VMEM((seg, HEAD_DIM), BF16), pltpu.VMEM((seg, HEAD_DIM), BF16)],
        compiler_params=_cp(("arbitrary", "arbitrary", "arbitrary")),
        name="gqa_attention",
    )(proj, proj, proj, cos2, sin2, cos2, sin2, q_gain, k_gain)


def _ret_kernel(lg_ref, q_ref, k_ref, v_ref, g_ref, cos_ref, sin_ref, gn_ref, o_ref,
                oacc, qf_s, qb_s, uf_s, ub_s, sf_s, sb_s, *, c, nchunk, nctx):
    h = pl.program_id(1)
    lgf = lg_ref[0, h]
    lgb = lg_ref[1, h]
    cf = float(c)
    n_i = lax.broadcasted_iota(jnp.int32, (c, 1), 0).astype(F32)
    m_i = lax.broadcasted_iota(jnp.int32, (1, c), 1).astype(F32)
    diff = n_i - m_i
    dmat = (jnp.where(diff >= 0, jnp.exp(lgf * jnp.maximum(diff, 0.0)), 0.0)
            + jnp.where(diff <= 0, jnp.exp(lgb * jnp.maximum(-diff, 0.0)), 0.0))
    qdf = jnp.exp(lgf * (n_i + 1.0))
    qdb = jnp.exp(lgb * (cf - n_i))
    kdf = jnp.exp(lgf * (cf - 1.0 - n_i))
    kdb = jnp.exp(lgb * n_i)
    cdf = jnp.exp(lgf * cf)
    cdb = jnp.exp(lgb * cf)
    kscale = HEAD_DIM ** -0.5

    def phase_a(ci, carry):
        r0 = pl.multiple_of(ci * c, c)
        rows = pl.ds(r0, c)
        cos2 = cos_ref[rows, :]
        sin2 = sin_ref[rows, :]
        q = _rope(q_ref[rows, :], cos2, sin2)
        k = _rope(k_ref[rows, :], cos2, sin2) * kscale
        vb = v_ref[rows, :].astype(BF16)
        s = _dot_nt(q.astype(BF16), k.astype(BF16)) * dmat
        oacc[rows, :] = _dot(s.astype(BF16), vb)
        qf_s[rows, :] = (q * qdf).astype(BF16)
        qb_s[rows, :] = (q * qdb).astype(BF16)
        uf_s[ci] = _dot_tn((k * kdf).astype(BF16), vb)
        ub_s[ci] = _dot_tn((k * kdb).astype(BF16), vb)
        return carry

    unroll = 3 if nchunk % 3 == 0 else 1
    lax.fori_loop(0, nchunk, phase_a, 0, unroll=unroll)

    def scan_f(ci, s):
        sf_s[ci] = s.astype(BF16)
        return s * cdf + uf_s[ci]

    lax.fori_loop(0, nchunk, scan_f, jnp.zeros((HEAD_DIM, HEAD_DIM), F32))

    def scan_b(i, s):
        ci = jnp.where(i < nctx, nctx - 1 - i, nchunk - 1 - (i - nctx))
        sb_s[ci] = s.astype(BF16)
        return s * cdb + ub_s[ci]

    lax.fori_loop(0, nchunk, scan_b, jnp.zeros((HEAD_DIM, HEAD_DIM), F32))

    def phase_c(ci, carry):
        r0 = pl.multiple_of(ci * c, c)
        rows = pl.ds(r0, c)
        o = oacc[rows, :] + _dot(qf_s[rows, :], sf_s[ci]) + _dot(qb_s[rows, :], sb_s[ci])
        mu = jnp.mean(o, axis=-1, keepdims=True)
        d = o - mu
        on = d * lax.rsqrt(jnp.mean(d * d, axis=-1, keepdims=True) + NORM_EPS)
        g = g_ref[rows, :]
        o_ref[rows, :] = (on * gn_ref[...] * (g * _sigmoid(g))).astype(o_ref.dtype)
        return carry

    lax.fori_loop(0, nchunk, phase_c, 0, unroll=unroll)


def _retention(proj, log_g, cos2, sin2, gn_gain, bsz, seg, lc):
    t = proj.shape[0]
    nchunk = seg // lc
    kvw = 2 * A_KV_HEADS
    k0, v0 = kvw, kvw + B_HEADS
    q0 = kvw + 2 * B_HEADS + A_Q_HEADS
    g0 = q0 + B_HEADS
    kern = functools.partial(_ret_kernel, c=lc, nchunk=nchunk, nctx=1)
    col = lambda c0: pl.BlockSpec((seg, HEAD_DIM), lambda b, h: (b, c0 + h))
    return pl.pallas_call(
        kern,
        grid=(bsz, B_HEADS),
        in_specs=[
            pl.BlockSpec(memory_space=pltpu.SMEM),
            col(q0), col(k0), col(v0), col(g0),
            pl.BlockSpec((seg, HEAD_DIM), lambda b, h: (0, 0)),
            pl.BlockSpec((seg, HEAD_DIM), lambda b, h: (0, 0)),
            pl.BlockSpec((1, HEAD_DIM), lambda b, h: (0, h)),
        ],
        out_specs=pl.BlockSpec((seg, HEAD_DIM), lambda b, h: (b, h)),
        out_shape=jax.ShapeDtypeStruct((t, B_HEADS * HEAD_DIM), BF16),
        scratch_shapes=[
            pltpu.VMEM((seg, HEAD_DIM), F32),
            pltpu.VMEM((seg, HEAD_DIM), BF16),
            pltpu.VMEM((seg, HEAD_DIM), BF16),
            pltpu.VMEM((nchunk, HEAD_DIM, HEAD_DIM), F32),
            pltpu.VMEM((nchunk, HEAD_DIM, HEAD_DIM), F32),
            pltpu.VMEM((nchunk, HEAD_DIM, HEAD_DIM), BF16),
            pltpu.VMEM((nchunk, HEAD_DIM, HEAD_DIM), BF16),
        ],
        compiler_params=_cp(("arbitrary", "arbitrary")),
        name="retention",
    )(log_g, proj, proj, proj, proj, cos2, sin2, gn_gain)


def _split2(x):
    hi = x.astype(BF16)
    lo = (x - hi.astype(F32)).astype(BF16)
    return hi, lo


def _hgrn_kernel(ff_ref, fb_ref, v_ref, q_ref, g_ref, lb_ref, gn_ref, o_ref,
                 oacc, q_s, u_s, d_s, st_s, *, c, nchunk, nctx):
    g = 4 if nchunk % 4 == 0 else (2 if nchunk % 2 == 0 else 1)
    tc = g * c
    lb = lb_ref[...]
    one_m_lb = 1.0 - lb
    n_i = lax.broadcasted_iota(jnp.int32, (tc, tc), 0)
    m_i = lax.broadcasted_iota(jnp.int32, (tc, tc), 1)
    shift = c.bit_length() - 1
    same = jnp.right_shift(n_i, shift) == jnp.right_shift(m_i, shift)
    lower = same & (n_i >= m_i)
    upper = same & (m_i >= n_i)
    tri_l = lower.astype(BF16)
    tri_u = upper.astype(BF16)
    mid = c // 2

    def csum(tri, x):
        hi, lo = _split2(x)
        return _dot(tri, hi) + _dot(tri, lo)

    def chunk_row(x, row):
        return jnp.concatenate(
            [jnp.broadcast_to(x[j * c + row:j * c + row + 1, :], (c, HEAD_DIM)) for j in range(g)], axis=0)

    def phase_a(ti, carry):
        r0 = pl.multiple_of(ti * tc, tc)
        rows = pl.ds(r0, tc)
        qr = q_ref[rows, :]
        q = qr * _sigmoid(qr)
        vb = v_ref[rows, :].astype(BF16)
        frf = ff_ref[rows, :]
        frb = fb_ref[rows, :]
        sgf = _sigmoid(frf)
        sgb = _sigmoid(frb)
        kf = one_m_lb * (1.0 - sgf)
        kb = one_m_lb * (1.0 - sgb)
        lff = jnp.log(lb + one_m_lb * sgf)
        lfb = jnp.log(lb + one_m_lb * sgb)
        cum_f = csum(tri_l, lff)
        cum_b = csum(tri_u, lfb)
        an_f = chunk_row(cum_f, mid)
        an_b = chunk_row(cum_b, mid)
        a_f = _dot_nt((q * jnp.exp(cum_f - an_f)).astype(BF16), (kf * jnp.exp(an_f - cum_f)).astype(BF16))
        a_b = _dot_nt((q * jnp.exp(cum_b - an_b)).astype(BF16), (kb * jnp.exp(an_b - cum_b)).astype(BF16))
        a = jnp.where(lower, a_f, 0.0) + jnp.where(upper, a_b, 0.0)
        oacc[rows, :] = _dot(a.astype(BF16), vb)
        last_f = chunk_row(cum_f, c - 1)
        last_b = chunk_row(cum_b, 0)
        q_s[rows, 0:HEAD_DIM] = (q * jnp.exp(cum_f)).astype(BF16)
        q_s[rows, HEAD_DIM:] = (q * jnp.exp(cum_b)).astype(BF16)
        kh = jnp.concatenate([(kf * jnp.exp(last_f - cum_f)).astype(BF16),
                              (kb * jnp.exp(last_b - cum_b)).astype(BF16)], axis=1)
        for j in range(g):
            sl = slice(j * c, (j + 1) * c)
            ci = ti * g + j
            u_s[ci] = _dot_tn(vb[sl], kh[sl])
            d_s[ci] = jnp.concatenate([jnp.exp(last_f[j * c:j * c + 1, :]),
                                       jnp.exp(last_b[j * c:j * c + 1, :])], axis=1)
        return carry

    ntile = nchunk // g
    lax.fori_loop(0, ntile, phase_a, 0, unroll=3 if ntile % 3 == 0 else 1)
    unroll = 12 if nchunk % 12 == 0 else g

    def scan(i, carry):
        s_f, s_b = carry
        cb = jnp.where(i < nctx, nctx - 1 - i, nchunk - 1 - (i - nctx))
        st_s[i, :, 0:HEAD_DIM] = s_f.astype(BF16)
        st_s[cb, :, HEAD_DIM:] = s_b.astype(BF16)
        s_f = s_f * d_s[i][:, 0:HEAD_DIM] + u_s[i][:, 0:HEAD_DIM]
        s_b = s_b * d_s[cb][:, HEAD_DIM:] + u_s[cb][:, HEAD_DIM:]
        return s_f, s_b

    zero = jnp.zeros((HEAD_DIM, HEAD_DIM), F32)
    lax.fori_loop(0, nchunk, scan, (zero, zero))

    def phase_c(ci, carry):
        r0 = pl.multiple_of(ci * c, c)
        rows = pl.ds(r0, c)
        o = oacc[rows, :] + _dot_nt(q_s[rows, :], st_s[ci])
        on = _rms_rows(o)
        g = g_ref[rows, :]
        o_ref[rows, :] = (on * gn_ref[...] * (g * _sigmoid(g))).astype(o_ref.dtype)
        return carry

    lax.fori_loop(0, nchunk, phase_c, 0, unroll=unroll)


def _hgrn(proj, lb, gn_gain, bsz, seg, lc):
    t = proj.shape[0]
    c = GLA_CHUNK
    nchunk = seg // c
    nctx = lc // c
    kern = functools.partial(_hgrn_kernel, c=c, nchunk=nchunk, nctx=nctx)
    col = lambda c0: pl.BlockSpec((seg, HEAD_DIM), lambda b, h: (b, c0 + h))
    vec = pl.BlockSpec((1, HEAD_DIM), lambda b, h: (0, h))
    return pl.pallas_call(
        kern,
        grid=(bsz, C_HEADS),
        in_specs=[col(0), col(C_HEADS), col(2 * C_HEADS), col(3 * C_HEADS), col(4 * C_HEADS), vec, vec],
        out_specs=pl.BlockSpec((seg, HEAD_DIM), lambda b, h: (b, h)),
        out_shape=jax.ShapeDtypeStruct((t, C_HEADS * HEAD_DIM), BF16),
        scratch_shapes=[
            pltpu.VMEM((seg, HEAD_DIM), F32),
            pltpu.VMEM((seg, 2 * HEAD_DIM), BF16),
            pltpu.VMEM((nchunk, HEAD_DIM, 2 * HEAD_DIM), F32),
            pltpu.VMEM((nchunk, 1, 2 * HEAD_DIM), F32),
            pltpu.VMEM((nchunk, HEAD_DIM, 2 * HEAD_DIM), BF16),
        ],
        compiler_params=_cp(("arbitrary", "arbitrary")),
        name="hgrn2",
    )(proj, proj, proj, proj, proj, lb, gn_gain)


def _out_kernel(*refs, n_in):
    xs = refs[:n_in]
    ws = refs[n_in:2 * n_in]
    z_ref, gate_ref, gain_ref, sh_ref, sc_ref, wr_ref, br_ref = refs[2 * n_in:2 * n_in + 7]
    z_out, h_out, lg_out = refs[2 * n_in + 7:]
    o = _dot(xs[0][...], ws[0][...])
    for x_ref, w_ref in zip(xs[1:], ws[1:]):
        o = o + _dot(x_ref[...], w_ref[...])
    z1 = z_ref[...] + gate_ref[...] * o
    z_out[...] = z1
    h = _rms_rows(z1) * gain_ref[...]
    h = h * (1.0 + sc_ref[...]) + sh_ref[...]
    hb = h.astype(BF16)
    h_out[...] = hb
    lg_out[...] = _dot(hb, wr_ref[...]) + br_ref[...]


def _out_proj(xs, ws, z, mods, gain, wr, br, bsz, nseg, off, rb):
    d = z.shape[1]
    nout = nseg - off
    n_in = len(xs)
    npad = wr.shape[1]

    def rin(b, j):
        return (b * nseg + off + j, 0)

    def rout(b, j):
        return (b * nout + j, 0)

    def mod(which):
        return pl.BlockSpec((None, None, 1, d),
                            lambda b, j: (jnp.where(j + off == 0, bsz, b), which, 0, 0))

    in_specs = [pl.BlockSpec((rb, x.shape[1]), rin) for x in xs]
    in_specs += [pl.BlockSpec(w.shape, lambda b, j: (0, 0)) for w in ws]
    in_specs += [
        pl.BlockSpec((rb, d), rin),
        mod(2),
        pl.BlockSpec((1, d), lambda b, j: (0, 0)),
        mod(3), mod(4),
        pl.BlockSpec(wr.shape, lambda b, j: (0, 0)),
        pl.BlockSpec((1, npad), lambda b, j: (0, 0)),
    ]
    tm = bsz * nout * rb
    return pl.pallas_call(
        functools.partial(_out_kernel, n_in=n_in),
        grid=(bsz, nout),
        in_specs=in_specs,
        out_specs=[pl.BlockSpec((rb, d), rout), pl.BlockSpec((rb, d), rout), pl.BlockSpec((rb, npad), rout)],
        out_shape=[jax.ShapeDtypeStruct((tm, d), F32), jax.ShapeDtypeStruct((tm, d), BF16),
                   jax.ShapeDtypeStruct((tm, npad), F32)],
        compiler_params=_cp(("arbitrary", "arbitrary")),
        name="out_proj",
    )(*xs, *ws, z, mods, gain, mods, mods, wr, br)


def _topk_kernel(lg_ref, idx_ref, gate_ref):
    l = lg_ref[...]
    lane = lax.broadcasted_iota(jnp.int32, l.shape, 1)
    vals, idxs = [], []
    for _ in range(TOP_K):
        m = jnp.max(l, axis=-1, keepdims=True)
        i = jnp.min(jnp.where(l == m, lane, LANES), axis=-1, keepdims=True)
        vals.append(m)
        idxs.append(i)
        l = jnp.where(lane == i, -jnp.inf, l)
    es = [jnp.exp(v - vals[0]) for v in vals]
    den = es[0] + es[1] + es[2] + es[3]
    io = jnp.zeros(l.shape, jnp.int32)
    go = jnp.zeros(l.shape, F32)
    for k in range(TOP_K):
        io = jnp.where(lane == k, idxs[k], io)
        go = jnp.where(lane == k, es[k] / den, go)
    idx_ref[...] = io
    gate_ref[...] = go


def _topk(logits, tm=512):
    t, n = logits.shape
    tm = _tile(t, tm)
    return pl.pallas_call(
        _topk_kernel,
        grid=(t // tm,),
        in_specs=[pl.BlockSpec((tm, n), lambda i: (i, 0))],
        out_specs=[pl.BlockSpec((tm, n), lambda i: (i, 0)), pl.BlockSpec((tm, n), lambda i: (i, 0))],
        out_shape=[jax.ShapeDtypeStruct((t, n), jnp.int32), jax.ShapeDtypeStruct((t, n), F32)],
        compiler_params=_cp(("arbitrary",)),
        name="router_topk",
    )(logits)


def _gather_kernel(idx0_ref, idxn_ref, src_ref, out_ref, buf, gsem, osem, *, chunk):
    i = pl.program_id(0)
    n = pl.num_programs(0)
    slot = i % 2
    other = 1 - slot

    def out_copy(step, s):
        return pltpu.make_async_copy(buf.at[s], out_ref.at[pl.ds(step * chunk, chunk)], osem.at[s])

    def issue_rows(idx_ref, s):
        def body(p, carry):
            r = 2 * p
            pltpu.make_async_copy(src_ref.at[idx_ref[0, r]], buf.at[s, r], gsem.at[s]).start(priority=0)
            pltpu.make_async_copy(src_ref.at[idx_ref[0, r + 1]], buf.at[s, r + 1], gsem.at[s]).start(priority=1)
            return carry

        lax.fori_loop(0, chunk // 2, body, 0, unroll=4)

    @pl.when(i == 0)
    def _first():
        issue_rows(idx0_ref, 0)

    @pl.when(i + 1 < n)
    def _next():
        @pl.when(i >= 1)
        def _free_slot():
            out_copy(i - 1, other).wait()

        issue_rows(idxn_ref, other)

    pltpu.make_async_copy(src_ref.at[pl.ds(0, chunk)], buf.at[slot], gsem.at[slot]).wait()
    out_copy(i, slot).start()

    @pl.when(i == n - 1)
    def _flush():
        out_copy(i, slot).wait()

        @pl.when(i >= 1)
        def _prev():
            out_copy(i - 1, other).wait()


def _gather_rows(src, idx, chunk=GATHER_CHUNK):
    r = idx.shape[0]
    d = src.shape[1:]
    nchunks = r // chunk
    assert src.shape[0] >= chunk
    idx3 = idx.reshape(nchunks, 1, chunk)
    return pl.pallas_call(
        functools.partial(_gather_kernel, chunk=chunk),
        grid=(nchunks,),
        in_specs=[
            pl.BlockSpec((None, 1, chunk), lambda i: (0, 0, 0), memory_space=pltpu.SMEM),
            pl.BlockSpec((None, 1, chunk), lambda i: (jnp.minimum(i + 1, nchunks - 1), 0, 0),
                         memory_space=pltpu.SMEM),
            pl.BlockSpec(memory_space=pl.ANY),
        ],
        out_specs=pl.BlockSpec(memory_space=pl.ANY),
        out_shape=jax.ShapeDtypeStruct((r,) + d, src.dtype),
        scratch_shapes=[pltpu.VMEM((2, chunk) + d, src.dtype), pltpu.SemaphoreType.DMA((2,)),
                        pltpu.SemaphoreType.DMA((2,))],
        compiler_params=_cp(("arbitrary",)),
        name="row_gather",
    )(idx3, idx3, src)


def _scatter_kernel(dest_ref, src_ref, init_ref, out_ref, sem, *, chunk, fan):
    del init_ref

    def body(t, carry):
        for k in range(fan):
            pltpu.make_async_copy(src_ref.at[t], out_ref.at[dest_ref[0, t * fan + k]], sem).start(priority=k % 2)
        return carry

    lax.fori_loop(0, chunk, body, 0, unroll=2)
    for _ in range(fan):
        pltpu.make_async_copy(src_ref, out_ref.at[pl.ds(0, chunk)], sem).wait()


def _scatter_rows(src, dest, n_out, fan, chunk=GATHER_CHUNK):
    s = src.shape[0]
    d = src.shape[1:]
    chunk = _tile(s, chunk)
    nchunks = s // chunk
    assert n_out >= chunk
    return pl.pallas_call(
        functools.partial(_scatter_kernel, chunk=chunk, fan=fan),
        grid=(nchunks,),
        in_specs=[
            pl.BlockSpec((None, 1, chunk * fan), lambda i: (i, 0, 0), memory_space=pltpu.SMEM),
            pl.BlockSpec((chunk,) + d, lambda i: (i, 0, 0)),
            pl.BlockSpec(memory_space=pl.ANY),
        ],
        out_specs=pl.BlockSpec(memory_space=pl.ANY),
        out_shape=jax.ShapeDtypeStruct((n_out,) + d, src.dtype),
        scratch_shapes=[pltpu.SemaphoreType.DMA],
        input_output_aliases={2: 0},
        compiler_params=_cp(("arbitrary",)),
        name="row_scatter",
    )(dest.reshape(nchunks, 1, chunk * fan), src, jnp.zeros((n_out,) + d, src.dtype))


def _gmm1_kernel(be_ref, first_ref, nused_ref, x_ref, w_ref, b_ref, o_ref, wsc, hsc):
    i = pl.program_id(1)
    nu = nused_ref[0]
    tn = w_ref.shape[1]

    def matmul(slot):
        hsc[slot] = _dot(x_ref[...], wsc[...]) + b_ref[...]

    def activate(slot):
        w2 = 2 * LANES
        bm = o_ref.shape[0]
        lane = lax.broadcasted_iota(jnp.int32, (bm, LANES), 1)
        even = (2 * lane) % LANES
        for s in range(tn // w2):
            parts = []
            for u in range(2):
                lo = s * w2 + u * LANES
                hid = hsc[slot, :, lo:lo + LANES]
                glu = jnp.minimum(hid, SWIGLU_LIMIT)
                glu = glu * _sigmoid(SWIGLU_ALPHA * glu)
                lin = jnp.clip(hid, -SWIGLU_LIMIT, SWIGLU_LIMIT) + 1.0
                prod = glu * pltpu.roll(lin, LANES - 1, 1)
                parts.append(jnp.take_along_axis(prod, even, axis=1))
            o_ref[:, s * LANES:(s + 1) * LANES] = jnp.where(
                lane < LANES // 2, parts[0], parts[1]).astype(o_ref.dtype)

    @pl.when((i < nu) & (first_ref[jnp.minimum(i, nu - 1)] == 1))
    def _cast():
        wsc[...] = w_ref[...].astype(BF16)

    @pl.when((i == 0) & (nu > 0))
    def _head():
        matmul(0)

    for par in range(2):
        @pl.when((i >= 1) & (i < nu) & (i % 2 == par))
        def _steady():
            matmul(par)
            activate(1 - par)

        @pl.when((i >= 1) & (i == nu) & (i % 2 == par))
        def _tail():
            activate(1 - par)

    @pl.when(i > nu)
    def _unused():
        o_ref[...] = jnp.zeros(o_ref.shape, o_ref.dtype)


def _gmm2_kernel(be_ref, first_ref, nused_ref, x_ref, w_ref, b_ref, o_ref, wsc):
    i = pl.program_id(1)

    @pl.when(i < nused_ref[0])
    def _():
        @pl.when(first_ref[i] == 1)
        def _cast():
            wsc[...] = w_ref[...].astype(BF16)

        o_ref[...] = (_dot(x_ref[...], wsc[...]) + b_ref[...]).astype(o_ref.dtype)

    @pl.when(i >= nused_ref[0])
    def _unused():
        o_ref[...] = jnp.zeros(o_ref.shape, o_ref.dtype)


def _gmm(kernel, x, w, b, layer, be, first, nused, bm, tn, out_cols, out_dtype, name, skew=False):
    r, k = x.shape
    n = w.shape[3]
    tn = _tile(n, tn)
    nb = r // bm
    n_tiles = n // tn
    oc = out_cols // n_tiles
    lag = 1 if skew else 0

    def blk(i, nu):
        return jnp.minimum(i, nu[0] - 1)

    scratch = [pltpu.VMEM((k, tn), BF16)]
    if skew:
        scratch.append(pltpu.VMEM((2, bm, tn), F32))
    grid_spec = pltpu.PrefetchScalarGridSpec(
        num_scalar_prefetch=3,
        grid=(n_tiles, nb + lag),
        in_specs=[
            pl.BlockSpec((bm, k), lambda j, i, be, fi, nu: (blk(i, nu), 0)),
            pl.BlockSpec((None, None, k, tn), lambda j, i, be, fi, nu: (layer, be[blk(i, nu)], 0, j)),
            pl.BlockSpec((None, None, 1, tn), lambda j, i, be, fi, nu: (layer, be[blk(i, nu)], 0, j)),
        ],
        out_specs=pl.BlockSpec((bm, oc), lambda j, i, be, fi, nu: (jnp.maximum(i - lag, 0), j)),
        scratch_shapes=scratch,
    )
    return pl.pallas_call(
        kernel,
        grid_spec=grid_spec,
        out_shape=jax.ShapeDtypeStruct((r, out_cols), out_dtype),
        compiler_params=_cp(("arbitrary", "arbitrary")),
        name=name,
    )(be, first, nused, x, w, b.reshape(b.shape[0], b.shape[1], 1, n))


def _combine_kernel(z_ref, y0_ref, y1_ref, y2_ref, y3_ref, gt_ref, gate_ref, gain_ref, sh_ref, sc_ref,
                    *outs, final):
    gt = gt_ref[...]
    ffn = gt[:, 0:1] * y0_ref[...].astype(F32)
    for k, y_ref in enumerate((y1_ref, y2_ref, y3_ref), start=1):
        ffn = ffn + gt[:, k:k + 1] * y_ref[...].astype(F32)
    z2 = z_ref[...] + gate_ref[...] * ffn
    h = _rms_rows(z2) * gain_ref[...]
    if final:
        outs[0][...] = h
    else:
        outs[0][...] = z2
        outs[1][...] = (h * (1.0 + sc_ref[...]) + sh_ref[...]).astype(outs[1].dtype)


def _combine(z1, yt, gates, mods_cur, mods_next, gain, bsz, nblk, rb, final):
    t, d = z1.shape
    has_ctx = not final

    def midx(i):
        if has_ctx:
            return jnp.where(i % nblk == 0, bsz, i // nblk)
        return i // nblk

    row = lambda i: (i, 0)
    nrow = t // rb
    out_specs = [pl.BlockSpec((rb, d), row)]
    out_shape = [jax.ShapeDtypeStruct((t, d), F32)]
    if not final:
        out_specs.append(pl.BlockSpec((rb, d), row))
        out_shape.append(jax.ShapeDtypeStruct((t, d), BF16))
    y_specs = [pl.BlockSpec((rb, d), functools.partial(lambda i, k: (k * nrow + i, 0), k=k))
               for k in range(TOP_K)]
    return pl.pallas_call(
        functools.partial(_combine_kernel, final=final),
        grid=(nrow,),
        in_specs=[
            pl.BlockSpec((rb, d), row),
            *y_specs,
            pl.BlockSpec((rb, gates.shape[1]), row),
            pl.BlockSpec((None, None, 1, d), lambda i: (midx(i), 5, 0, 0)),
            pl.BlockSpec((1, d), lambda i: (0, 0)),
            pl.BlockSpec((None, None, 1, d), lambda i: (midx(i), 0, 0, 0)),
            pl.BlockSpec((None, None, 1, d), lambda i: (midx(i), 1, 0, 0)),
        ],
        out_specs=out_specs,
        out_shape=out_shape,
        compiler_params=_cp(("arbitrary",)),
        name="moe_combine",
    )(z1, yt, yt, yt, yt, gates, mods_cur, gain, mods_next, mods_next)


def _routing(idx, n_exp, bm, gchunk):
    tm = idx.shape[0]
    onehot = (idx[:, :, None] == jnp.arange(n_exp, dtype=jnp.int32)).astype(jnp.int32).sum(axis=1)
    csum = jnp.cumsum(onehot, axis=0)
    counts = csum[-1]
    rank = jnp.take_along_axis(csum - onehot, idx, axis=1)
    padded = (counts + bm - 1) // bm * bm
    pad_end = jnp.cumsum(padded)
    pad_start = pad_end - padded
    dest = pad_start[idx] + rank
    nb = (tm * TOP_K + n_exp * (bm - 1)) // bm
    nb = -(-(nb * bm) // gchunk) * gchunk // bm
    nused = (pad_end[-1] // bm).astype(jnp.int32).reshape(1)
    block_start = jnp.arange(nb, dtype=jnp.int32) * bm
    be = jnp.minimum((pad_end[None, :] <= block_start[:, None]).astype(jnp.int32).sum(axis=1), n_exp - 1)
    first = jnp.concatenate([jnp.ones((1,), jnp.int32), (be[1:] != be[:-1]).astype(jnp.int32)])
    dest = dest.astype(jnp.int32)
    dest_kmajor = dest.T.reshape(-1)
    return dest.reshape(-1), dest_kmajor, nb * bm, be, first, nused


def _moe(h2, logits, w1, b1, w2, b2, layer):
    tm, d = h2.shape
    n_exp = w1.shape[1]
    sub = d // LANES
    idx_p, gate_p = _topk(logits)
    idx = idx_p[:, :TOP_K]
    dest_tmajor, dest, n_rows, be, first, nused = _routing(idx, n_exp, MOE_BM, GATHER_CHUNK)
    xs = _scatter_rows(h2.reshape(tm, sub, LANES), dest_tmajor, n_rows, TOP_K).reshape(-1, d)
    act = _gmm(_gmm1_kernel, xs, w1, b1, layer, be, first, nused, MOE_BM, 1024, w1.shape[3] // 2, BF16, "moe_up",
               skew=True)
    y = _gmm(_gmm2_kernel, act, w2, b2, layer, be, first, nused, MOE_BM, 2048, w2.shape[3], BF16, "moe_down")
    yt = _gather_rows(y.reshape(-1, sub, LANES), dest).reshape(TOP_K * tm, d)
    return yt, gate_p


def _rope_tables(n_lat, lc):
    n_rows = n_lat // GRID_W
    row = jnp.repeat(jnp.arange(n_rows, dtype=F32), GRID_W)
    col = jnp.tile(jnp.arange(GRID_W, dtype=F32), n_rows)
    n_freq = HEAD_DIM // 4
    inv_freq = ROPE_THETA ** (-jnp.arange(n_freq, dtype=F32) / n_freq)
    ang = jnp.concatenate([row[:, None] * inv_freq, col[:, None] * inv_freq], axis=-1)
    cos, sin = jnp.cos(ang), jnp.sin(ang)
    cos2 = jnp.concatenate([cos, cos], axis=-1)
    sin2 = jnp.concatenate([-sin, sin], axis=-1)
    cos2 = jnp.concatenate([jnp.ones((lc, HEAD_DIM), F32), cos2], axis=0)
    sin2 = jnp.concatenate([jnp.zeros((lc, HEAD_DIM), F32), sin2], axis=0)
    return cos2, sin2


def kernel(x, c, ctx, c_ctx, ada_w, ada_b, norm_mix, norm_ffn, ab_w_in, ab_w_out, a_q_norm, a_k_norm, b_decay_exp, b_gn, c_w_in, c_w_out, c_lb, c_gn, router_w, router_b, exp_w1, exp_b1, exp_w2, exp_b2, norm_final):
    bsz, n_lat, d = x.shape
    lc = ctx.shape[1]
    depth = ada_w.shape[0]
    assert depth == 2 and n_lat % lc == 0 and bsz < MOD_ROWS and lc % GLA_CHUNK == 0
    seg = lc + n_lat
    nseg = seg // lc
    n_exp = router_w.shape[2]

    z = jnp.concatenate([ctx, x], axis=1).reshape(bsz * seg, d)

    cond = jnp.zeros((MOD_ROWS, d), F32).at[:bsz].set(c).at[bsz].set(c_ctx)
    mods = _ada(cond, ada_w, ada_b).reshape(depth, MOD_ROWS, N_MOD, 1, d)
    cos2, sin2 = _rope_tables(n_lat, lc)

    lb_soft = jax.nn.softmax(c_lb.astype(F32), axis=0)
    lower_bounds = jnp.cumsum(lb_soft, axis=0) - lb_soft[0]
    log_g = jnp.log1p(-jnp.exp2(-b_decay_exp[0].astype(F32)))

    wr = jnp.zeros((depth, d, LANES), BF16).at[:, :, :n_exp].set(router_w.astype(BF16))
    br = jnp.full((depth, 1, LANES), -1e30, F32).at[:, 0, :n_exp].set(router_b)

    hz = _norm_mod(z, norm_mix[0:1], mods[0], nseg, bsz, lc)
    proj = _matmul(hz, ab_w_in[0].astype(BF16), 1024, 512, name="in_proj0")
    att = _attention(proj, cos2, sin2, a_q_norm[0:1], a_k_norm[0:1], bsz, seg, lc)
    ret = _retention(proj, log_g, cos2, sin2, b_gn[0:1], bsz, seg, lc)
    w_out = ab_w_out[0].astype(BF16)
    aw = A_Q_HEADS * HEAD_DIM
    z1, h2, logits = _out_proj([att, ret], [w_out[:aw], w_out[aw:]], z, mods[0], norm_ffn[0:1],
                               wr[0], br[0], bsz, nseg, 0, lc)
    yt, gates = _moe(h2, logits, exp_w1, exp_b1, exp_w2, exp_b2, 0)
    z2, hz = _combine(z1, yt, gates, mods[0], mods[1], norm_mix[1:2], bsz, nseg, lc, final=False)

    proj = _matmul(hz, c_w_in[0].astype(BF16), 1024, 1024, name="in_proj1")
    hg = _hgrn(proj, lower_bounds[1:2], c_gn[0:1], bsz, seg, lc)
    z1, h2, logits = _out_proj([hg], [c_w_out[0].astype(BF16)], z2, mods[1], norm_ffn[1:2],
                               wr[1], br[1], bsz, nseg, 1, lc)
    yt, gates = _moe(h2, logits, exp_w1, exp_b1, exp_w2, exp_b2, 1)
    (out,) = _combine(z1, yt, gates, mods[1], mods[1], norm_final.reshape(1, d), bsz, nseg - 1, lc, final=True)
    return out.reshape(bsz, n_lat, d)
```

```python
import functools

import jax
import jax.numpy as jnp
from jax import lax
from jax.experimental import pallas as pl
from jax.experimental.pallas import tpu as pltpu

HEAD_DIM = 128
GRID_W = 64
ROPE_THETA = 10000.0
NORM_EPS = 1e-6
N_MOD = 6
A_Q_HEADS = 8
A_KV_HEADS = 2
B_HEADS = 8
C_HEADS = 16
TOP_K = 4
SWIGLU_ALPHA = 1.702
SWIGLU_LIMIT = 7.0

LANES = 128
MOD_ROWS = 16
GLA_CHUNK = 64
MOE_BM = 512
GATHER_CHUNK = 512
VMEM_LIMIT = 56 * 1024 * 1024

LOG2E = 1.4426950408889634

F32 = jnp.float32
BF16 = jnp.bfloat16


def _cp(sem, vmem=VMEM_LIMIT):
    return pltpu.CompilerParams(dimension_semantics=sem, vmem_limit_bytes=vmem)


def _tile(n, pref):
    t = pref
    while n % t:
        t //= 2
    return t


def _dot(a, b):
    return jnp.dot(a, b, preferred_element_type=F32)


def _dot_nt(a, b):
    return lax.dot_general(a, b, (((1,), (1,)), ((), ())), preferred_element_type=F32)


def _dot_tn(a, b):
    return lax.dot_general(a, b, (((0,), (0,)), ((), ())), preferred_element_type=F32)


def _sigmoid(x):
    return 1.0 / (1.0 + jnp.exp(-x))


def _rope(t, cos2, sin2):
    return t * cos2 + pltpu.roll(t, HEAD_DIM // 2, 1) * sin2


def _rms_rows(x):
    return x * lax.rsqrt(jnp.mean(x * x, axis=-1, keepdims=True) + NORM_EPS)


def _ada_kernel(c_ref, w_ref, b_ref, o_ref):
    c = c_ref[...]
    s = (c * _sigmoid(c)).astype(BF16)
    o_ref[...] = _dot(s, w_ref[...].astype(BF16)) + b_ref[...]


def _ada(cond, ada_w, ada_b, tn=1024):
    depth, d, n = ada_w.shape
    return pl.pallas_call(
        _ada_kernel,
        grid=(depth, n // tn),
        in_specs=[
            pl.BlockSpec((MOD_ROWS, d), lambda l, j: (0, 0)),
            pl.BlockSpec((None, d, tn), lambda l, j: (l, 0, j)),
            pl.BlockSpec((None, 1, tn), lambda l, j: (l, 0, j)),
        ],
        out_specs=pl.BlockSpec((None, MOD_ROWS, tn), lambda l, j: (l, 0, j)),
        out_shape=jax.ShapeDtypeStruct((depth, MOD_ROWS, n), F32),
        compiler_params=_cp(("arbitrary", "arbitrary")),
        name="ada_mod",
    )(cond, ada_w, ada_b.reshape(depth, 1, n))


def _norm_mod_kernel(z_ref, g_ref, sh_ref, sc_ref, o_ref):
    y = _rms_rows(z_ref[...]) * g_ref[...]
    o_ref[...] = (y * (1.0 + sc_ref[...]) + sh_ref[...]).astype(o_ref.dtype)


def _norm_mod(z, gain, mods, nseg, bsz, rb):
    t, d = z.shape

    def midx(i):
        return jnp.where(i % nseg == 0, bsz, i // nseg)

    return pl.pallas_call(
        _norm_mod_kernel,
        grid=(t // rb,),
        in_specs=[
            pl.BlockSpec((rb, d), lambda i: (i, 0)),
            pl.BlockSpec((1, d), lambda i: (0, 0)),
            pl.BlockSpec((None, None, 1, d), lambda i: (midx(i), 0, 0, 0)),
            pl.BlockSpec((None, None, 1, d), lambda i: (midx(i), 1, 0, 0)),
        ],
        out_specs=pl.BlockSpec((rb, d), lambda i: (i, 0)),
        out_shape=jax.ShapeDtypeStruct((t, d), BF16),
        compiler_params=_cp(("arbitrary",)),
        name="norm_mod",
    )(z, gain, mods, mods)


def _mm_kernel(x_ref, w_ref, o_ref):
    o_ref[...] = _dot(x_ref[...], w_ref[...]).astype(o_ref.dtype)


def _matmul(x, w, tm, tn, out_dtype=F32, name="matmul"):
    m, k = x.shape
    n = w.shape[1]
    tm, tn = _tile(m, tm), _tile(n, tn)
    return pl.pallas_call(
        _mm_kernel,
        grid=(m // tm, n // tn),
        in_specs=[
            pl.BlockSpec((tm, k), lambda i, j: (i, 0)),
            pl.BlockSpec((k, tn), lambda i, j: (0, j)),
        ],
        out_specs=pl.BlockSpec((tm, tn), lambda i, j: (i, j)),
        out_shape=jax.ShapeDtypeStruct((m, n), out_dtype),
        compiler_params=_cp(("arbitrary", "arbitrary")),
        name=name,
    )(x, w)


def _attn_kernel(q_ref, k_ref, v_ref, cosq_ref, sinq_ref, cosk_ref, sink_ref,
                 qg_ref, kg_ref, o_ref, ks, vs, *, lc, group):
    qi = pl.program_id(2)
    rb = q_ref.shape[0]
    scale = HEAD_DIM ** -0.5

    @pl.when(qi == 0)
    def _prep():
        kn = _rms_rows(k_ref[...]) * kg_ref[...]
        ks[...] = _rope(kn, cosk_ref[...], sink_ref[...]).astype(BF16)
        vs[...] = v_ref[...].astype(BF16)

    def attend(k, v):
        for r in range(group):
            cols = slice(r * HEAD_DIM, (r + 1) * HEAD_DIM)
            qh = _rms_rows(q_ref[:, cols]) * qg_ref[...]
            qh = _rope(qh, cosq_ref[...], sinq_ref[...]).astype(BF16)
            s = _dot_nt(qh, k)
            m = jnp.max(s, axis=-1, keepdims=True)
            p = jnp.exp2((s - m) * (scale * LOG2E))
            l = jnp.sum(p, axis=-1, keepdims=True)
            o_ref[:, cols] = (_dot(p.astype(BF16), v) / l).astype(o_ref.dtype)

    @pl.when(qi == 0)
    def _ctx():
        attend(ks[0:lc], vs[0:lc])

    @pl.when(qi > 0)
    def _lat():
        attend(ks[...], vs[...])


def _attention(proj, cos2, sin2, q_gain, k_gain, bsz, seg, lc):
    t = proj.shape[0]
    nseg = seg // lc
    group = A_Q_HEADS // A_KV_HEADS
    kv_w = A_KV_HEADS
    q_col0 = (2 * kv_w + 2 * B_HEADS) // group
    kern = functools.partial(_attn_kernel, lc=lc, group=group)
    return pl.pallas_call(
        kern,
        grid=(bsz, A_KV_HEADS, nseg),
        in_specs=[
            pl.BlockSpec((lc, group * HEAD_DIM), lambda b, g, i: (b * nseg + i, q_col0 + g)),
            pl.BlockSpec((seg, HEAD_DIM), lambda b, g, i: (b, g)),
            pl.BlockSpec((seg, HEAD_DIM), lambda b, g, i: (b, kv_w + g)),
            pl.BlockSpec((lc, HEAD_DIM), lambda b, g, i: (i, 0)),
            pl.BlockSpec((lc, HEAD_DIM), lambda b, g, i: (i, 0)),
            pl.BlockSpec((seg, HEAD_DIM), lambda b, g, i: (0, 0)),
            pl.BlockSpec((seg, HEAD_DIM), lambda b, g, i: (0, 0)),
            pl.BlockSpec((1, HEAD_DIM), lambda b, g, i: (0, 0)),
            pl.BlockSpec((1, HEAD_DIM), lambda b, g, i: (0, 0)),
        ],
        out_specs=pl.BlockSpec((lc, group * HEAD_DIM), lambda b, g, i: (b * nseg + i, g)),
        out_shape=jax.ShapeDtypeStruct((t, A_Q_HEADS * HEAD_DIM), BF16),
        scratch_shapes=[pltpu.VMEM((seg, HEAD_DIM), BF16), pltpu.VMEM((seg, HEAD_DIM), BF16)],
        compiler_params=_cp(("arbitrary", "arbitrary", "arbitrary")),
        name="gqa_attention",
    )(proj, proj, proj, cos2, sin2, cos2, sin2, q_gain, k_gain)


def _ret_kernel(lg_ref, q_ref, k_ref, v_ref, g_ref, cos_ref, sin_ref, gn_ref, o_ref,
                oacc, qf_s, qb_s, uf_s, ub_s, sf_s, sb_s, *, c, nchunk, nctx):
    h = pl.program_id(1)
    lgf = lg_ref[0, h]
    lgb = lg_ref[1, h]
    cf = float(c)
    n_i = lax.broadcasted_iota(jnp.int32, (c, 1), 0).astype(F32)
    m_i = lax.broadcasted_iota(jnp.int32, (1, c), 1).astype(F32)
    diff = n_i - m_i
    dmat = (jnp.where(diff >= 0, jnp.exp(lgf * jnp.maximum(diff, 0.0)), 0.0)
            + jnp.where(diff <= 0, jnp.exp(lgb * jnp.maximum(-diff, 0.0)), 0.0))
    qdf = jnp.exp(lgf * (n_i + 1.0))
    qdb = jnp.exp(lgb * (cf - n_i))
    kdf = jnp.exp(lgf * (cf - 1.0 - n_i))
    kdb = jnp.exp(lgb * n_i)
    cdf = jnp.exp(lgf * cf)
    cdb = jnp.exp(lgb * cf)
    kscale = HEAD_DIM ** -0.5

    def phase_a(ci, carry):
        r0 = pl.multiple_of(ci * c, c)
        rows = pl.ds(r0, c)
        cos2 = cos_ref[rows, :]
        sin2 = sin_ref[rows, :]
        q = _rope(q_ref[rows, :], cos2, sin2)
        k = _rope(k_ref[rows, :], cos2, sin2) * kscale
        vb = v_ref[rows, :].astype(BF16)
        s = _dot_nt(q.astype(BF16), k.astype(BF16)) * dmat
        oacc[rows, :] = _dot(s.astype(BF16), vb)
        qf_s[rows, :] = (q * qdf).astype(BF16)
        qb_s[rows, :] = (q * qdb).astype(BF16)
        uf_s[ci] = _dot_tn((k * kdf).astype(BF16), vb)
        ub_s[ci] = _dot_tn((k * kdb).astype(BF16), vb)
        return carry

    unroll = 3 if nchunk % 3 == 0 else 1
    lax.fori_loop(0, nchunk, phase_a, 0, unroll=unroll)

    def scan_f(ci, s):
        sf_s[ci] = s.astype(BF16)
        return s * cdf + uf_s[ci]

    lax.fori_loop(0, nchunk, scan_f, jnp.zeros((HEAD_DIM, HEAD_DIM), F32))

    def scan_b(i, s):
        ci = jnp.where(i < nctx, nctx - 1 - i, nchunk - 1 - (i - nctx))
        sb_s[ci] = s.astype(BF16)
        return s * cdb + ub_s[ci]

    lax.fori_loop(0, nchunk, scan_b, jnp.zeros((HEAD_DIM, HEAD_DIM), F32))

    def phase_c(ci, carry):
        r0 = pl.multiple_of(ci * c, c)
        rows = pl.ds(r0, c)
        o = oacc[rows, :] + _dot(qf_s[rows, :], sf_s[ci]) + _dot(qb_s[rows, :], sb_s[ci])
        mu = jnp.mean(o, axis=-1, keepdims=True)
        d = o - mu
        on = d * lax.rsqrt(jnp.mean(d * d, axis=-1, keepdims=True) + NORM_EPS)
        g = g_ref[rows, :]
        o_ref[rows, :] = (on * gn_ref[...] * (g * _sigmoid(g))).astype(o_ref.dtype)
        return carry

    lax.fori_loop(0, nchunk, phase_c, 0, unroll=unroll)


def _retention(proj, log_g, cos2, sin2, gn_gain, bsz, seg, lc):
    t = proj.shape[0]
    nchunk = seg // lc
    kvw = 2 * A_KV_HEADS
    k0, v0 = kvw, kvw + B_HEADS
    q0 = kvw + 2 * B_HEADS + A_Q_HEADS
    g0 = q0 + B_HEADS
    kern = functools.partial(_ret_kernel, c=lc, nchunk=nchunk, nctx=1)
    col = lambda c0: pl.BlockSpec((seg, HEAD_DIM), lambda b, h: (b, c0 + h))
    return pl.pallas_call(
        kern,
        grid=(bsz, B_HEADS),
        in_specs=[
            pl.BlockSpec(memory_space=pltpu.SMEM),
            col(q0), col(k0), col(v0), col(g0),
            pl.BlockSpec((seg, HEAD_DIM), lambda b, h: (0, 0)),
            pl.BlockSpec((seg, HEAD_DIM), lambda b, h: (0, 0)),
            pl.BlockSpec((1, HEAD_DIM), lambda b, h: (0, h)),
        ],
        out_specs=pl.BlockSpec((seg, HEAD_DIM), lambda b, h: (b, h)),
        out_shape=jax.ShapeDtypeStruct((t, B_HEADS * HEAD_DIM), BF16),
        scratch_shapes=[
            pltpu.VMEM((seg, HEAD_DIM), F32),
            pltpu.VMEM((seg, HEAD_DIM), BF16),
            pltpu.VMEM((seg, HEAD_DIM), BF16),
            pltpu.VMEM((nchunk, HEAD_DIM, HEAD_DIM), F32),
            pltpu.VMEM((nchunk, HEAD_DIM, HEAD_DIM), F32),
            pltpu.VMEM((nchunk, HEAD_DIM, HEAD_DIM), BF16),
            pltpu.VMEM((nchunk, HEAD_DIM, HEAD_DIM), BF16),
        ],
        compiler_params=_cp(("arbitrary", "arbitrary")),
        name="retention",
    )(log_g, proj, proj, proj, proj, cos2, sin2, gn_gain)


def _split2(x):
    hi = x.astype(BF16)
    lo = (x - hi.astype(F32)).astype(BF16)
    return hi, lo


def _hgrn_kernel(ff_ref, fb_ref, v_ref, q_ref, g_ref, lb_ref, gn_ref, o_ref,
                 oacc, q_s, u_s, d_s, st_s, *, c, nchunk, nctx):
    g = 4 if nchunk % 4 == 0 else (2 if nchunk % 2 == 0 else 1)
    tc = g * c
    lb = lb_ref[...]
    one_m_lb = 1.0 - lb
    n_i = lax.broadcasted_iota(jnp.int32, (tc, tc), 0)
    m_i = lax.broadcasted_iota(jnp.int32, (tc, tc), 1)
    shift = c.bit_length() - 1
    same = jnp.right_shift(n_i, shift) == jnp.right_shift(m_i, shift)
    lower = same & (n_i >= m_i)
    upper = same & (m_i >= n_i)
    tri_l = lower.astype(BF16)
    tri_u = upper.astype(BF16)
    mid = c // 2

    def csum(tri, x):
        hi, lo = _split2(x)
        return _dot(tri, hi) + _dot(tri, lo)

    def chunk_row(x, row):
        return jnp.concatenate(
            [jnp.broadcast_to(x[j * c + row:j * c + row + 1, :], (c, HEAD_DIM)) for j in range(g)], axis=0)

    def phase_a(ti, carry):
        r0 = pl.multiple_of(ti * tc, tc)
        rows = pl.ds(r0, tc)
        qr = q_ref[rows, :]
        q = qr * _sigmoid(qr)
        vb = v_ref[rows, :].astype(BF16)
        frf = ff_ref[rows, :]
        frb = fb_ref[rows, :]
        sgf = _sigmoid(frf)
        sgb = _sigmoid(frb)
        kf = one_m_lb * (1.0 - sgf)
        kb = one_m_lb * (1.0 - sgb)
        lff = jnp.log(lb + one_m_lb * sgf)
        lfb = jnp.log(lb + one_m_lb * sgb)
        cum_f = csum(tri_l, lff)
        cum_b = csum(tri_u, lfb)
        an_f = chunk_row(cum_f, mid)
        an_b = chunk_row(cum_b, mid)
        a_f = _dot_nt((q * jnp.exp(cum_f - an_f)).astype(BF16), (kf * jnp.exp(an_f - cum_f)).astype(BF16))
        a_b = _dot_nt((q * jnp.exp(cum_b - an_b)).astype(BF16), (kb * jnp.exp(an_b - cum_b)).astype(BF16))
        a = jnp.where(lower, a_f, 0.0) + jnp.where(upper, a_b, 0.0)
        oacc[rows, :] = _dot(a.astype(BF16), vb)
        last_f = chunk_row(cum_f, c - 1)
        last_b = chunk_row(cum_b, 0)
        q_s[rows, 0:HEAD_DIM] = (q * jnp.exp(cum_f)).astype(BF16)
        q_s[rows, HEAD_DIM:] = (q * jnp.exp(cum_b)).astype(BF16)
        kh = jnp.concatenate([(kf * jnp.exp(last_f - cum_f)).astype(BF16),
                              (kb * jnp.exp(last_b - cum_b)).astype(BF16)], axis=1)
        for j in range(g):
            sl = slice(j * c, (j + 1) * c)
            ci = ti * g + j
            u_s[ci] = _dot_tn(vb[sl], kh[sl])
            d_s[ci] = jnp.concatenate([jnp.exp(last_f[j * c:j * c + 1, :]),
                                       jnp.exp(last_b[j * c:j * c + 1, :])], axis=1)
        return carry

    ntile = nchunk // g
    lax.fori_loop(0, ntile, phase_a, 0, unroll=3 if ntile % 3 == 0 else 1)
    unroll = 12 if nchunk % 12 == 0 else g

    def scan(i, carry):
        s_f, s_b = carry
        cb = jnp.where(i < nctx, nctx - 1 - i, nchunk - 1 - (i - nctx))
        st_s[i, :, 0:HEAD_DIM] = s_f.astype(BF16)
        st_s[cb, :, HEAD_DIM:] = s_b.astype(BF16)
        s_f = s_f * d_s[i][:, 0:HEAD_DIM] + u_s[i][:, 0:HEAD_DIM]
        s_b = s_b * d_s[cb][:, HEAD_DIM:] + u_s[cb][:, HEAD_DIM:]
        return s_f, s_b

    zero = jnp.zeros((HEAD_DIM, HEAD_DIM), F32)
    lax.fori_loop(0, nchunk, scan, (zero, zero))

    def phase_c(ci, carry):
        r0 = pl.multiple_of(ci * c, c)
        rows = pl.ds(r0, c)
        o = oacc[rows, :] + _dot_nt(q_s[rows, :], st_s[ci])
        on = _rms_rows(o)
        g = g_ref[rows, :]
        o_ref[rows, :] = (on * gn_ref[...] * (g * _sigmoid(g))).astype(o_ref.dtype)
        return carry

    lax.fori_loop(0, nchunk, phase_c, 0, unroll=unroll)


def _hgrn(proj, lb, gn_gain, bsz, seg, lc):
    t = proj.shape[0]
    c = GLA_CHUNK
    nchunk = seg // c
    nctx = lc // c
    kern = functools.partial(_hgrn_kernel, c=c, nchunk=nchunk, nctx=nctx)
    col = lambda c0: pl.BlockSpec((seg, HEAD_DIM), lambda b, h: (b, c0 + h))
    vec = pl.BlockSpec((1, HEAD_DIM), lambda b, h: (0, h))
    return pl.pallas_call(
        kern,
        grid=(bsz, C_HEADS),
        in_specs=[col(0), col(C_HEADS), col(2 * C_HEADS), col(3 * C_HEADS), col(4 * C_HEADS), vec, vec],
        out_specs=pl.BlockSpec((seg, HEAD_DIM), lambda b, h: (b, h)),
        out_shape=jax.ShapeDtypeStruct((t, C_HEADS * HEAD_DIM), BF16),
        scratch_shapes=[
            pltpu.VMEM((seg, HEAD_DIM), F32),
            pltpu.VMEM((seg, 2 * HEAD_DIM), BF16),
            pltpu.VMEM((nchunk, HEAD_DIM, 2 * HEAD_DIM), F32),
            pltpu.VMEM((nchunk, 1, 2 * HEAD_DIM), F32),
            pltpu.VMEM((nchunk, HEAD_DIM, 2 * HEAD_DIM), BF16),
        ],
        compiler_params=_cp(("arbitrary", "arbitrary")),
        name="hgrn2",
    )(proj, proj, proj, proj, proj, lb, gn_gain)


def _out_kernel(*refs, n_in):
    xs = refs[:n_in]
    ws = refs[n_in:2 * n_in]
    z_ref, gate_ref, gain_ref, sh_ref, sc_ref, wr_ref, br_ref = refs[2 * n_in:2 * n_in + 7]
    z_out, h_out, lg_out = refs[2 * n_in + 7:]
    o = _dot(xs[0][...], ws[0][...])
    for x_ref, w_ref in zip(xs[1:], ws[1:]):
        o = o + _dot(x_ref[...], w_ref[...])
    z1 = z_ref[...] + gate_ref[...] * o
    z_out[...] = z1
    h = _rms_rows(z1) * gain_ref[...]
    h = h * (1.0 + sc_ref[...]) + sh_ref[...]
    hb = h.astype(BF16)
    h_out[...] = hb
    lg_out[...] = _dot(hb, wr_ref[...]) + br_ref[...]


def _out_proj(xs, ws, z, mods, gain, wr, br, bsz, nseg, off, rb):
    d = z.shape[1]
    nout = nseg - off
    n_in = len(xs)
    npad = wr.shape[1]

    def rin(b, j):
        return (b * nseg + off + j, 0)

    def rout(b, j):
        return (b * nout + j, 0)

    def mod(which):
        return pl.BlockSpec((None, None, 1, d),
                            lambda b, j: (jnp.where(j + off == 0, bsz, b), which, 0, 0))

    in_specs = [pl.BlockSpec((rb, x.shape[1]), rin) for x in xs]
    in_specs += [pl.BlockSpec(w.shape, lambda b, j: (0, 0)) for w in ws]
    in_specs += [
        pl.BlockSpec((rb, d), rin),
        mod(2),
        pl.BlockSpec((1, d), lambda b, j: (0, 0)),
        mod(3), mod(4),
        pl.BlockSpec(wr.shape, lambda b, j: (0, 0)),
        pl.BlockSpec((1, npad), lambda b, j: (0, 0)),
    ]
    tm = bsz * nout * rb
    return pl.pallas_call(
        functools.partial(_out_kernel, n_in=n_in),
        grid=(bsz, nout),
        in_specs=in_specs,
        out_specs=[pl.BlockSpec((rb, d), rout), pl.BlockSpec((rb, d), rout), pl.BlockSpec((rb, npad), rout)],
        out_shape=[jax.ShapeDtypeStruct((tm, d), F32), jax.ShapeDtypeStruct((tm, d), BF16),
                   jax.ShapeDtypeStruct((tm, npad), F32)],
        compiler_params=_cp(("arbitrary", "arbitrary")),
        name="out_proj",
    )(*xs, *ws, z, mods, gain, mods, mods, wr, br)


def _topk_kernel(lg_ref, idx_ref, gate_ref):
    l = lg_ref[...]
    lane = lax.broadcasted_iota(jnp.int32, l.shape, 1)
    vals, idxs = [], []
    for _ in range(TOP_K):
        m = jnp.max(l, axis=-1, keepdims=True)
        i = jnp.min(jnp.where(l == m, lane, LANES), axis=-1, keepdims=True)
        vals.append(m)
        idxs.append(i)
        l = jnp.where(lane == i, -jnp.inf, l)
    es = [jnp.exp(v - vals[0]) for v in vals]
    den = es[0] + es[1] + es[2] + es[3]
    io = jnp.zeros(l.shape, jnp.int32)
    go = jnp.zeros(l.shape, F32)
    for k in range(TOP_K):
        io = jnp.where(lane == k, idxs[k], io)
        go = jnp.where(lane == k, es[k] / den, go)
    idx_ref[...] = io
    gate_ref[...] = go


def _topk(logits, tm=512):
    t, n = logits.shape
    tm = _tile(t, tm)
    return pl.pallas_call(
        _topk_kernel,
        grid=(t // tm,),
        in_specs=[pl.BlockSpec((tm, n), lambda i: (i, 0))],
        out_specs=[pl.BlockSpec((tm, n), lambda i: (i, 0)), pl.BlockSpec((tm, n), lambda i: (i, 0))],
        out_shape=[jax.ShapeDtypeStruct((t, n), jnp.int32), jax.ShapeDtypeStruct((t, n), F32)],
        compiler_params=_cp(("arbitrary",)),
        name="router_topk",
    )(logits)


def _gather_kernel(idx0_ref, idxn_ref, src_ref, out_ref, buf, gsem, osem, *, chunk):
    i = pl.program_id(0)
    n = pl.num_programs(0)
    slot = i % 2
    other = 1 - slot

    def out_copy(step, s):
        return pltpu.make_async_copy(buf.at[s], out_ref.at[pl.ds(step * chunk, chunk)], osem.at[s])

    def issue_rows(idx_ref, s):
        def body(p, carry):
            r = 2 * p
            pltpu.make_async_copy(src_ref.at[idx_ref[0, r]], buf.at[s, r], gsem.at[s]).start(priority=0)
            pltpu.make_async_copy(src_ref.at[idx_ref[0, r + 1]], buf.at[s, r + 1], gsem.at[s]).start(priority=1)
            return carry

        lax.fori_loop(0, chunk // 2, body, 0, unroll=4)

    @pl.when(i == 0)
    def _first():
        issue_rows(idx0_ref, 0)

    @pl.when(i + 1 < n)
    def _next():
        @pl.when(i >= 1)
        def _free_slot():
            out_copy(i - 1, other).wait()

        issue_rows(idxn_ref, other)

    pltpu.make_async_copy(src_ref.at[pl.ds(0, chunk)], buf.at[slot], gsem.at[slot]).wait()
    out_copy(i, slot).start()

    @pl.when(i == n - 1)
    def _flush():
        out_copy(i, slot).wait()

        @pl.when(i >= 1)
        def _prev():
            out_copy(i - 1, other).wait()


def _gather_rows(src, idx, chunk=GATHER_CHUNK):
    r = idx.shape[0]
    d = src.shape[1:]
    nchunks = r // chunk
    assert src.shape[0] >= chunk
    idx3 = idx.reshape(nchunks, 1, chunk)
    return pl.pallas_call(
        functools.partial(_gather_kernel, chunk=chunk),
        grid=(nchunks,),
        in_specs=[
            pl.BlockSpec((None, 1, chunk), lambda i: (0, 0, 0), memory_space=pltpu.SMEM),
            pl.BlockSpec((None, 1, chunk), lambda i: (jnp.minimum(i + 1, nchunks - 1), 0, 0),
                         memory_space=pltpu.SMEM),
            pl.BlockSpec(memory_space=pl.ANY),
        ],
        out_specs=pl.BlockSpec(memory_space=pl.ANY),
        out_shape=jax.ShapeDtypeStruct((r,) + d, src.dtype),
        scratch_shapes=[pltpu.VMEM((2, chunk) + d, src.dtype), pltpu.SemaphoreType.DMA((2,)),
                        pltpu.SemaphoreType.DMA((2,))],
        compiler_params=_cp(("arbitrary",)),
        name="row_gather",
    )(idx3, idx3, src)


def _scatter_kernel(dest_ref, src_ref, init_ref, out_ref, sem, *, chunk, fan):
    del init_ref

    def body(t, carry):
        for k in range(fan):
            pltpu.make_async_copy(src_ref.at[t], out_ref.at[dest_ref[0, t * fan + k]], sem).start(priority=k % 2)
        return carry

    lax.fori_loop(0, chunk, body, 0, unroll=2)
    for _ in range(fan):
        pltpu.make_async_copy(src_ref, out_ref.at[pl.ds(0, chunk)], sem).wait()


def _scatter_rows(src, dest, n_out, fan, chunk=GATHER_CHUNK):
    s = src.shape[0]
    d = src.shape[1:]
    chunk = _tile(s, chunk)
    nchunks = s // chunk
    assert n_out >= chunk
    return pl.pallas_call(
        functools.partial(_scatter_kernel, chunk=chunk, fan=fan),
        grid=(nchunks,),
        in_specs=[
            pl.BlockSpec((None, 1, chunk * fan), lambda i: (i, 0, 0), memory_space=pltpu.SMEM),
            pl.BlockSpec((chunk,) + d, lambda i: (i, 0, 0)),
            pl.BlockSpec(memory_space=pl.ANY),
        ],
        out_specs=pl.BlockSpec(memory_space=pl.ANY),
        out_shape=jax.ShapeDtypeStruct((n_out,) + d, src.dtype),
        scratch_shapes=[pltpu.SemaphoreType.DMA],
        input_output_aliases={2: 0},
        compiler_params=_cp(("arbitrary",)),
        name="row_scatter",
    )(dest.reshape(nchunks, 1, chunk * fan), src, jnp.zeros((n_out,) + d, src.dtype))


def _gmm1_kernel(be_ref, first_ref, nused_ref, x_ref, w_ref, b_ref, o_ref, wsc, hsc):
    i = pl.program_id(1)
    nu = nused_ref[0]
    tn = w_ref.shape[1]

    def matmul(slot):
        hsc[slot] = _dot(x_ref[...], wsc[...]) + b_ref[...]

    def activate(slot):
        w2 = 2 * LANES
        bm = o_ref.shape[0]
        lane = lax.broadcasted_iota(jnp.int32, (bm, LANES), 1)
        even = (2 * lane) % LANES
        for s in range(tn // w2):
            parts = []
            for u in range(2):
                lo = s * w2 + u * LANES
                hid = hsc[slot, :, lo:lo + LANES]
                glu = jnp.minimum(hid, SWIGLU_LIMIT)
                glu = glu * _sigmoid(SWIGLU_ALPHA * glu)
                lin = jnp.clip(hid, -SWIGLU_LIMIT, SWIGLU_LIMIT) + 1.0
                prod = glu * pltpu.roll(lin, LANES - 1, 1)
                parts.append(jnp.take_along_axis(prod, even, axis=1))
            o_ref[:, s * LANES:(s + 1) * LANES] = jnp.where(
                lane < LANES // 2, parts[0], parts[1]).astype(o_ref.dtype)

    @pl.when((i < nu) & (first_ref[jnp.minimum(i, nu - 1)] == 1))
    def _cast():
        wsc[...] = w_ref[...].astype(BF16)

    @pl.when((i == 0) & (nu > 0))
    def _head():
        matmul(0)

    for par in range(2):
        @pl.when((i >= 1) & (i < nu) & (i % 2 == par))
        def _steady():
            matmul(par)
            activate(1 - par)

        @pl.when((i >= 1) & (i == nu) & (i % 2 == par))
        def _tail():
            activate(1 - par)

    @pl.when(i > nu)
    def _unused():
        o_ref[...] = jnp.zeros(o_ref.shape, o_ref.dtype)


def _gmm2_kernel(be_ref, first_ref, nused_ref, x_ref, w_ref, b_ref, o_ref, wsc):
    i = pl.program_id(1)

    @pl.when(i < nused_ref[0])
    def _():
        @pl.when(first_ref[i] == 1)
        def _cast():
            wsc[...] = w_ref[...].astype(BF16)

        o_ref[...] = (_dot(x_ref[...], wsc[...]) + b_ref[...]).astype(o_ref.dtype)

    @pl.when(i >= nused_ref[0])
    def _unused():
        o_ref[...] = jnp.zeros(o_ref.shape, o_ref.dtype)


def _gmm(kernel, x, w, b, layer, be, first, nused, bm, tn, out_cols, out_dtype, name, skew=False):
    r, k = x.shape
    n = w.shape[3]
    tn = _tile(n, tn)
    nb = r // bm
    n_tiles = n // tn
    oc = out_cols // n_tiles
    lag = 1 if skew else 0

    def blk(i, nu):
        return jnp.minimum(i, nu[0] - 1)

    scratch = [pltpu.VMEM((k, tn), BF16)]
    if skew:
        scratch.append(pltpu.VMEM((2, bm, tn), F32))
    grid_spec = pltpu.PrefetchScalarGridSpec(
        num_scalar_prefetch=3,
        grid=(n_tiles, nb + lag),
        in_specs=[
            pl.BlockSpec((bm, k), lambda j, i, be, fi, nu: (blk(i, nu), 0)),
            pl.BlockSpec((None, None, k, tn), lambda j, i, be, fi, nu: (layer, be[blk(i, nu)], 0, j)),
            pl.BlockSpec((None, None, 1, tn), lambda j, i, be, fi, nu: (layer, be[blk(i, nu)], 0, j)),
        ],
        out_specs=pl.BlockSpec((bm, oc), lambda j, i, be, fi, nu: (jnp.maximum(i - lag, 0), j)),
        scratch_shapes=scratch,
    )
    return pl.pallas_call(
        kernel,
        grid_spec=grid_spec,
        out_shape=jax.ShapeDtypeStruct((r, out_cols), out_dtype),
        compiler_params=_cp(("arbitrary", "arbitrary")),
        name=name,
    )(be, first, nused, x, w, b.reshape(b.shape[0], b.shape[1], 1, n))


def _combine_kernel(z_ref, y0_ref, y1_ref, y2_ref, y3_ref, gt_ref, gate_ref, gain_ref, sh_ref, sc_ref,
                    *outs, final):
    gt = gt_ref[...]
    shape2d = z_ref.shape
    ffn = gt[:, 0:1] * y0_ref[...].reshape(shape2d).astype(F32)
    for k, y_ref in enumerate((y1_ref, y2_ref, y3_ref), start=1):
        ffn = ffn + gt[:, k:k + 1] * y_ref[...].reshape(shape2d).astype(F32)
    z2 = z_ref[...] + gate_ref[...] * ffn
    h = _rms_rows(z2) * gain_ref[...]
    if final:
        outs[0][...] = h
    else:
        outs[0][...] = z2
        outs[1][...] = (h * (1.0 + sc_ref[...]) + sh_ref[...]).astype(outs[1].dtype)


def _combine(z1, yt, gates, mods_cur, mods_next, gain, bsz, nblk, rb, final):
    t, d = z1.shape
    has_ctx = not final

    def midx(i):
        if has_ctx:
            return jnp.where(i % nblk == 0, bsz, i // nblk)
        return i // nblk

    row = lambda i: (i, 0)
    nrow = t // rb
    out_specs = [pl.BlockSpec((rb, d), row)]
    out_shape = [jax.ShapeDtypeStruct((t, d), F32)]
    if not final:
        out_specs.append(pl.BlockSpec((rb, d), row))
        out_shape.append(jax.ShapeDtypeStruct((t, d), BF16))
    y_specs = [pl.BlockSpec((rb,) + yt.shape[1:], functools.partial(lambda i, k: (k * nrow + i, 0, 0), k=k))
               for k in range(TOP_K)]
    return pl.pallas_call(
        functools.partial(_combine_kernel, final=final),
        grid=(nrow,),
        in_specs=[
            pl.BlockSpec((rb, d), row),
            *y_specs,
            pl.BlockSpec((rb, gates.shape[1]), row),
            pl.BlockSpec((None, None, 1, d), lambda i: (midx(i), 5, 0, 0)),
            pl.BlockSpec((1, d), lambda i: (0, 0)),
            pl.BlockSpec((None, None, 1, d), lambda i: (midx(i), 0, 0, 0)),
            pl.BlockSpec((None, None, 1, d), lambda i: (midx(i), 1, 0, 0)),
        ],
        out_specs=out_specs,
        out_shape=out_shape,
        compiler_params=_cp(("arbitrary",)),
        name="moe_combine",
    )(z1, yt, yt, yt, yt, gates, mods_cur, gain, mods_next, mods_next)


def _routing(idx, n_exp, bm, gchunk):
    tm = idx.shape[0]
    onehot = (idx[:, :, None] == jnp.arange(n_exp, dtype=jnp.int32)).astype(jnp.int32).sum(axis=1)
    csum = jnp.cumsum(onehot, axis=0)
    counts = csum[-1]
    rank = jnp.take_along_axis(csum - onehot, idx, axis=1)
    padded = (counts + bm - 1) // bm * bm
    pad_end = jnp.cumsum(padded)
    pad_start = pad_end - padded
    dest = pad_start[idx] + rank
    nb = (tm * TOP_K + n_exp * (bm - 1)) // bm
    nb = -(-(nb * bm) // gchunk) * gchunk // bm
    nused = (pad_end[-1] // bm).astype(jnp.int32).reshape(1)
    block_start = jnp.arange(nb, dtype=jnp.int32) * bm
    be = jnp.minimum((pad_end[None, :] <= block_start[:, None]).astype(jnp.int32).sum(axis=1), n_exp - 1)
    first = jnp.concatenate([jnp.ones((1,), jnp.int32), (be[1:] != be[:-1]).astype(jnp.int32)])
    dest = dest.astype(jnp.int32)
    dest_kmajor = dest.T.reshape(-1)
    return dest.reshape(-1), dest_kmajor, nb * bm, be, first, nused


def _moe(h2, logits, w1, b1, w2, b2, layer):
    tm, d = h2.shape
    n_exp = w1.shape[1]
    sub = d // LANES
    idx_p, gate_p = _topk(logits)
    idx = idx_p[:, :TOP_K]
    dest_tmajor, dest, n_rows, be, first, nused = _routing(idx, n_exp, MOE_BM, GATHER_CHUNK)
    xs = _scatter_rows(h2.reshape(tm, sub, LANES), dest_tmajor, n_rows, TOP_K).reshape(-1, d)
    act = _gmm(_gmm1_kernel, xs, w1, b1, layer, be, first, nused, MOE_BM, 1024, w1.shape[3] // 2, BF16, "moe_up",
               skew=True)
    y = _gmm(_gmm2_kernel, act, w2, b2, layer, be, first, nused, MOE_BM, 2048, w2.shape[3], BF16, "moe_down")
    yt = _gather_rows(y.reshape(-1, sub, LANES), dest)
    return yt, gate_p


def _rope_tables(n_lat, lc):
    n_rows = n_lat // GRID_W
    row = jnp.repeat(jnp.arange(n_rows, dtype=F32), GRID_W)
    col = jnp.tile(jnp.arange(GRID_W, dtype=F32), n_rows)
    n_freq = HEAD_DIM // 4
    inv_freq = ROPE_THETA ** (-jnp.arange(n_freq, dtype=F32) / n_freq)
    ang = jnp.concatenate([row[:, None] * inv_freq, col[:, None] * inv_freq], axis=-1)
    cos, sin = jnp.cos(ang), jnp.sin(ang)
    cos2 = jnp.concatenate([cos, cos], axis=-1)
    sin2 = jnp.concatenate([-sin, sin], axis=-1)
    cos2 = jnp.concatenate([jnp.ones((lc, HEAD_DIM), F32), cos2], axis=0)
    sin2 = jnp.concatenate([jnp.zeros((lc, HEAD_DIM), F32), sin2], axis=0)
    return cos2, sin2


def kernel(x, c, ctx, c_ctx, ada_w, ada_b, norm_mix, norm_ffn, ab_w_in, ab_w_out, a_q_norm, a_k_norm, b_decay_exp, b_gn, c_w_in, c_w_out, c_lb, c_gn, router_w, router_b, exp_w1, exp_b1, exp_w2, exp_b2, norm_final):
    bsz, n_lat, d = x.shape
    lc = ctx.shape[1]
    depth = ada_w.shape[0]
    assert depth == 2 and n_lat % lc == 0 and bsz < MOD_ROWS and lc % GLA_CHUNK == 0
    seg = lc + n_lat
    nseg = seg // lc
    n_exp = router_w.shape[2]

    z = jnp.concatenate([ctx, x], axis=1).reshape(bsz * seg, d)

    cond = jnp.zeros((MOD_ROWS, d), F32).at[:bsz].set(c).at[bsz].set(c_ctx)
    mods = _ada(cond, ada_w, ada_b).reshape(depth, MOD_ROWS, N_MOD, 1, d)
    cos2, sin2 = _rope_tables(n_lat, lc)

    lb_soft = jax.nn.softmax(c_lb.astype(F32), axis=0)
    lower_bounds = jnp.cumsum(lb_soft, axis=0) - lb_soft[0]
    log_g = jnp.log1p(-jnp.exp2(-b_decay_exp[0].astype(F32)))

    wr = jnp.zeros((depth, d, LANES), BF16).at[:, :, :n_exp].set(router_w.astype(BF16))
    br = jnp.full((depth, 1, LANES), -1e30, F32).at[:, 0, :n_exp].set(router_b)

    hz = _norm_mod(z, norm_mix[0:1], mods[0], nseg, bsz, lc)
    proj = _matmul(hz, ab_w_in[0].astype(BF16), 1024, 512, name="in_proj0")
    att = _attention(proj, cos2, sin2, a_q_norm[0:1], a_k_norm[0:1], bsz, seg, lc)
    ret = _retention(proj, log_g, cos2, sin2, b_gn[0:1], bsz, seg, lc)
    w_out = ab_w_out[0].astype(BF16)
    aw = A_Q_HEADS * HEAD_DIM
    z1, h2, logits = _out_proj([att, ret], [w_out[:aw], w_out[aw:]], z, mods[0], norm_ffn[0:1],
                               wr[0], br[0], bsz, nseg, 0, lc)
    yt, gates = _moe(h2, logits, exp_w1, exp_b1, exp_w2, exp_b2, 0)
    z2, hz = _combine(z1, yt, gates, mods[0], mods[1], norm_mix[1:2], bsz, nseg, lc, final=False)

    proj = _matmul(hz, c_w_in[0].astype(BF16), 1024, 1024, name="in_proj1")
    hg = _hgrn(proj, lower_bounds[1:2], c_gn[0:1], bsz, seg, lc)
    z1, h2, logits = _out_proj([hg], [c_w_out[0].astype(BF16)], z2, mods[1], norm_ffn[1:2],
                               wr[1], br[1], bsz, nseg, 1, lc)
    yt, gates = _moe(h2, logits, exp_w1, exp_b1, exp_w2, exp_b2, 1)
    (out,) = _combine(z1, yt, gates, mods[1], mods[1], norm_final.reshape(1, d), bsz, nseg - 1, lc, final=True)
    return out.reshape(bsz, n_lat, d)
```
